```python
import math
import functools
import jax
import jax.numpy as jnp
from jax import lax
import numpy as np


D_MODEL = 2048
BATCH = 2
SEQ = 4096
DEPTH = 4
DEC_BATCH = 8
DEC_SEQ = 8
PAST_LEN = 16384
PAGE_SIZE = 128

DA_HEADS = 8
DA_HEAD_DIM = 64
DA_V_DIM = 2 * DA_HEAD_DIM
ML_HEADS = 4
ML_DQK = 256
ML_DV = 256
D_FF = 5632
N_BUCKETS = 32
MAX_DISTANCE = 128
Q_BLOCK = 128
ML_CHUNK = 64
LN_EPS = 1e-5
DN_ALPHA = (2.0 * DEPTH) ** 0.25
DN_BETA = (8.0 * DEPTH) ** -0.25

ATT_Q = DA_HEADS * 2 * DA_HEAD_DIM
ATT_K = DA_HEADS * 2 * DA_HEAD_DIM
ATT_V = DA_HEADS * DA_V_DIM
ML_Q = ML_HEADS * ML_DQK
ML_K = ML_HEADS * ML_DQK
ML_V = ML_HEADS * ML_DV
ML_O = ML_HEADS * ML_DV
ML_IF = 2 * ML_HEADS
N_GATE = 2 * D_MODEL
IN_SIZES = (ATT_Q, ATT_K, ATT_V, ML_Q, ML_K, ML_V, ML_O, ML_IF, N_GATE)
IN_SPLITS = tuple(int(s) for s in np.cumsum(IN_SIZES)[:-1])
N_IN = sum(IN_SIZES)

kernel_name = 'hybrid_diffattn_mlstm_step'


def layer_norm(x, g, b):
    xf = x.astype(jnp.float32)
    xc = xf - jnp.mean(xf, axis=-1, keepdims=True)
    var = jnp.mean(xc * xc, axis=-1, keepdims=True)
    return (xc * lax.rsqrt(var + LN_EPS) * g.astype(jnp.float32) + b.astype(jnp.float32)).astype(x.dtype)


def swiglu(x, w_up, w_down):
    a, b = jnp.split(x @ w_up, 2, axis=-1)
    return (jax.nn.silu(a) * b) @ w_down


def t5_bucket(dist):
    n = jnp.maximum(dist, 0)
    max_exact = N_BUCKETS // 2
    nf = jnp.maximum(n, 1).astype(jnp.float32)
    large = max_exact + (jnp.log(nf / max_exact) / math.log(MAX_DISTANCE / max_exact)
                         * (N_BUCKETS - max_exact)).astype(jnp.int32)
    large = jnp.minimum(large, N_BUCKETS - 1)
    return jnp.where(n < max_exact, n, large)


def diff_attend(q, k, v, bias, allowed, lam, lam_init, subln_w):
    scale = DA_HEAD_DIM ** -0.5
    s = jnp.einsum('bqhjd,bkhjd->bjhqk', q.astype(jnp.float32), k.astype(jnp.float32)) * scale
    s = jnp.where(allowed, s + bias.astype(jnp.float32), -jnp.inf)
    p = jax.nn.softmax(s, axis=-1)
    a = p[:, 0] - lam * p[:, 1]
    o = jnp.einsum('bhqk,bkhd->bqhd', a, v.astype(jnp.float32))
    o = o * lax.rsqrt(jnp.mean(o * o, axis=-1, keepdims=True) + LN_EPS)
    return (o * subln_w.astype(jnp.float32) * (1.0 - lam_init)).astype(v.dtype)


def prompt_attn(rel_bias, q, k, v, lam, lam_init, subln_w):
    B, S = q.shape[0], q.shape[1]
    kpos = jnp.arange(S)

    def block(i):
        start = i * Q_BLOCK
        qb = lax.dynamic_slice_in_dim(q, start, Q_BLOCK, axis=1)
        dist = (start + jnp.arange(Q_BLOCK))[:, None] - kpos[None, :]
        bias = jnp.transpose(rel_bias[t5_bucket(dist)], (2, 0, 1))
        return diff_attend(qb, k, v, bias, dist >= 0, lam, lam_init, subln_w)

    out = lax.map(block, jnp.arange(S // Q_BLOCK))
    return jnp.moveaxis(out, 0, 1).reshape(B, S, DA_HEADS, DA_V_DIM)


def sample_attn(cache_k, cache_v, page_table, rel_bias, layer, q, k, v, lam, lam_init, subln_w):
    Bd, Ts = q.shape[0], q.shape[1]
    past = page_table.shape[1] * PAGE_SIZE
    past_k = cache_k[layer, page_table].reshape(Bd, past, DA_HEADS, 2, DA_HEAD_DIM).astype(k.dtype)
    past_v = cache_v[layer, page_table].reshape(Bd, past, DA_HEADS, DA_V_DIM).astype(v.dtype)
    kk = jnp.concatenate([past_k, k], axis=1)
    vv = jnp.concatenate([past_v, v], axis=1)
    dist = (past + jnp.arange(Ts))[:, None] - jnp.arange(past + Ts)[None, :]
    bias = jnp.transpose(rel_bias[t5_bucket(dist)], (2, 0, 1))
    return diff_attend(q, kk, vv, bias, dist >= 0, lam, lam_init, subln_w)


def mlstm_chunkwise(q, k, v, ig, lf, C0, n0, m0):
    B, T = q.shape[0], q.shape[1]
    L = ML_CHUNK if T % ML_CHUNK == 0 else T
    nc = T // L

    def chunks(a):
        a = a.astype(jnp.float32).reshape((B, nc, L) + a.shape[2:])
        return jnp.swapaxes(jnp.moveaxis(a, 1, 0), 2, 3)

    causal = jnp.tril(jnp.ones((L, L), dtype=bool))

    def step(carry, inp):
        C, n, m = carry
        qc, kc, vc, igc, lfc = inp
        b = jnp.cumsum(lfc, axis=-1)
        logw = jnp.where(causal, b[..., :, None] - b[..., None, :] + igc[..., None, :], -jnp.inf)
        m_inter = b + m[..., None]
        m_t = jnp.maximum(jnp.max(logw, axis=-1), m_inter)
        inter = jnp.exp(m_inter - m_t)
        s = jnp.einsum('bhtd,bhsd->bhts', qc, kc) * jnp.exp(logw - m_t[..., None])
        num = jnp.einsum('bhts,bhsv->bhtv', s, vc) + inter[..., None] * jnp.einsum('bhvd,bhtd->bhtv', C, qc)
        den = jnp.sum(s, axis=-1) + inter * jnp.einsum('bhd,bhtd->bht', n, qc)
        h = num / jnp.maximum(jnp.abs(den), jnp.exp(-m_t))[..., None]
        b_last = b[..., -1]
        logw_end = b_last[..., None] - b + igc
        m_new = jnp.maximum(b_last + m, jnp.max(logw_end, axis=-1))
        w_end = jnp.exp(logw_end - m_new[..., None])
        decay = jnp.exp(b_last + m - m_new)
        C_new = decay[..., None, None] * C + jnp.einsum('bhs,bhsv,bhsd->bhvd', w_end, vc, kc)
        n_new = decay[..., None] * n + jnp.einsum('bhs,bhsd->bhd', w_end, kc)
        return (C_new, n_new, m_new), h

    init = (C0.astype(jnp.float32), n0.astype(jnp.float32), m0.astype(jnp.float32))
    (C, n, m), hs = lax.scan(step, init, (chunks(q), chunks(k), chunks(v), chunks(ig), chunks(lf)))
    h = jnp.moveaxis(jnp.swapaxes(hs, 2, 3), 0, 1).reshape(B, T, ML_HEADS, ML_DV)
    return h, C, n, m


def decoder_layer(x, attn_core, C0, n0, m0, lam_init, w_in, b_gate, b_if, lam_qk, subln_w,
                  mlstm_norm_w, w_branch_attn, w_branch_mlstm, w_out, ffn_up, ffn_down, ln_g, ln_b):
    B, T = x.shape[0], x.shape[1]
    x = layer_norm(DN_ALPHA * x + 0.5 * swiglu(x, ffn_up[0], ffn_down[0]), ln_g[0], ln_b[0])
    aq, ak, av, mq, mk, mv, mo, mif, gates = jnp.split(x @ w_in, IN_SPLITS, axis=-1)
    aq = aq.reshape(B, T, DA_HEADS, 2, DA_HEAD_DIM)
    ak = ak.reshape(B, T, DA_HEADS, 2, DA_HEAD_DIM)
    av = av.reshape(B, T, DA_HEADS, DA_V_DIM)
    lq = lam_qk.astype(jnp.float32)
    lam = jnp.exp(jnp.sum(lq[0] * lq[1])) - jnp.exp(jnp.sum(lq[2] * lq[3])) + lam_init
    ao = attn_core(aq, ak, av, lam, lam_init, subln_w)
    gif = mif.astype(jnp.float32) + b_if.astype(jnp.float32)
    ig = gif[..., :ML_HEADS]
    lf = jax.nn.log_sigmoid(gif[..., ML_HEADS:])
    mh, C, n, m = mlstm_chunkwise(mq.reshape(B, T, ML_HEADS, ML_DQK),
                                  mk.reshape(B, T, ML_HEADS, ML_DQK) * (ML_DQK ** -0.5),
                                  mv.reshape(B, T, ML_HEADS, ML_DV), ig, lf, C0, n0, m0)
    mc = mh - jnp.mean(mh, axis=-1, keepdims=True)
    mh = mc * lax.rsqrt(jnp.mean(mc * mc, axis=-1, keepdims=True) + LN_EPS) \
        * mlstm_norm_w.reshape(ML_HEADS, ML_DV).astype(jnp.float32)
    mh = (jax.nn.sigmoid(mo.astype(jnp.float32)) * mh.reshape(B, T, ML_V)).astype(x.dtype)
    g = jax.nn.sigmoid(gates + b_gate)
    merged = g[..., :D_MODEL] * (ao.reshape(B, T, ATT_V).astype(x.dtype) @ w_branch_attn) \
        + g[..., D_MODEL:] * (mh @ w_branch_mlstm)
    x = layer_norm(DN_ALPHA * x + merged @ w_out, ln_g[1], ln_b[1])
    x = layer_norm(DN_ALPHA * x + 0.5 * swiglu(x, ffn_up[1], ffn_down[1]), ln_g[2], ln_b[2])
    k_rows = ak.reshape(B, T, DA_HEADS, 2 * DA_HEAD_DIM)
    return x, k_rows, av, C.astype(C0.dtype), n.astype(n0.dtype), m.astype(m0.dtype)


def setup_inputs(seed: int = 0) -> dict:
    key = jax.random.key(seed)
    ks = jax.random.split(key, 24)
    nrm = jax.random.normal
    n_pages = PAST_LEN // PAGE_SIZE
    n_used = DEC_BATCH * n_pages
    n_pool = n_used + max(1, n_used // 4)
    page_table = jax.random.permutation(ks[0], n_pool)[:n_used].reshape(DEC_BATCH, n_pages).astype(jnp.int32)
    x_prompt = nrm(ks[1], (BATCH, SEQ, D_MODEL), jnp.float32)
    x_sample = nrm(ks[2], (DEC_BATCH, DEC_SEQ, D_MODEL), jnp.float32)
    cache_k = nrm(ks[3], (DEPTH, n_pool, PAGE_SIZE, DA_HEADS, 2 * DA_HEAD_DIM), jnp.float32)
    cache_v = nrm(ks[4], (DEPTH, n_pool, PAGE_SIZE, DA_HEADS, DA_V_DIM), jnp.float32)
    state_C = 0.05 * nrm(ks[5], (DEPTH, DEC_BATCH, ML_HEADS, ML_DV, ML_DQK), jnp.float32)
    state_n = 0.05 * nrm(ks[6], (DEPTH, DEC_BATCH, ML_HEADS, ML_DQK), jnp.float32)
    state_m = 0.5 * nrm(ks[7], (DEPTH, DEC_BATCH, ML_HEADS), jnp.float32)
    rel_bias = 0.5 * nrm(ks[8], (N_BUCKETS, DA_HEADS), jnp.float32)
    col_scale = jnp.concatenate([
        jnp.ones((ATT_Q + ATT_K,), jnp.float32), jnp.full((ATT_V,), DN_BETA, jnp.float32),
        jnp.ones((ML_Q + ML_K,), jnp.float32), jnp.full((ML_V,), DN_BETA, jnp.float32),
        jnp.ones((ML_O + ML_IF + N_GATE,), jnp.float32)])
    w_in = nrm(ks[9], (DEPTH, D_MODEL, N_IN), jnp.float32) * (D_MODEL ** -0.5) * col_scale
    b_gate = 0.1 * nrm(ks[10], (DEPTH, N_GATE), jnp.float32)
    b_if = jnp.concatenate([
        0.1 * nrm(ks[11], (DEPTH, ML_HEADS), jnp.float32),
        jnp.broadcast_to(jnp.linspace(3.0, 6.0, ML_HEADS), (DEPTH, ML_HEADS))
        + 0.1 * nrm(ks[12], (DEPTH, ML_HEADS), jnp.float32)], axis=-1)
    lam_qk = 0.1 * nrm(ks[13], (DEPTH, 4, DA_HEAD_DIM), jnp.float32)
    subln_w = 1.0 + 0.05 * nrm(ks[14], (DEPTH, DA_V_DIM), jnp.float32)
    mlstm_norm_w = 1.0 + 0.05 * nrm(ks[15], (DEPTH, ML_V), jnp.float32)
    w_branch_attn = nrm(ks[16], (DEPTH, ATT_V, D_MODEL), jnp.float32) * (ATT_V ** -0.5 * DN_BETA)
    w_branch_mlstm = nrm(ks[17], (DEPTH, ML_V, D_MODEL), jnp.float32) * (ML_V ** -0.5 * DN_BETA)
    w_out = nrm(ks[18], (DEPTH, D_MODEL, D_MODEL), jnp.float32) * (D_MODEL ** -0.5 * DN_BETA)
    ffn_up = nrm(ks[19], (DEPTH, 2, D_MODEL, 2 * D_FF), jnp.float32) * (D_MODEL ** -0.5 * DN_BETA)
    ffn_down = nrm(ks[20], (DEPTH, 2, D_FF, D_MODEL), jnp.float32) * (D_FF ** -0.5 * DN_BETA)
    ln_g = 1.0 + 0.05 * nrm(ks[21], (DEPTH, 3, D_MODEL), jnp.float32)
    ln_b = 0.02 * nrm(ks[22], (DEPTH, 3, D_MODEL), jnp.float32)
    return {'x_prompt': x_prompt, 'x_sample': x_sample, 'cache_k': cache_k, 'cache_v': cache_v,
            'page_table': page_table, 'state_C': state_C, 'state_n': state_n, 'state_m': state_m,
            'rel_bias': rel_bias, 'w_in': w_in, 'b_gate': b_gate, 'b_if': b_if, 'lam_qk': lam_qk,
            'subln_w': subln_w, 'mlstm_norm_w': mlstm_norm_w, 'w_branch_attn': w_branch_attn,
            'w_branch_mlstm': w_branch_mlstm, 'w_out': w_out, 'ffn_up': ffn_up, 'ffn_down': ffn_down,
            'ln_g': ln_g, 'ln_b': ln_b}


def reference(x_prompt, x_sample, cache_k, cache_v, page_table, state_C, state_n, state_m, rel_bias,
              w_in, b_gate, b_if, lam_qk, subln_w, mlstm_norm_w, w_branch_attn, w_branch_mlstm,
              w_out, ffn_up, ffn_down, ln_g, ln_b):
    B = x_prompt.shape[0]
    xp = x_prompt
    xs = x_sample
    zero_C = jnp.zeros((B, ML_HEADS, ML_DV, ML_DQK), x_prompt.dtype)
    zero_n = jnp.zeros((B, ML_HEADS, ML_DQK), x_prompt.dtype)
    zero_m = jnp.zeros((B, ML_HEADS), x_prompt.dtype)
    prompt_core = functools.partial(prompt_attn, rel_bias)
    kp_list, vp_list, ks_list, vs_list = [], [], [], []
    Cp_list, np_list, mp_list, Cs_list, ns_list, ms_list = [], [], [], [], [], []
    for l in range(DEPTH):
        lam_init = 0.8 - 0.6 * math.exp(-0.3 * l)
        sample_core = functools.partial(sample_attn, cache_k, cache_v, page_table, rel_bias, l)
        weights = (w_in[l], b_gate[l], b_if[l], lam_qk[l], subln_w[l], mlstm_norm_w[l],
                   w_branch_attn[l], w_branch_mlstm[l], w_out[l], ffn_up[l], ffn_down[l], ln_g[l], ln_b[l])
        xp, k_new, v_new, C_new, n_new, m_new = decoder_layer(xp, prompt_core, zero_C, zero_n, zero_m,
                                                              lam_init, *weights)
        kp_list.append(k_new)
        vp_list.append(v_new)
        Cp_list.append(C_new)
        np_list.append(n_new)
        mp_list.append(m_new)
        xs, k_new, v_new, C_new, n_new, m_new = decoder_layer(xs, sample_core, state_C[l], state_n[l],
                                                              state_m[l], lam_init, *weights)
        ks_list.append(k_new)
        vs_list.append(v_new)
        Cs_list.append(C_new)
        ns_list.append(n_new)
        ms_list.append(m_new)
    return (xp, xs, jnp.stack(kp_list), jnp.stack(vp_list), jnp.stack(ks_list), jnp.stack(vs_list),
            jnp.stack(Cp_list), jnp.stack(np_list), jnp.stack(mp_list),
            jnp.stack(Cs_list), jnp.stack(ns_list), jnp.stack(ms_list))
```

```python
import functools
import math

import numpy as np
import jax
import jax.numpy as jnp
from jax import lax
from jax.experimental import pallas as pl
from jax.experimental.pallas import tpu as pltpu

F32 = jnp.float32
BF16 = jnp.bfloat16

D_MODEL = 2048
DEPTH = 4
PAGE_SIZE = 128
DA_HEADS = 8
DA_HEAD_DIM = 64
DA_V_DIM = 2 * DA_HEAD_DIM
ML_HEADS = 4
ML_DQK = 256
ML_DV = 256
D_FF = 5632
N_BUCKETS = 32
MAX_DISTANCE = 128
LN_EPS = 1e-5
DN_ALPHA = (2.0 * DEPTH) ** 0.25

ATT_W = DA_HEADS * DA_V_DIM
ML_W = ML_HEADS * ML_DV
N_MAIN = 3 * ATT_W + 4 * ML_W
N_GATE = 2 * D_MODEL
N_IF = 2 * ML_HEADS
N_IF_PAD = 256
N_PROJ = N_MAIN + N_GATE + N_IF_PAD
COL_GATE = N_MAIN
COL_IF = N_MAIN + N_GATE

NEG_BIG = -1e30
VMEM_LIMIT = 48 * 1024 * 1024


def _cparams(sem):
    return pltpu.CompilerParams(dimension_semantics=sem, vmem_limit_bytes=VMEM_LIMIT)


def _layer_norm_rows(y, g, b):
    mu = jnp.mean(y, axis=-1, keepdims=True)
    yc = y - mu
    var = jnp.mean(yc * yc, axis=-1, keepdims=True)
    return yc * lax.rsqrt(var + LN_EPS) * g + b


def _ffn_kernel(x_ref, xb_ref, wa_ref, wb_ref, wd_ref, g_ref, b_ref, o_ref, ob_ref, acc_ref):
    j = pl.program_id(1)

    @pl.when(j == 0)
    def _():
        acc_ref[...] = jnp.zeros_like(acc_ref)

    xb = xb_ref[...]
    a = jnp.dot(xb, wa_ref[...], preferred_element_type=F32)
    b = jnp.dot(xb, wb_ref[...], preferred_element_type=F32)
    h = (a * jax.nn.sigmoid(a) * b).astype(BF16)
    acc_ref[...] += jnp.dot(h, wd_ref[...], preferred_element_type=F32)

    @pl.when(j == pl.num_programs(1) - 1)
    def _():
        y = DN_ALPHA * x_ref[...] + 0.5 * acc_ref[...]
        o = _layer_norm_rows(y, g_ref[...], b_ref[...])
        o_ref[...] = o
        ob_ref[...] = o.astype(BF16)


def _ffn_ln(x, xb, w_up, w_down, ln_g, ln_b, layer, which, ln_idx, tm, tf):
    m = x.shape[0]
    nf = D_FF // tf
    return pl.pallas_call(
        _ffn_kernel,
        grid=(m // tm, nf),
        in_specs=[
            pl.BlockSpec((tm, D_MODEL), lambda i, j: (i, 0)),
            pl.BlockSpec((tm, D_MODEL), lambda i, j: (i, 0)),
            pl.BlockSpec((None, None, D_MODEL, tf), lambda i, j: (layer, which, 0, j)),
            pl.BlockSpec((None, None, D_MODEL, tf), lambda i, j: (layer, which, 0, j + nf)),
            pl.BlockSpec((None, None, tf, D_MODEL), lambda i, j: (layer, which, j, 0)),
            pl.BlockSpec((None, None, 1, D_MODEL), lambda i, j: (layer, ln_idx, 0, 0)),
            pl.BlockSpec((None, None, 1, D_MODEL), lambda i, j: (layer, ln_idx, 0, 0)),
        ],
        out_specs=[
            pl.BlockSpec((tm, D_MODEL), lambda i, j: (i, 0)),
            pl.BlockSpec((tm, D_MODEL), lambda i, j: (i, 0)),
        ],
        out_shape=[jax.ShapeDtypeStruct((m, D_MODEL), F32),
                   jax.ShapeDtypeStruct((m, D_MODEL), BF16)],
        scratch_shapes=[pltpu.VMEM((tm, D_MODEL), F32)],
        compiler_params=_cparams(("parallel", "arbitrary")),
        name="ffn_ln",
    )(x, xb, w_up, w_up, w_down, ln_g, ln_b)


def _proj_kernel(xb_ref, w_ref, o_ref):
    o_ref[...] = jnp.dot(xb_ref[...], w_ref[...], preferred_element_type=F32)


def _in_proj(xb, w_cat, layer, tm, tn):
    m = xb.shape[0]
    return pl.pallas_call(
        _proj_kernel,
        grid=(m // tm, N_PROJ // tn),
        in_specs=[
            pl.BlockSpec((tm, D_MODEL), lambda i, j: (i, 0)),
            pl.BlockSpec((None, D_MODEL, tn), lambda i, j: (layer, 0, j)),
        ],
        out_specs=pl.BlockSpec((tm, tn), lambda i, j: (i, j)),
        out_shape=jax.ShapeDtypeStruct((m, N_PROJ), F32),
        compiler_params=_cparams(("parallel", "parallel")),
        name="in_proj",
    )(xb, w_cat)


def _t5_bucket_np(dist):
    n = np.maximum(dist, 0)
    max_exact = N_BUCKETS // 2
    nf = np.maximum(n, 1).astype(np.float32)
    large = max_exact + (np.log(nf / np.float32(max_exact)) / np.float32(math.log(MAX_DISTANCE / max_exact))
                         * np.float32(N_BUCKETS - max_exact)).astype(np.int32)
    large = np.minimum(large, N_BUCKETS - 1)
    return np.where(n < max_exact, n, large).astype(np.int32)


def _lambda_scalar(lq, lam_init):
    s01 = jnp.sum(lq[0:1, :] * lq[1:2, :], axis=1, keepdims=True)
    s23 = jnp.sum(lq[2:3, :] * lq[3:4, :], axis=1, keepdims=True)
    return jnp.exp(s01) - jnp.exp(s23) + lam_init


def _sub_norm(o, w_row, lam_init):
    o = o * lax.rsqrt(jnp.mean(o * o, axis=-1, keepdims=True) + LN_EPS)
    return o * w_row * (1.0 - lam_init)


def _pattn_kernel(qi_tab, ki_tab, q_ref, k_ref, v_ref, bias_ref, lq_ref, sw_ref, o_ref,
                  m_sc, l_sc, acc_sc, *, tq, lam_init):
    t = pl.program_id(2)
    qi = qi_tab[t]
    ki = ki_tab[t]

    @pl.when(ki == 0)
    def _():
        m_sc[...] = jnp.full_like(m_sc, -jnp.inf)
        l_sc[...] = jnp.zeros_like(l_sc)
        acc_sc[...] = jnp.zeros_like(acc_sc)

    q = q_ref[...] * (DA_HEAD_DIM ** -0.5)
    lane = lax.broadcasted_iota(jnp.int32, q.shape, 1)
    q1 = jnp.where(lane < DA_HEAD_DIM, q, 0.0)
    q2 = jnp.where(lane >= DA_HEAD_DIM, q, 0.0)
    qz = jnp.concatenate([q1, q2], axis=0).astype(BF16)
    kb = k_ref[...].astype(BF16)
    s = lax.dot_general(qz, kb, (((1,), (1,)), ((), ())), preferred_element_type=F32)
    bias = bias_ref[...]
    s = s + jnp.concatenate([bias, bias], axis=0)

    m_old = m_sc[...]
    m_new = jnp.maximum(m_old, jnp.max(s, axis=1, keepdims=True))
    p = jnp.exp(s - m_new)
    alpha = jnp.exp(m_old - m_new)
    l_sc[...] = alpha * l_sc[...] + jnp.sum(p, axis=1, keepdims=True)
    acc_sc[...] = alpha * acc_sc[...] + jnp.dot(p.astype(BF16), v_ref[...].astype(BF16),
                                                preferred_element_type=F32)
    m_sc[...] = m_new

    @pl.when(ki == qi)
    def _():
        o_all = acc_sc[...] / l_sc[...]
        lam = _lambda_scalar(lq_ref[...], lam_init)
        o = o_all[:tq] - lam * o_all[tq:]
        o_ref[...] = _sub_norm(o, sw_ref[...], lam_init).astype(o_ref.dtype)


def _prompt_attn(proj, bias_tiles, lam_qk, subln_w, layer, lam_init, batch, seq, tq):
    nq = seq // tq
    tri = [(qi, ki) for qi in range(nq) for ki in range(qi + 1)]
    qi_tab = jnp.asarray([a for a, _ in tri], jnp.int32)
    ki_tab = jnp.asarray([b for _, b in tri], jnp.int32)
    hq, hk, hv = 0, ATT_W // DA_V_DIM, 2 * ATT_W // DA_V_DIM

    def bias_idx(b, h, t, qt, kt):
        return (h, jnp.minimum(qt[t] - kt[t], 2), 0, 0)

    grid_spec = pltpu.PrefetchScalarGridSpec(
        num_scalar_prefetch=2,
        grid=(batch, DA_HEADS, len(tri)),
        in_specs=[
            pl.BlockSpec((tq, DA_V_DIM), lambda b, h, t, qt, kt: (b * nq + qt[t], hq + h)),
            pl.BlockSpec((tq, DA_V_DIM), lambda b, h, t, qt, kt: (b * nq + kt[t], hk + h)),
            pl.BlockSpec((tq, DA_V_DIM), lambda b, h, t, qt, kt: (b * nq + kt[t], hv + h)),
            pl.BlockSpec((None, None, tq, tq), bias_idx),
            pl.BlockSpec((None, 4, DA_HEAD_DIM), lambda b, h, t, qt, kt: (layer, 0, 0)),
            pl.BlockSpec((None, 1, DA_V_DIM), lambda b, h, t, qt, kt: (layer, 0, 0)),
        ],
        out_specs=pl.BlockSpec((tq, DA_V_DIM), lambda b, h, t, qt, kt: (b * nq + qt[t], h)),
        scratch_shapes=[pltpu.VMEM((2 * tq, 1), F32), pltpu.VMEM((2 * tq, 1), F32),
                        pltpu.VMEM((2 * tq, DA_V_DIM), F32)],
    )
    return pl.pallas_call(
        functools.partial(_pattn_kernel, tq=tq, lam_init=lam_init),
        grid_spec=grid_spec,
        out_shape=jax.ShapeDtypeStruct((batch * seq, ATT_W), BF16),
        compiler_params=_cparams(("parallel", "parallel", "arbitrary")),
        name="prompt_attn",
    )(qi_tab, ki_tab, proj, proj, proj, bias_tiles, lam_qk, subln_w)


def _prompt_bias_tiles(rel_bias, tq):
    r = np.arange(tq)[:, None] - np.arange(tq)[None, :]
    assert tq >= MAX_DISTANCE and np.all(_t5_bucket_np(np.arange(MAX_DISTANCE, 1 << 16)) == N_BUCKETS - 1)
    tiles = []
    for delta in range(3):
        dist = r + delta * tq
        bias = jnp.transpose(rel_bias[_t5_bucket_np(dist)], (2, 0, 1))
        tiles.append(jnp.where(jnp.asarray(dist >= 0), bias, NEG_BIG))
    return jnp.stack(tiles, axis=1)


def _sattn_kernel(pt_ref, w_ref, kn_ref, vn_ref, bl_ref, bn_ref, c_ref, lq_ref, sw_ref, *rest,
                  n_group, n_pages, n_tok, lam_init):
    k_refs = rest[:n_group]
    v_refs = rest[n_group:2 * n_group]
    o_ref = rest[2 * n_group]
    s_sc, acc_sc = rest[2 * n_group + 1:]
    phase = pl.program_id(1)
    step = pl.program_id(2)
    n_steps = pl.num_programs(2)
    past = n_pages * PAGE_SIZE
    half = DA_HEADS * n_tok

    @pl.when(phase == 0)
    def _():
        w = w_ref[...]

        @pl.when(step == 0)
        def _():
            kn = kn_ref[...]
            kn = jnp.concatenate([kn, jnp.zeros_like(kn)], axis=0).astype(BF16)
            sn = jnp.dot(kn, w, preferred_element_type=F32)[:n_tok]
            s_sc[pl.ds(past, n_tok), :] = sn + bn_ref[...]

        for g in range(n_group):
            page = step * n_group + g
            s = jnp.dot(k_refs[g][...].astype(BF16), w, preferred_element_type=F32)
            bias = jnp.where(page == n_pages - 1, bl_ref[...], c_ref[...])
            s_sc[pl.ds(pl.multiple_of(page * PAGE_SIZE, PAGE_SIZE), PAGE_SIZE), :] = s + bias

    @pl.when(phase == 1)
    def _():
        @pl.when(step == 0)
        def _():
            chunk = 512
            n_chunks = past // chunk

            def max_body(i, m):
                blk = s_sc[pl.ds(pl.multiple_of(i * chunk, chunk), chunk), :]
                return jnp.maximum(m, jnp.max(blk, axis=0, keepdims=True))

            s_new = s_sc[pl.ds(past, n_tok), :]
            m = lax.fori_loop(0, n_chunks, max_body, jnp.max(s_new, axis=0, keepdims=True))

            def sum_body(i, l):
                sl = pl.ds(pl.multiple_of(i * chunk, chunk), chunk)
                p = jnp.exp(s_sc[sl, :] - m)
                s_sc[sl, :] = p
                return l + jnp.sum(p, axis=0, keepdims=True)

            p_new = jnp.exp(s_new - m)
            l = lax.fori_loop(0, n_chunks, sum_body, jnp.sum(p_new, axis=0, keepdims=True))
            inv_l = 1.0 / l
            lam = _lambda_scalar(lq_ref[...], lam_init)

            def comb(p):
                pn = p * inv_l
                return pn - lam * pltpu.roll(pn, half, 1)

            def comb_body(i, c):
                sl = pl.ds(pl.multiple_of(i * chunk, chunk), chunk)
                s_sc[sl, :] = comb(s_sc[sl, :])
                return c

            lax.fori_loop(0, n_chunks, comb_body, 0)
            a_new_t = comb(p_new).T.astype(BF16).astype(F32)
            vn = vn_ref[...].astype(BF16).astype(F32)
            acc0 = a_new_t[:, 0:1] * vn[0:1, :]
            for t in range(1, n_tok):
                acc0 = acc0 + a_new_t[:, t:t + 1] * vn[t:t + 1, :]
            acc_sc[...] = acc0

        tot = jnp.zeros(acc_sc.shape, F32)
        for g in range(n_group):
            page = step * n_group + g
            a = s_sc[pl.ds(pl.multiple_of(page * PAGE_SIZE, PAGE_SIZE), PAGE_SIZE), :]
            at = a.T.astype(BF16)
            tot = tot + jnp.dot(at, v_refs[g][...].astype(BF16), preferred_element_type=F32)
        acc_sc[...] += tot

        @pl.when(step == n_steps - 1)
        def _():
            sw = sw_ref[...]
            for h in range(DA_HEADS):
                o = acc_sc[h * n_tok:(h + 1) * n_tok, h * DA_V_DIM:(h + 1) * DA_V_DIM]
                o_ref[:, h * DA_V_DIM:(h + 1) * DA_V_DIM] = _sub_norm(o, sw, lam_init).astype(o_ref.dtype)


def _sample_attn(proj, cache_k, cache_v, page_table, rel_bias, lam_qk, subln_w, layer, lam_init,
                 dec_batch, n_tok, n_group):
    n_pages = page_table.shape[1]
    assert n_pages % n_group == 0 and n_tok == 8
    past = n_pages * PAGE_SIZE
    n_steps = n_pages // n_group
    half = DA_HEADS * n_tok
    width = DA_HEADS * DA_V_DIM

    q = proj[:, :ATT_W].reshape(dec_batch, n_tok, DA_HEADS, 2, DA_HEAD_DIM) * (DA_HEAD_DIM ** -0.5)
    w = jnp.einsum('bthjd,hH,jJ->bhjdJHt', q, jnp.eye(DA_HEADS, dtype=F32), jnp.eye(2, dtype=F32))
    w = w.reshape(dec_batch, width, 2 * half).astype(BF16)

    tok = np.arange(n_tok)

    def col_bias(dist):
        b = rel_bias[_t5_bucket_np(dist)]
        b = jnp.where(jnp.asarray(dist >= 0)[..., None], b, NEG_BIG)
        b = jnp.transpose(b, (0, 2, 1)).reshape(dist.shape[0], half)
        return jnp.concatenate([b, b], axis=1)

    assert np.all(_t5_bucket_np(np.arange(PAGE_SIZE + 1, past + n_tok + 1)) == N_BUCKETS - 1)
    bias_last = col_bias((past + tok)[None, :] - (past - PAGE_SIZE + np.arange(PAGE_SIZE))[:, None])
    bias_new = col_bias(tok[None, :] - tok[:, None])
    bias_far = col_bias(np.full((1, n_tok), PAGE_SIZE + 1))

    ck = cache_k.reshape(cache_k.shape[0], cache_k.shape[1], PAGE_SIZE, width)
    cv = cache_v.reshape(cache_v.shape[0], cache_v.shape[1], PAGE_SIZE, width)
    kcol, vcol = ATT_W // width, 2 * ATT_W // width

    def k_idx(g):
        def idx(b, ph, s, pt):
            s_eff = jnp.where(ph == 0, s, n_steps - 1)
            return (layer, pt[b, s_eff * n_group + g], 0, 0)
        return idx

    def v_idx(g):
        def idx(b, ph, s, pt):
            s_eff = jnp.where(ph == 0, 0, s)
            return (layer, pt[b, s_eff * n_group + g], 0, 0)
        return idx

    const2 = lambda b, ph, s, pt: (0, 0)
    grid_spec = pltpu.PrefetchScalarGridSpec(
        num_scalar_prefetch=1,
        grid=(dec_batch, 2, n_steps),
        in_specs=[
            pl.BlockSpec((None, width, 2 * half), lambda b, ph, s, pt: (b, 0, 0)),
            pl.BlockSpec((n_tok, width), lambda b, ph, s, pt: (b, kcol)),
            pl.BlockSpec((n_tok, width), lambda b, ph, s, pt: (b, vcol)),
            pl.BlockSpec((PAGE_SIZE, 2 * half), const2),
            pl.BlockSpec((n_tok, 2 * half), const2),
            pl.BlockSpec((1, 2 * half), const2),
            pl.BlockSpec((None, 4, DA_HEAD_DIM), lambda b, ph, s, pt: (layer, 0, 0)),
            pl.BlockSpec((None, 1, DA_V_DIM), lambda b, ph, s, pt: (layer, 0, 0)),
        ] + [pl.BlockSpec((None, None, PAGE_SIZE, width), k_idx(g)) for g in range(n_group)]
          + [pl.BlockSpec((None, None, PAGE_SIZE, width), v_idx(g)) for g in range(n_group)],
        out_specs=pl.BlockSpec((n_tok, width), lambda b, ph, s, pt: (b, 0)),
        scratch_shapes=[pltpu.VMEM((past + n_tok, 2 * half), F32),
                        pltpu.VMEM((2 * half, width), F32)],
    )
    return pl.pallas_call(
        functools.partial(_sattn_kernel, n_group=n_group, n_pages=n_pages, n_tok=n_tok,
                          lam_init=lam_init),
        grid_spec=grid_spec,
        out_shape=jax.ShapeDtypeStruct((dec_batch * n_tok, width), BF16),
        compiler_params=_cparams(("parallel", "arbitrary", "arbitrary")),
        name="sample_attn",
    )(page_table, w, proj, proj, bias_last, bias_new, bias_far, lam_qk, subln_w,
      *([ck] * n_group), *([cv] * n_group))


def _mlstm_kernel(q_ref, k_ref, v_ref, o_gate_ref, g_ref, nw_ref, c0_ref, n0_ref, m0_ref,
                  h_ref, c_ref, n_ref, m_ref, *, rows, chunk):
    c_idx = pl.program_id(1)

    @pl.when(c_idx == 0)
    def _():
        c_ref[...] = c0_ref[...]
        n_ref[...] = n0_ref[...]
        m_ref[...] = m0_ref[...]

    row = lax.broadcasted_iota(jnp.int32, (chunk, chunk), 0)
    col = lax.broadcasted_iota(jnp.int32, (chunk, chunk), 1)
    tril = row >= col
    eye = row == col

    def padded(x):
        if rows == chunk:
            return x
        return jnp.concatenate([x, jnp.zeros((chunk - rows, x.shape[1]), x.dtype)], axis=0)

    for h in range(ML_HEADS):
        sl = slice(h * ML_DQK, (h + 1) * ML_DQK)
        q = padded(q_ref[:, sl])
        k = padded(k_ref[:, sl]) * (ML_DQK ** -0.5)
        v = padded(v_ref[:, sl])
        ig = g_ref[h:h + 1, :]
        gf = g_ref[ML_HEADS + h:ML_HEADS + h + 1, :]
        lf = jnp.minimum(gf, 0.0) - jnp.log1p(jnp.exp(-jnp.abs(gf)))
        c_state = c_ref[h]
        n_state = n_ref[h]
        m_state = m_ref[:, h:h + 1]

        b_col = jnp.sum(jnp.where(tril, lf, 0.0), axis=1, keepdims=True)
        b_row = jnp.sum(jnp.where(eye, b_col, 0.0), axis=0, keepdims=True)
        logw = jnp.where(tril, b_col - b_row + ig, NEG_BIG)
        m_inter = b_col + m_state
        m_t = jnp.maximum(jnp.max(logw, axis=1, keepdims=True), m_inter)
        inter = jnp.exp(m_inter - m_t)
        qb = q.astype(BF16)
        kb = k.astype(BF16)
        s = lax.dot_general(qb, kb, (((1,), (1,)), ((), ())), preferred_element_type=F32)
        s = s * jnp.exp(logw - m_t)
        num = jnp.dot(s.astype(BF16), v.astype(BF16), preferred_element_type=F32)
        num = num + inter * lax.dot_general(qb, c_state.astype(BF16), (((1,), (1,)), ((), ())),
                                            preferred_element_type=F32)
        den = jnp.sum(s, axis=1, keepdims=True) + inter * jnp.sum(q * n_state, axis=1, keepdims=True)
        hh = num / jnp.maximum(jnp.abs(den), jnp.exp(-m_t))

        b_last = jnp.sum(lf, axis=1, keepdims=True)
        logw_end = b_last - b_row + ig
        m_new = jnp.maximum(b_last + m_state, jnp.max(logw_end, axis=1, keepdims=True))
        w_end = jnp.exp(logw_end - m_new)
        decay = jnp.exp(b_last + m_state - m_new)
        w_col = jnp.sum(jnp.where(eye, w_end, 0.0), axis=1, keepdims=True)
        vw = (v * w_col).astype(BF16)
        c_ref[h] = decay * c_state + lax.dot_general(vw, kb, (((0,), (0,)), ((), ())),
                                                     preferred_element_type=F32)
        n_ref[h] = decay * n_state + jnp.sum(k * w_col, axis=0, keepdims=True)
        m_ref[:, h:h + 1] = m_new

        hh = hh[:rows]
        mc = hh - jnp.mean(hh, axis=1, keepdims=True)
        y = mc * lax.rsqrt(jnp.mean(mc * mc, axis=1, keepdims=True) + LN_EPS) * nw_ref[:, sl]
        h_ref[:, sl] = (jax.nn.sigmoid(o_gate_ref[:, sl]) * y).astype(h_ref.dtype)


def _mlstm(proj, gates_t, norm_w, c0, n0, m0, layer, batch, seq, rows_per_block):
    nc = seq // rows_per_block
    chunk = gates_t.shape[2] // nc
    cq, ck, cv, co = (3 * ATT_W // ML_W, 3 * ATT_W // ML_W + 1, 3 * ATT_W // ML_W + 2,
                      3 * ATT_W // ML_W + 3)
    rows = lambda col: pl.BlockSpec((rows_per_block, ML_W), lambda b, c: (b * nc + c, col))
    state_c = pl.BlockSpec((None, ML_HEADS, ML_DV, ML_DQK), lambda b, c: (b, 0, 0, 0))
    state_n = pl.BlockSpec((None, ML_HEADS, 1, ML_DQK), lambda b, c: (b, 0, 0, 0))
    state_m = pl.BlockSpec((None, 1, ML_HEADS), lambda b, c: (b, 0, 0))
    return pl.pallas_call(
        functools.partial(_mlstm_kernel, rows=rows_per_block, chunk=chunk),
        grid=(batch, nc),
        in_specs=[rows(cq), rows(ck), rows(cv), rows(co),
                  pl.BlockSpec((None, 2 * ML_HEADS, chunk), lambda b, c: (b, 0, c)),
                  pl.BlockSpec((None, 1, ML_W), lambda b, c: (layer, 0, 0)),
                  state_c, state_n, state_m],
        out_specs=[pl.BlockSpec((rows_per_block, ML_W), lambda b, c: (b * nc + c, 0)),
                   state_c, state_n, state_m],
        out_shape=[jax.ShapeDtypeStruct((batch * seq, ML_W), BF16),
                   jax.ShapeDtypeStruct((batch, ML_HEADS, ML_DV, ML_DQK), F32),
                   jax.ShapeDtypeStruct((batch, ML_HEADS, 1, ML_DQK), F32),
                   jax.ShapeDtypeStruct((batch, 1, ML_HEADS), F32)],
        compiler_params=_cparams(("parallel", "arbitrary")),
        name="mlstm",
    )(proj, proj, proj, proj, gates_t, norm_w, c0, n0, m0)


def _merge_kernel(ao_ref, mh_ref, wa_ref, wm_ref, ga_ref, gm_ref, ba_ref, bm_ref, o_ref):
    ta = jnp.dot(ao_ref[...], wa_ref[...], preferred_element_type=F32)
    tmm = jnp.dot(mh_ref[...], wm_ref[...], preferred_element_type=F32)
    ga = jax.nn.sigmoid(ga_ref[...] + ba_ref[...])
    gm = jax.nn.sigmoid(gm_ref[...] + bm_ref[...])
    o_ref[...] = (ga * ta + gm * tmm).astype(o_ref.dtype)


def _merge(ao, mh, proj, w_ba, w_bm, b_gate, layer, tm, tn):
    m = ao.shape[0]
    nd = D_MODEL // tn
    ga0 = COL_GATE // tn
    return pl.pallas_call(
        _merge_kernel,
        grid=(m // tm, nd),
        in_specs=[
            pl.BlockSpec((tm, ATT_W), lambda i, j: (i, 0)),
            pl.BlockSpec((tm, ML_W), lambda i, j: (i, 0)),
            pl.BlockSpec((None, ATT_W, tn), lambda i, j: (layer, 0, j)),
            pl.BlockSpec((None, ML_W, tn), lambda i, j: (layer, 0, j)),
            pl.BlockSpec((tm, tn), lambda i, j: (i, ga0 + j)),
            pl.BlockSpec((tm, tn), lambda i, j: (i, ga0 + nd + j)),
            pl.BlockSpec((None, 1, tn), lambda i, j: (layer, 0, j)),
            pl.BlockSpec((None, 1, tn), lambda i, j: (layer, 0, nd + j)),
        ],
        out_specs=pl.BlockSpec((tm, tn), lambda i, j: (i, j)),
        out_shape=jax.ShapeDtypeStruct((m, D_MODEL), BF16),
        compiler_params=_cparams(("parallel", "parallel")),
        name="merge",
    )(ao, mh, w_ba, w_bm, proj, proj, b_gate, b_gate)


def _out_kernel(x_ref, mg_ref, w_ref, g_ref, b_ref, o_ref, ob_ref):
    y = DN_ALPHA * x_ref[...] + jnp.dot(mg_ref[...], w_ref[...], preferred_element_type=F32)
    o = _layer_norm_rows(y, g_ref[...], b_ref[...])
    o_ref[...] = o
    ob_ref[...] = o.astype(BF16)


def _out_proj_ln(x, merged, w_out, ln_g, ln_b, layer, tm):
    m = x.shape[0]
    return pl.pallas_call(
        _out_kernel,
        grid=(m // tm,),
        in_specs=[
            pl.BlockSpec((tm, D_MODEL), lambda i: (i, 0)),
            pl.BlockSpec((tm, D_MODEL), lambda i: (i, 0)),
            pl.BlockSpec((None, D_MODEL, D_MODEL), lambda i: (layer, 0, 0)),
            pl.BlockSpec((None, None, 1, D_MODEL), lambda i: (layer, 1, 0, 0)),
            pl.BlockSpec((None, None, 1, D_MODEL), lambda i: (layer, 1, 0, 0)),
        ],
        out_specs=[pl.BlockSpec((tm, D_MODEL), lambda i: (i, 0)),
                   pl.BlockSpec((tm, D_MODEL), lambda i: (i, 0))],
        out_shape=[jax.ShapeDtypeStruct((m, D_MODEL), F32),
                   jax.ShapeDtypeStruct((m, D_MODEL), BF16)],
        compiler_params=_cparams(("parallel",)),
        name="out_proj_ln",
    )(x, merged, w_out, ln_g, ln_b)


def _gates_transposed(proj, b_if, batch, seq, pad_to):
    gif = proj[:, COL_IF:COL_IF + N_IF] + b_if
    gt = jnp.transpose(gif.reshape(batch, seq, N_IF), (0, 2, 1))
    if pad_to > seq:
        pad = jnp.concatenate([jnp.full((batch, ML_HEADS, pad_to - seq), NEG_BIG, F32),
                               jnp.full((batch, ML_HEADS, pad_to - seq), 1e4, F32)], axis=1)
        gt = jnp.concatenate([gt, pad], axis=2)
    return gt


def kernel(x_prompt, x_sample, cache_k, cache_v, page_table, state_C, state_n, state_m, rel_bias,
           w_in, b_gate, b_if, lam_qk, subln_w, mlstm_norm_w, w_branch_attn, w_branch_mlstm,
           w_out, ffn_up, ffn_down, ln_g, ln_b):
    batch, seq, _ = x_prompt.shape
    dec_batch, dec_seq, _ = x_sample.shape
    mp, ms = batch * seq, dec_batch * dec_seq
    tm_p, tm_s = 512, ms
    tq = 512
    prompt_chunk = 256

    w_cat = jnp.concatenate(
        [w_in[:, :, :N_MAIN], w_in[:, :, N_MAIN + N_IF:], w_in[:, :, N_MAIN:N_MAIN + N_IF],
         jnp.zeros((DEPTH, D_MODEL, N_IF_PAD - N_IF), w_in.dtype)], axis=2).astype(BF16)
    up_b = ffn_up.astype(BF16)
    down_b = ffn_down.astype(BF16)
    wba_b = w_branch_attn.astype(BF16)
    wbm_b = w_branch_mlstm.astype(BF16)
    wout_b = w_out.astype(BF16)
    ln_g = ln_g.reshape(DEPTH, 3, 1, D_MODEL)
    ln_b = ln_b.reshape(DEPTH, 3, 1, D_MODEL)
    b_gate3 = b_gate.reshape(DEPTH, 1, N_GATE)
    subln3 = subln_w.reshape(DEPTH, 1, DA_V_DIM)
    normw3 = mlstm_norm_w.reshape(DEPTH, 1, ML_W)
    bias_tiles = _prompt_bias_tiles(rel_bias, tq)

    zero_c = jnp.zeros((batch, ML_HEADS, ML_DV, ML_DQK), F32)
    zero_n = jnp.zeros((batch, ML_HEADS, 1, ML_DQK), F32)
    zero_m = jnp.zeros((batch, 1, ML_HEADS), F32)

    xp = x_prompt.reshape(mp, D_MODEL)
    xs = x_sample.reshape(ms, D_MODEL)
    xpb = xp.astype(BF16)
    xsb = xs.astype(BF16)

    outs = {name: [] for name in ('kp', 'vp', 'ks', 'vs', 'cp', 'np', 'mp', 'cs', 'ns', 'ms')}
    for l in range(DEPTH):
        lam_init = 0.8 - 0.6 * math.exp(-0.3 * l)

        def rowwise_pre(x, xb, tm):
            x1, x1b = _ffn_ln(x, xb, up_b, down_b, ln_g, ln_b, l, 0, 0, tm, 512)
            proj = _in_proj(x1b, w_cat, l, tm, 1280)
            return x1, proj

        def rowwise_post(x1, proj, ao, mh, tm):
            merged = _merge(ao, mh, proj, wba_b, wbm_b, b_gate3, l, tm, 512)
            x2, x2b = _out_proj_ln(x1, merged, wout_b, ln_g, ln_b, l, min(tm, 256))
            return _ffn_ln(x2, x2b, up_b, down_b, ln_g, ln_b, l, 1, 2, tm, 512)

        x1, proj = rowwise_pre(xp, xpb, tm_p)
        ao = _prompt_attn(proj, bias_tiles, lam_qk, subln3, l, lam_init, batch, seq, tq)
        gt = _gates_transposed(proj, b_if[l], batch, seq, seq)
        mh, c_new, n_new, m_new = _mlstm(proj, gt, normw3, zero_c, zero_n, zero_m, l, batch, seq,
                                         prompt_chunk)
        xp, xpb = rowwise_post(x1, proj, ao, mh, tm_p)
        outs['kp'].append(proj[:, ATT_W:2 * ATT_W].reshape(batch, seq, DA_HEADS, DA_V_DIM))
        outs['vp'].append(proj[:, 2 * ATT_W:3 * ATT_W].reshape(batch, seq, DA_HEADS, DA_V_DIM))
        outs['cp'].append(c_new)
        outs['np'].append(n_new.reshape(batch, ML_HEADS, ML_DQK))
        outs['mp'].append(m_new.reshape(batch, ML_HEADS))

        x1, proj = rowwise_pre(xs, xsb, tm_s)
        ao = _sample_attn(proj, cache_k, cache_v, page_table, rel_bias, lam_qk, subln3, l, lam_init,
                          dec_batch, dec_seq, 8)
        gt = _gates_transposed(proj, b_if[l], dec_batch, dec_seq, 128)
        mh, c_new, n_new, m_new = _mlstm(proj, gt, normw3, state_C[l],
                                         state_n[l].reshape(dec_batch, ML_HEADS, 1, ML_DQK),
                                         state_m[l].reshape(dec_batch, 1, ML_HEADS), l, dec_batch,
                                         dec_seq, dec_seq)
        xs, xsb = rowwise_post(x1, proj, ao, mh, tm_s)
        outs['ks'].append(proj[:, ATT_W:2 * ATT_W].reshape(dec_batch, dec_seq, DA_HEADS, DA_V_DIM))
        outs['vs'].append(proj[:, 2 * ATT_W:3 * ATT_W].reshape(dec_batch, dec_seq, DA_HEADS, DA_V_DIM))
        outs['cs'].append(c_new)
        outs['ns'].append(n_new.reshape(dec_batch, ML_HEADS, ML_DQK))
        outs['ms'].append(m_new.reshape(dec_batch, ML_HEADS))

    st = lambda name: jnp.stack(outs[name])
    return (xp.reshape(batch, seq, D_MODEL), xs.reshape(dec_batch, dec_seq, D_MODEL),
            st('kp'), st('vp'), st('ks'), st('vs'), st('cp'), st('np'), st('mp'),
            st('cs'), st('ns'), st('ms'))
```

```python
import functools
import math

import numpy as np
import jax
import jax.numpy as jnp
from jax import lax
from jax.experimental import pallas as pl
from jax.experimental.pallas import tpu as pltpu

F32 = jnp.float32
BF16 = jnp.bfloat16

D_MODEL = 2048
DEPTH = 4
PAGE_SIZE = 128
DA_HEADS = 8
DA_HEAD_DIM = 64
DA_V_DIM = 2 * DA_HEAD_DIM
ML_HEADS = 4
ML_DQK = 256
ML_DV = 256
D_FF = 5632
N_BUCKETS = 32
MAX_DISTANCE = 128
LN_EPS = 1e-5
DN_ALPHA = (2.0 * DEPTH) ** 0.25

ATT_W = DA_HEADS * DA_V_DIM
ML_W = ML_HEADS * ML_DV
N_MAIN = 3 * ATT_W + 4 * ML_W
N_GATE = 2 * D_MODEL
N_IF = 2 * ML_HEADS
N_IF_PAD = 256
N_PROJ = N_MAIN + N_GATE + N_IF_PAD
COL_GATE = N_MAIN
COL_IF = N_MAIN + N_GATE

NEG_BIG = -1e30
LOG2E = math.log2(math.e)
QCOLS = 256
ACC_PAD = 16
VMEM_LIMIT = 48 * 1024 * 1024


def _cparams(sem):
    return pltpu.CompilerParams(dimension_semantics=sem, vmem_limit_bytes=VMEM_LIMIT)


def _layer_norm_rows(y, g, b):
    mu = jnp.mean(y, axis=-1, keepdims=True)
    yc = y - mu
    var = jnp.mean(yc * yc, axis=-1, keepdims=True)
    return yc * lax.rsqrt(var + LN_EPS) * g + b


def _ffn_kernel(x_ref, xb_ref, wa_ref, wb_ref, wd_ref, g_ref, b_ref, o_ref, ob_ref, acc_ref):
    j = pl.program_id(1)

    @pl.when(j == 0)
    def _():
        acc_ref[...] = jnp.zeros_like(acc_ref)

    xb = xb_ref[...]
    a = jnp.dot(xb, wa_ref[...], preferred_element_type=F32)
    b = jnp.dot(xb, wb_ref[...], preferred_element_type=F32)
    h = (a * jax.nn.sigmoid(a) * b).astype(BF16)
    acc_ref[...] += jnp.dot(h, wd_ref[...], preferred_element_type=F32)

    @pl.when(j == pl.num_programs(1) - 1)
    def _():
        y = DN_ALPHA * x_ref[...] + 0.5 * acc_ref[...]
        o = _layer_norm_rows(y, g_ref[...], b_ref[...])
        o_ref[...] = o
        ob_ref[...] = o.astype(BF16)


def _ffn_ln(x, xb, w_up, w_down, ln_g, ln_b, layer, which, ln_idx, tm, tf):
    m = x.shape[0]
    nf = D_FF // tf
    return pl.pallas_call(
        _ffn_kernel,
        grid=(m // tm, nf),
        in_specs=[
            pl.BlockSpec((tm, D_MODEL), lambda i, j: (i, 0)),
            pl.BlockSpec((tm, D_MODEL), lambda i, j: (i, 0)),
            pl.BlockSpec((None, None, D_MODEL, tf), lambda i, j: (layer, which, 0, j)),
            pl.BlockSpec((None, None, D_MODEL, tf), lambda i, j: (layer, which, 0, j + nf)),
            pl.BlockSpec((None, None, tf, D_MODEL), lambda i, j: (layer, which, j, 0)),
            pl.BlockSpec((None, None, 1, D_MODEL), lambda i, j: (layer, ln_idx, 0, 0)),
            pl.BlockSpec((None, None, 1, D_MODEL), lambda i, j: (layer, ln_idx, 0, 0)),
        ],
        out_specs=[
            pl.BlockSpec((tm, D_MODEL), lambda i, j: (i, 0)),
            pl.BlockSpec((tm, D_MODEL), lambda i, j: (i, 0)),
        ],
        out_shape=[jax.ShapeDtypeStruct((m, D_MODEL), F32),
                   jax.ShapeDtypeStruct((m, D_MODEL), BF16)],
        scratch_shapes=[pltpu.VMEM((tm, D_MODEL), F32)],
        compiler_params=_cparams(("parallel", "arbitrary")),
        name="ffn_ln",
    )(x, xb, w_up, w_up, w_down, ln_g, ln_b)


def _proj_kernel(xb_ref, w_ref, o_ref):
    o_ref[...] = jnp.dot(xb_ref[...], w_ref[...], preferred_element_type=F32)


def _in_proj(xb, w_cat, layer, tm, tn):
    m = xb.shape[0]
    return pl.pallas_call(
        _proj_kernel,
        grid=(m // tm, N_PROJ // tn),
        in_specs=[
            pl.BlockSpec((tm, D_MODEL), lambda i, j: (i, 0)),
            pl.BlockSpec((None, D_MODEL, tn), lambda i, j: (layer, 0, j)),
        ],
        out_specs=pl.BlockSpec((tm, tn), lambda i, j: (i, j)),
        out_shape=jax.ShapeDtypeStruct((m, N_PROJ), F32),
        compiler_params=_cparams(("parallel", "parallel")),
        name="in_proj",
    )(xb, w_cat)


def _t5_bucket_np(dist):
    n = np.maximum(dist, 0)
    max_exact = N_BUCKETS // 2
    nf = np.maximum(n, 1).astype(np.float32)
    large = max_exact + (np.log(nf / np.float32(max_exact)) / np.float32(math.log(MAX_DISTANCE / max_exact))
                         * np.float32(N_BUCKETS - max_exact)).astype(np.int32)
    large = np.minimum(large, N_BUCKETS - 1)
    return np.where(n < max_exact, n, large).astype(np.int32)


def _lambda_scalar(lq, lam_init):
    s01 = jnp.sum(lq[0:1, :] * lq[1:2, :], axis=1, keepdims=True)
    s23 = jnp.sum(lq[2:3, :] * lq[3:4, :], axis=1, keepdims=True)
    return jnp.exp(s01) - jnp.exp(s23) + lam_init


def _sub_norm(o, w_row, lam_init):
    o = o * lax.rsqrt(jnp.mean(o * o, axis=-1, keepdims=True) + LN_EPS)
    return o * w_row * (1.0 - lam_init)


def _pattn_kernel(qi_tab, ki_tab, q_ref, k_ref, v_ref, bias_ref, lq_ref, sw_ref, o_ref,
                  qt_sc, m_sc, acc_sc, *, tq, lam_init):
    t = pl.program_id(2)
    qi = qi_tab[t]
    ki = ki_tab[t]
    tk = k_ref.shape[0]

    @pl.when(ki == 0)
    def _():
        q = q_ref[...] * (DA_HEAD_DIM ** -0.5 * LOG2E)
        lane = lax.broadcasted_iota(jnp.int32, q.shape, 1)
        qt_sc[:, :tq] = jnp.where(lane < DA_HEAD_DIM, q, 0.0).T.astype(BF16)
        qt_sc[:, tq:] = jnp.where(lane >= DA_HEAD_DIM, q, 0.0).T.astype(BF16)
        m_sc[...] = jnp.full_like(m_sc, -jnp.inf)
        acc_sc[...] = jnp.zeros_like(acc_sc)

    def accumulate(with_bias):
        kb = k_ref[...].astype(BF16)
        vt = jnp.concatenate([v_ref[...].T, jnp.ones((ACC_PAD, tk), F32)], axis=0).astype(BF16)
        n_chunks = 2 * tq // QCOLS
        cols = [slice(c * QCOLS, (c + 1) * QCOLS) for c in range(n_chunks)]

        def scores(c):
            s = jnp.dot(kb, qt_sc[:, cols[c]], preferred_element_type=F32)
            if with_bias:
                b0 = (c * QCOLS) % tq
                s = s + bias_ref[:, b0:b0 + QCOLS]
            return s

        def softmax(c, s):
            m_old = m_sc[:, cols[c]]
            m_new = jnp.maximum(m_old, jnp.max(s, axis=0, keepdims=True))
            m_sc[:, cols[c]] = m_new
            return jnp.exp2(s - m_new).astype(BF16), jnp.exp2(m_old - m_new)

        def update(c, p, alpha):
            acc_sc[:, cols[c]] = alpha * acc_sc[:, cols[c]] + jnp.dot(vt, p, preferred_element_type=F32)

        s_q = {0: scores(0), 1: scores(1)}
        for c in range(n_chunks):
            p, alpha = softmax(c, s_q.pop(c))
            if c + 2 < n_chunks:
                s_q[c + 2] = scores(c + 2)
            update(c, p, alpha)

    @pl.when(qi - ki <= 1)
    def _():
        accumulate(True)

    @pl.when(qi - ki > 1)
    def _():
        accumulate(False)

    @pl.when(ki == qi)
    def _():
        acc = acc_sc[...]
        o_all = acc[:DA_V_DIM] * (1.0 / acc[DA_V_DIM:DA_V_DIM + 1])
        lam = _lambda_scalar(lq_ref[...], lam_init)
        o = o_all[:, :tq] - lam * o_all[:, tq:]
        o = o * lax.rsqrt(jnp.mean(o * o, axis=0, keepdims=True) + LN_EPS)
        o = o * (sw_ref[...] * (1.0 - lam_init))
        o_ref[...] = o.T.astype(o_ref.dtype)


def _prompt_attn(proj, bias_tiles, lam_qk, subln_col, layer, lam_init, batch, seq, tq):
    nq = seq // tq
    tri = [(qi, ki) for qi in range(nq) for ki in range(qi + 1)]
    qi_tab = jnp.asarray([a for a, _ in tri], jnp.int32)
    ki_tab = jnp.asarray([b for _, b in tri], jnp.int32)
    hq, hk, hv = 0, ATT_W // DA_V_DIM, 2 * ATT_W // DA_V_DIM

    def bias_idx(b, h, t, qt, kt):
        return (h, jnp.minimum(qt[t] - kt[t], 1), 0, 0)

    grid_spec = pltpu.PrefetchScalarGridSpec(
        num_scalar_prefetch=2,
        grid=(batch, DA_HEADS, len(tri)),
        in_specs=[
            pl.BlockSpec((tq, DA_V_DIM), lambda b, h, t, qt, kt: (b * nq + qt[t], hq + h)),
            pl.BlockSpec((tq, DA_V_DIM), lambda b, h, t, qt, kt: (b * nq + kt[t], hk + h)),
            pl.BlockSpec((tq, DA_V_DIM), lambda b, h, t, qt, kt: (b * nq + kt[t], hv + h)),
            pl.BlockSpec((None, None, tq, tq), bias_idx),
            pl.BlockSpec((None, 4, DA_HEAD_DIM), lambda b, h, t, qt, kt: (layer, 0, 0)),
            pl.BlockSpec((None, DA_V_DIM, 1), lambda b, h, t, qt, kt: (layer, 0, 0)),
        ],
        out_specs=pl.BlockSpec((tq, DA_V_DIM), lambda b, h, t, qt, kt: (b * nq + qt[t], h)),
        scratch_shapes=[pltpu.VMEM((DA_V_DIM, 2 * tq), BF16), pltpu.VMEM((1, 2 * tq), F32),
                        pltpu.VMEM((DA_V_DIM + ACC_PAD, 2 * tq), F32)],
    )
    return pl.pallas_call(
        functools.partial(_pattn_kernel, tq=tq, lam_init=lam_init),
        grid_spec=grid_spec,
        out_shape=jax.ShapeDtypeStruct((batch * seq, ATT_W), BF16),
        compiler_params=_cparams(("parallel", "parallel", "arbitrary")),
        name="prompt_attn",
    )(qi_tab, ki_tab, proj, proj, proj, bias_tiles, lam_qk, subln_col)


def _prompt_bias_tiles(rel_bias, tq):
    assert tq >= MAX_DISTANCE and np.all(_t5_bucket_np(np.arange(MAX_DISTANCE, 1 << 16)) == N_BUCKETS - 1)
    span = 2 * tq
    d_row = np.zeros(span + 1, np.int64)
    d_row[:tq] = np.arange(tq)
    d_row[span + 1 - np.arange(1, tq)] = -np.arange(1, tq)
    far = rel_bias[N_BUCKETS - 1]
    tiles = []
    for delta in range(2):
        dist = d_row + delta * tq
        y = (rel_bias[_t5_bucket_np(dist)] - far) * LOG2E
        y = jnp.where(jnp.asarray(dist >= 0)[:, None], y, NEG_BIG).T
        skew = jnp.tile(y, (1, tq))[:, :tq * span].reshape(DA_HEADS, tq, span)
        tiles.append(skew[:, :, :tq])
    return jnp.stack(tiles, axis=1)


def _page_rows(page_ref):
    return jnp.concatenate([page_ref[pl.ds(h, PAGE_SIZE, stride=DA_HEADS), :] for h in range(DA_HEADS)],
                           axis=1).astype(BF16)


def _sattn_kernel(pt_ref,w_ref, kn_ref, vn_ref, bl_ref, bn_ref, c_ref, lq_ref, sw_ref, *rest,
                  n_group, n_pages, n_tok, lam_init):
    k_refs = rest[:n_group]
    v_refs = rest[n_group:2 * n_group]
    o_ref = rest[2 * n_group]
    s_sc, acc_sc = rest[2 * n_group + 1:]
    phase = pl.program_id(1)
    step = pl.program_id(2)
    n_steps = pl.num_programs(2)
    past = n_pages * PAGE_SIZE
    half = DA_HEADS * n_tok

    @pl.when(phase == 0)
    def _():
        w = w_ref[...]

        @pl.when(step == 0)
        def _():
            kn = kn_ref[...]
            kn = jnp.concatenate([kn, jnp.zeros_like(kn)], axis=0).astype(BF16)
            sn = jnp.dot(kn, w, preferred_element_type=F32)[:n_tok]
            s_sc[pl.ds(past, n_tok), :] = sn + bn_ref[...]

        for g in range(n_group):
            page = step * n_group + g
            s = jnp.dot(_page_rows(k_refs[g]), w, preferred_element_type=F32)
            bias = jnp.where(page == n_pages - 1, bl_ref[...], c_ref[...])
            s_sc[pl.ds(pl.multiple_of(page * PAGE_SIZE, PAGE_SIZE), PAGE_SIZE), :] = s + bias

    @pl.when(phase == 1)
    def _():
        @pl.when(step == 0)
        def _():
            chunk = 512
            n_chunks = past // chunk

            def max_body(i, m):
                blk = s_sc[pl.ds(pl.multiple_of(i * chunk, chunk), chunk), :]
                return jnp.maximum(m, jnp.max(blk, axis=0, keepdims=True))

            s_new = s_sc[pl.ds(past, n_tok), :]
            m = lax.fori_loop(0, n_chunks, max_body, jnp.max(s_new, axis=0, keepdims=True))

            def sum_body(i, l):
                sl = pl.ds(pl.multiple_of(i * chunk, chunk), chunk)
                p = jnp.exp(s_sc[sl, :] - m)
                s_sc[sl, :] = p
                return l + jnp.sum(p, axis=0, keepdims=True)

            p_new = jnp.exp(s_new - m)
            l = lax.fori_loop(0, n_chunks, sum_body, jnp.sum(p_new, axis=0, keepdims=True))
            inv_l = 1.0 / l
            lam = _lambda_scalar(lq_ref[...], lam_init)

            def comb(p):
                pn = p * inv_l
                return pn - lam * pltpu.roll(pn, half, 1)

            def comb_body(i, c):
                sl = pl.ds(pl.multiple_of(i * chunk, chunk), chunk)
                s_sc[sl, :] = comb(s_sc[sl, :])
                return c

            lax.fori_loop(0, n_chunks, comb_body, 0)
            a_new_t = comb(p_new).T.astype(BF16).astype(F32)
            vn = vn_ref[...].astype(BF16).astype(F32)
            acc0 = a_new_t[:, 0:1] * vn[0:1, :]
            for t in range(1, n_tok):
                acc0 = acc0 + a_new_t[:, t:t + 1] * vn[t:t + 1, :]
            acc_sc[...] = acc0

        tot = jnp.zeros(acc_sc.shape, F32)
        for g in range(n_group):
            page = step * n_group + g
            a = s_sc[pl.ds(pl.multiple_of(page * PAGE_SIZE, PAGE_SIZE), PAGE_SIZE), :]
            at = a.T.astype(BF16)
            tot = tot + jnp.dot(at, _page_rows(v_refs[g]), preferred_element_type=F32)
        acc_sc[...] += tot

        @pl.when(step == n_steps - 1)
        def _():
            sw = sw_ref[...]
            for h in range(DA_HEADS):
                o = acc_sc[h * n_tok:(h + 1) * n_tok, h * DA_V_DIM:(h + 1) * DA_V_DIM]
                o_ref[:, h * DA_V_DIM:(h + 1) * DA_V_DIM] = _sub_norm(o, sw, lam_init).astype(o_ref.dtype)


def _sample_attn(proj, cache_k, cache_v, page_table, rel_bias, lam_qk, subln_w, layer, lam_init,
                 dec_batch, n_tok, n_group):
    n_pages = page_table.shape[1]
    assert n_pages % n_group == 0 and n_tok == 8
    past = n_pages * PAGE_SIZE
    n_steps = n_pages // n_group
    half = DA_HEADS * n_tok
    width = DA_HEADS * DA_V_DIM

    q = proj[:, :ATT_W].reshape(dec_batch, n_tok, DA_HEADS, 2, DA_HEAD_DIM) * (DA_HEAD_DIM ** -0.5)
    w = jnp.einsum('bthjd,hH,jJ->bhjdJHt', q, jnp.eye(DA_HEADS, dtype=F32), jnp.eye(2, dtype=F32))
    w = w.reshape(dec_batch, width, 2 * half).astype(BF16)

    tok = np.arange(n_tok)

    def col_bias(dist):
        b = rel_bias[_t5_bucket_np(dist)]
        b = jnp.where(jnp.asarray(dist >= 0)[..., None], b, NEG_BIG)
        b = jnp.transpose(b, (0, 2, 1)).reshape(dist.shape[0], half)
        return jnp.concatenate([b, b], axis=1)

    assert np.all(_t5_bucket_np(np.arange(PAGE_SIZE + 1, past + n_tok + 1)) == N_BUCKETS - 1)
    bias_last = col_bias((past + tok)[None, :] - (past - PAGE_SIZE + np.arange(PAGE_SIZE))[:, None])
    bias_new = col_bias(tok[None, :] - tok[:, None])
    bias_far = col_bias(np.full((1, n_tok), PAGE_SIZE + 1))

    page_block = (None, None, PAGE_SIZE * DA_HEADS, DA_V_DIM)
    cache_k = cache_k.reshape(cache_k.shape[0], cache_k.shape[1], PAGE_SIZE * DA_HEADS, DA_V_DIM)
    cache_v = cache_v.reshape(cache_v.shape[0], cache_v.shape[1], PAGE_SIZE * DA_HEADS, DA_V_DIM)
    kcol, vcol = ATT_W // width, 2 * ATT_W // width

    def k_idx(g):
        def idx(b, ph, s, pt):
            s_eff = jnp.where(ph == 0, s, n_steps - 1)
            return (layer, pt[b, s_eff * n_group + g], 0, 0)
        return idx

    def v_idx(g):
        def idx(b, ph, s, pt):
            s_eff = jnp.where(ph == 0, 0, s)
            return (layer, pt[b, s_eff * n_group + g], 0, 0)
        return idx

    const2 = lambda b, ph, s, pt: (0, 0)
    grid_spec = pltpu.PrefetchScalarGridSpec(
        num_scalar_prefetch=1,
        grid=(dec_batch, 2, n_steps),
        in_specs=[
            pl.BlockSpec((None, width, 2 * half), lambda b, ph, s, pt: (b, 0, 0)),
            pl.BlockSpec((n_tok, width), lambda b, ph, s, pt: (b, kcol)),
            pl.BlockSpec((n_tok, width), lambda b, ph, s, pt: (b, vcol)),
            pl.BlockSpec((PAGE_SIZE, 2 * half), const2),
            pl.BlockSpec((n_tok, 2 * half), const2),
            pl.BlockSpec((1, 2 * half), const2),
            pl.BlockSpec((None, 4, DA_HEAD_DIM), lambda b, ph, s, pt: (layer, 0, 0)),
            pl.BlockSpec((None, 1, DA_V_DIM), lambda b, ph, s, pt: (layer, 0, 0)),
        ] + [pl.BlockSpec(page_block,k_idx(g)) for g in range(n_group)]
          + [pl.BlockSpec(page_block,v_idx(g)) for g in range(n_group)],
        out_specs=pl.BlockSpec((n_tok, width), lambda b, ph, s, pt: (b, 0)),
        scratch_shapes=[pltpu.VMEM((past + n_tok, 2 * half), F32),
                        pltpu.VMEM((2 * half, width), F32)],
    )
    return pl.pallas_call(
        functools.partial(_sattn_kernel, n_group=n_group, n_pages=n_pages, n_tok=n_tok,
                          lam_init=lam_init),
        grid_spec=grid_spec,
        out_shape=jax.ShapeDtypeStruct((dec_batch * n_tok, width), BF16),
        compiler_params=_cparams(("parallel", "arbitrary", "arbitrary")),
        name="sample_attn",
    )(page_table, w, proj, proj, bias_last, bias_new, bias_far, lam_qk, subln_w,
      *([cache_k] * n_group), *([cache_v] * n_group))


def _mlstm_kernel(q_ref, k_ref, v_ref, o_gate_ref, g_ref, nw_ref, c0_ref, n0_ref, m0_ref,
                  h_ref, c_ref, n_ref, m_ref, *, rows, chunk):
    c_idx = pl.program_id(1)

    @pl.when(c_idx == 0)
    def _():
        c_ref[...] = c0_ref[...]
        n_ref[...] = n0_ref[...]
        m_ref[...] = m0_ref[...]

    row = lax.broadcasted_iota(jnp.int32, (chunk, chunk), 0)
    col = lax.broadcasted_iota(jnp.int32, (chunk, chunk), 1)
    tril = row >= col
    eye = row == col

    def padded(x):
        if rows == chunk:
            return x
        return jnp.concatenate([x, jnp.zeros((chunk - rows, x.shape[1]), x.dtype)], axis=0)

    for h in range(ML_HEADS):
        sl = slice(h * ML_DQK, (h + 1) * ML_DQK)
        q = padded(q_ref[:, sl])
        k = padded(k_ref[:, sl]) * (ML_DQK ** -0.5)
        v = padded(v_ref[:, sl])
        ig = g_ref[h:h + 1, :]
        gf = g_ref[ML_HEADS + h:ML_HEADS + h + 1, :]
        lf = jnp.minimum(gf, 0.0) - jnp.log1p(jnp.exp(-jnp.abs(gf)))
        c_state = c_ref[h]
        n_state = n_ref[h]
        m_state = m_ref[:, h:h + 1]

        b_col = jnp.sum(jnp.where(tril, lf, 0.0), axis=1, keepdims=True)
        b_row = jnp.sum(jnp.where(eye, b_col, 0.0), axis=0, keepdims=True)
        logw = jnp.where(tril, b_col - b_row + ig, NEG_BIG)
        m_inter = b_col + m_state
        m_t = jnp.maximum(jnp.max(logw, axis=1, keepdims=True), m_inter)
        inter = jnp.exp(m_inter - m_t)
        qb = q.astype(BF16)
        kb = k.astype(BF16)
        s = lax.dot_general(qb, kb, (((1,), (1,)), ((), ())), preferred_element_type=F32)
        s = s * jnp.exp(logw - m_t)
        num = jnp.dot(s.astype(BF16), v.astype(BF16), preferred_element_type=F32)
        num = num + inter * lax.dot_general(qb, c_state.astype(BF16), (((1,), (1,)), ((), ())),
                                            preferred_element_type=F32)
        den = jnp.sum(s, axis=1, keepdims=True) + inter * jnp.sum(q * n_state, axis=1, keepdims=True)
        hh = num / jnp.maximum(jnp.abs(den), jnp.exp(-m_t))

        b_last = jnp.sum(lf, axis=1, keepdims=True)
        logw_end = b_last - b_row + ig
        m_new = jnp.maximum(b_last + m_state, jnp.max(logw_end, axis=1, keepdims=True))
        w_end = jnp.exp(logw_end - m_new)
        decay = jnp.exp(b_last + m_state - m_new)
        w_col = jnp.sum(jnp.where(eye, w_end, 0.0), axis=1, keepdims=True)
        vw = (v * w_col).astype(BF16)
        c_ref[h] = decay * c_state + lax.dot_general(vw, kb, (((0,), (0,)), ((), ())),
                                                     preferred_element_type=F32)
        n_ref[h] = decay * n_state + jnp.sum(k * w_col, axis=0, keepdims=True)
        m_ref[:, h:h + 1] = m_new

        hh = hh[:rows]
        mc = hh - jnp.mean(hh, axis=1, keepdims=True)
        y = mc * lax.rsqrt(jnp.mean(mc * mc, axis=1, keepdims=True) + LN_EPS) * nw_ref[:, sl]
        h_ref[:, sl] = (jax.nn.sigmoid(o_gate_ref[:, sl]) * y).astype(h_ref.dtype)


def _mlstm(proj, gates_t, norm_w, c0, n0, m0, layer, batch, seq, rows_per_block):
    nc = seq // rows_per_block
    chunk = gates_t.shape[2] // nc
    cq, ck, cv, co = (3 * ATT_W // ML_W, 3 * ATT_W // ML_W + 1, 3 * ATT_W // ML_W + 2,
                      3 * ATT_W // ML_W + 3)
    rows = lambda col: pl.BlockSpec((rows_per_block, ML_W), lambda b, c: (b * nc + c, col))
    state_c = pl.BlockSpec((None, ML_HEADS, ML_DV, ML_DQK), lambda b, c: (b, 0, 0, 0))
    state_n = pl.BlockSpec((None, ML_HEADS, 1, ML_DQK), lambda b, c: (b, 0, 0, 0))
    state_m = pl.BlockSpec((None, 1, ML_HEADS), lambda b, c: (b, 0, 0))
    return pl.pallas_call(
        functools.partial(_mlstm_kernel, rows=rows_per_block, chunk=chunk),
        grid=(batch, nc),
        in_specs=[rows(cq), rows(ck), rows(cv), rows(co),
                  pl.BlockSpec((None, 2 * ML_HEADS, chunk), lambda b, c: (b, 0, c)),
                  pl.BlockSpec((None, 1, ML_W), lambda b, c: (layer, 0, 0)),
                  state_c, state_n, state_m],
        out_specs=[pl.BlockSpec((rows_per_block, ML_W), lambda b, c: (b * nc + c, 0)),
                   state_c, state_n, state_m],
        out_shape=[jax.ShapeDtypeStruct((batch * seq, ML_W), BF16),
                   jax.ShapeDtypeStruct((batch, ML_HEADS, ML_DV, ML_DQK), F32),
                   jax.ShapeDtypeStruct((batch, ML_HEADS, 1, ML_DQK), F32),
                   jax.ShapeDtypeStruct((batch, 1, ML_HEADS), F32)],
        compiler_params=_cparams(("parallel", "arbitrary")),
        name="mlstm",
    )(proj, proj, proj, proj, gates_t, norm_w, c0, n0, m0)


def _merge_kernel(ao_ref, mh_ref, wa_ref, wm_ref, ga_ref, gm_ref, ba_ref, bm_ref, o_ref):
    ta = jnp.dot(ao_ref[...], wa_ref[...], preferred_element_type=F32)
    tmm = jnp.dot(mh_ref[...], wm_ref[...], preferred_element_type=F32)
    ga = jax.nn.sigmoid(ga_ref[...] + ba_ref[...])
    gm = jax.nn.sigmoid(gm_ref[...] + bm_ref[...])
    o_ref[...] = (ga * ta + gm * tmm).astype(o_ref.dtype)


def _merge(ao, mh, proj, w_ba, w_bm, b_gate, layer, tm, tn):
    m = ao.shape[0]
    nd = D_MODEL // tn
    ga0 = COL_GATE // tn
    return pl.pallas_call(
        _merge_kernel,
        grid=(m // tm, nd),
        in_specs=[
            pl.BlockSpec((tm, ATT_W), lambda i, j: (i, 0)),
            pl.BlockSpec((tm, ML_W), lambda i, j: (i, 0)),
            pl.BlockSpec((None, ATT_W, tn), lambda i, j: (layer, 0, j)),
            pl.BlockSpec((None, ML_W, tn), lambda i, j: (layer, 0, j)),
            pl.BlockSpec((tm, tn), lambda i, j: (i, ga0 + j)),
            pl.BlockSpec((tm, tn), lambda i, j: (i, ga0 + nd + j)),
            pl.BlockSpec((None, 1, tn), lambda i, j: (layer, 0, j)),
            pl.BlockSpec((None, 1, tn), lambda i, j: (layer, 0, nd + j)),
        ],
        out_specs=pl.BlockSpec((tm, tn), lambda i, j: (i, j)),
        out_shape=jax.ShapeDtypeStruct((m, D_MODEL), BF16),
        compiler_params=_cparams(("parallel", "parallel")),
        name="merge",
    )(ao, mh, w_ba, w_bm, proj, proj, b_gate, b_gate)


def _out_kernel(x_ref, mg_ref, w_ref, g_ref, b_ref, o_ref, ob_ref):
    y = DN_ALPHA * x_ref[...] + jnp.dot(mg_ref[...], w_ref[...], preferred_element_type=F32)
    o = _layer_norm_rows(y, g_ref[...], b_ref[...])
    o_ref[...] = o
    ob_ref[...] = o.astype(BF16)


def _out_proj_ln(x, merged, w_out, ln_g, ln_b, layer, tm):
    m = x.shape[0]
    return pl.pallas_call(
        _out_kernel,
        grid=(m // tm,),
        in_specs=[
            pl.BlockSpec((tm, D_MODEL), lambda i: (i, 0)),
            pl.BlockSpec((tm, D_MODEL), lambda i: (i, 0)),
            pl.BlockSpec((None, D_MODEL, D_MODEL), lambda i: (layer, 0, 0)),
            pl.BlockSpec((None, None, 1, D_MODEL), lambda i: (layer, 1, 0, 0)),
            pl.BlockSpec((None, None, 1, D_MODEL), lambda i: (layer, 1, 0, 0)),
        ],
        out_specs=[pl.BlockSpec((tm, D_MODEL), lambda i: (i, 0)),
                   pl.BlockSpec((tm, D_MODEL), lambda i: (i, 0))],
        out_shape=[jax.ShapeDtypeStruct((m, D_MODEL), F32),
                   jax.ShapeDtypeStruct((m, D_MODEL), BF16)],
        compiler_params=_cparams(("parallel",)),
        name="out_proj_ln",
    )(x, merged, w_out, ln_g, ln_b)


def _gates_transposed(proj, b_if, batch, seq, pad_to):
    gif = proj[:, COL_IF:COL_IF + N_IF] + b_if
    gt = jnp.transpose(gif.reshape(batch, seq, N_IF), (0, 2, 1))
    if pad_to > seq:
        pad = jnp.concatenate([jnp.full((batch, ML_HEADS, pad_to - seq), NEG_BIG, F32),
                               jnp.full((batch, ML_HEADS, pad_to - seq), 1e4, F32)], axis=1)
        gt = jnp.concatenate([gt, pad], axis=2)
    return gt


def kernel(x_prompt, x_sample, cache_k, cache_v, page_table, state_C, state_n, state_m, rel_bias,
           w_in, b_gate, b_if, lam_qk, subln_w, mlstm_norm_w, w_branch_attn, w_branch_mlstm,
           w_out, ffn_up, ffn_down, ln_g, ln_b):
    batch, seq, _ = x_prompt.shape
    dec_batch, dec_seq, _ = x_sample.shape
    mp, ms = batch * seq, dec_batch * dec_seq
    tm_p, tm_s = 512, ms
    tq = 512
    prompt_chunk = 256

    w_cat = jnp.concatenate(
        [w_in[:, :, :N_MAIN], w_in[:, :, N_MAIN + N_IF:], w_in[:, :, N_MAIN:N_MAIN + N_IF],
         jnp.zeros((DEPTH, D_MODEL, N_IF_PAD - N_IF), w_in.dtype)], axis=2).astype(BF16)
    up_b = ffn_up.astype(BF16)
    down_b = ffn_down.astype(BF16)
    wba_b = w_branch_attn.astype(BF16)
    wbm_b = w_branch_mlstm.astype(BF16)
    wout_b = w_out.astype(BF16)
    ln_g = ln_g.reshape(DEPTH, 3, 1, D_MODEL)
    ln_b = ln_b.reshape(DEPTH, 3, 1, D_MODEL)
    b_gate3 = b_gate.reshape(DEPTH, 1, N_GATE)
    subln3 = subln_w.reshape(DEPTH, 1, DA_V_DIM)
    subln_col = subln_w.reshape(DEPTH, DA_V_DIM, 1)
    normw3 = mlstm_norm_w.reshape(DEPTH, 1, ML_W)
    bias_tiles = _prompt_bias_tiles(rel_bias, tq)

    zero_c = jnp.zeros((batch, ML_HEADS, ML_DV, ML_DQK), F32)
    zero_n = jnp.zeros((batch, ML_HEADS, 1, ML_DQK), F32)
    zero_m = jnp.zeros((batch, 1, ML_HEADS), F32)

    xp = x_prompt.reshape(mp, D_MODEL)
    xs = x_sample.reshape(ms, D_MODEL)
    xpb = xp.astype(BF16)
    xsb = xs.astype(BF16)

    outs = {name: [] for name in ('kp', 'vp', 'ks', 'vs', 'cp', 'np', 'mp', 'cs', 'ns', 'ms')}
    for l in range(DEPTH):
        lam_init = 0.8 - 0.6 * math.exp(-0.3 * l)

        def rowwise_pre(x, xb, tm):
            x1, x1b = _ffn_ln(x, xb, up_b, down_b, ln_g, ln_b, l, 0, 0, tm, 512)
            proj = _in_proj(x1b, w_cat, l, tm, 1280)
            return x1, proj

        def rowwise_post(x1, proj, ao, mh, tm):
            merged = _merge(ao, mh, proj, wba_b, wbm_b, b_gate3, l, tm, 512)
            x2, x2b = _out_proj_ln(x1, merged, wout_b, ln_g, ln_b, l, min(tm, 256))
            return _ffn_ln(x2, x2b, up_b, down_b, ln_g, ln_b, l, 1, 2, tm, 512)

        x1, proj = rowwise_pre(xp, xpb, tm_p)
        ao = _prompt_attn(proj, bias_tiles, lam_qk, subln_col, l, lam_init, batch, seq, tq)
        gt = _gates_transposed(proj, b_if[l], batch, seq, seq)
        mh, c_new, n_new, m_new = _mlstm(proj, gt, normw3, zero_c, zero_n, zero_m, l, batch, seq,
                                         prompt_chunk)
        xp, xpb = rowwise_post(x1, proj, ao, mh, tm_p)
        outs['kp'].append(proj[:, ATT_W:2 * ATT_W].reshape(batch, seq, DA_HEADS, DA_V_DIM))
        outs['vp'].append(proj[:, 2 * ATT_W:3 * ATT_W].reshape(batch, seq, DA_HEADS, DA_V_DIM))
        outs['cp'].append(c_new)
        outs['np'].append(n_new.reshape(batch, ML_HEADS, ML_DQK))
        outs['mp'].append(m_new.reshape(batch, ML_HEADS))

        x1, proj = rowwise_pre(xs, xsb, tm_s)
        ao = _sample_attn(proj, cache_k, cache_v, page_table, rel_bias, lam_qk, subln3, l, lam_init,
                          dec_batch, dec_seq, 8)
        gt = _gates_transposed(proj, b_if[l], dec_batch, dec_seq, 128)
        mh, c_new, n_new, m_new = _mlstm(proj, gt, normw3, state_C[l],
                                         state_n[l].reshape(dec_batch, ML_HEADS, 1, ML_DQK),
                                         state_m[l].reshape(dec_batch, 1, ML_HEADS), l, dec_batch,
                                         dec_seq, dec_seq)
        xs, xsb = rowwise_post(x1, proj, ao, mh, tm_s)
        outs['ks'].append(proj[:, ATT_W:2 * ATT_W].reshape(dec_batch, dec_seq, DA_HEADS, DA_V_DIM))
        outs['vs'].append(proj[:, 2 * ATT_W:3 * ATT_W].reshape(dec_batch, dec_seq, DA_HEADS, DA_V_DIM))
        outs['cs'].append(c_new)
        outs['ns'].append(n_new.reshape(dec_batch, ML_HEADS, ML_DQK))
        outs['ms'].append(m_new.reshape(dec_batch, ML_HEADS))

    st = lambda name: jnp.stack(outs[name])
    return (xp.reshape(batch, seq, D_MODEL), xs.reshape(dec_batch, dec_seq, D_MODEL),
            st('kp'), st('vp'), st('ks'), st('vs'), st('cp'), st('np'), st('mp'),
            st('cs'), st('ns'), st('ms'))
```

```python
import functools
import math

import numpy as np
import jax
import jax.numpy as jnp
from jax import lax
from jax.experimental import pallas as pl
from jax.experimental.pallas import tpu as pltpu

F32 = jnp.float32
BF16 = jnp.bfloat16

D_MODEL = 2048
DEPTH = 4
PAGE_SIZE = 128
DA_HEADS = 8
DA_HEAD_DIM = 64
DA_V_DIM = 2 * DA_HEAD_DIM
ML_HEADS = 4
ML_DQK = 256
ML_DV = 256
D_FF = 5632
N_BUCKETS = 32
MAX_DISTANCE = 128
LN_EPS = 1e-5
DN_ALPHA = (2.0 * DEPTH) ** 0.25

ATT_W = DA_HEADS * DA_V_DIM
ML_W = ML_HEADS * ML_DV
N_MAIN = 3 * ATT_W + 4 * ML_W
N_GATE = 2 * D_MODEL
N_IF = 2 * ML_HEADS
N_IF_PAD = 256
N_PROJ = N_MAIN + N_GATE + N_IF_PAD
COL_GATE = N_MAIN
COL_IF = N_MAIN + N_GATE

NEG_BIG = -1e30
LOG2E = math.log2(math.e)
QCOLS = 256
ACC_PAD = 16
VMEM_LIMIT = 48 * 1024 * 1024


def _cparams(sem):
    return pltpu.CompilerParams(dimension_semantics=sem, vmem_limit_bytes=VMEM_LIMIT)


def _layer_norm_rows(y, g, b):
    mu = jnp.mean(y, axis=-1, keepdims=True)
    yc = y - mu
    var = jnp.mean(yc * yc, axis=-1, keepdims=True)
    return yc * lax.rsqrt(var + LN_EPS) * g + b


def _ffn_kernel(x_ref, xb_ref, wa_ref, wb_ref, wd_ref, g_ref, b_ref, o_ref, ob_ref, acc_ref):
    j = pl.program_id(1)

    @pl.when(j == 0)
    def _():
        acc_ref[...] = jnp.zeros_like(acc_ref)

    xb = xb_ref[...]
    a = jnp.dot(xb, wa_ref[...], preferred_element_type=F32)
    b = jnp.dot(xb, wb_ref[...], preferred_element_type=F32)
    h = (a * jax.nn.sigmoid(a) * b).astype(BF16)
    acc_ref[...] += jnp.dot(h, wd_ref[...], preferred_element_type=F32)

    @pl.when(j == pl.num_programs(1) - 1)
    def _():
        y = DN_ALPHA * x_ref[...] + 0.5 * acc_ref[...]
        o = _layer_norm_rows(y, g_ref[...], b_ref[...])
        o_ref[...] = o
        ob_ref[...] = o.astype(BF16)


def _ffn_ln(x, xb, w_up, w_down, ln_g, ln_b, layer, which, ln_idx, tm, tf):
    m = x.shape[0]
    nf = D_FF // tf
    return pl.pallas_call(
        _ffn_kernel,
        grid=(m // tm, nf),
        in_specs=[
            pl.BlockSpec((tm, D_MODEL), lambda i, j: (i, 0)),
            pl.BlockSpec((tm, D_MODEL), lambda i, j: (i, 0)),
            pl.BlockSpec((None, None, D_MODEL, tf), lambda i, j: (layer, which, 0, j)),
            pl.BlockSpec((None, None, D_MODEL, tf), lambda i, j: (layer, which, 0, j + nf)),
            pl.BlockSpec((None, None, tf, D_MODEL), lambda i, j: (layer, which, j, 0)),
            pl.BlockSpec((None, None, 1, D_MODEL), lambda i, j: (layer, ln_idx, 0, 0)),
            pl.BlockSpec((None, None, 1, D_MODEL), lambda i, j: (layer, ln_idx, 0, 0)),
        ],
        out_specs=[
            pl.BlockSpec((tm, D_MODEL), lambda i, j: (i, 0)),
            pl.BlockSpec((tm, D_MODEL), lambda i, j: (i, 0)),
        ],
        out_shape=[jax.ShapeDtypeStruct((m, D_MODEL), F32),
                   jax.ShapeDtypeStruct((m, D_MODEL), BF16)],
        scratch_shapes=[pltpu.VMEM((tm, D_MODEL), F32)],
        compiler_params=_cparams(("parallel", "arbitrary")),
        name="ffn_ln",
    )(x, xb, w_up, w_up, w_down, ln_g, ln_b)


def _proj_kernel(xb_ref, w_ref, o_ref):
    o_ref[...] = jnp.dot(xb_ref[...], w_ref[...], preferred_element_type=F32)


def _in_proj(xb, w_cat, layer, tm, tn):
    m = xb.shape[0]
    return pl.pallas_call(
        _proj_kernel,
        grid=(m // tm, N_PROJ // tn),
        in_specs=[
            pl.BlockSpec((tm, D_MODEL), lambda i, j: (i, 0)),
            pl.BlockSpec((None, D_MODEL, tn), lambda i, j: (layer, 0, j)),
        ],
        out_specs=pl.BlockSpec((tm, tn), lambda i, j: (i, j)),
        out_shape=jax.ShapeDtypeStruct((m, N_PROJ), F32),
        compiler_params=_cparams(("parallel", "parallel")),
        name="in_proj",
    )(xb, w_cat)


def _t5_bucket_np(dist):
    n = np.maximum(dist, 0)
    max_exact = N_BUCKETS // 2
    nf = np.maximum(n, 1).astype(np.float32)
    large = max_exact + (np.log(nf / np.float32(max_exact)) / np.float32(math.log(MAX_DISTANCE / max_exact))
                         * np.float32(N_BUCKETS - max_exact)).astype(np.int32)
    large = np.minimum(large, N_BUCKETS - 1)
    return np.where(n < max_exact, n, large).astype(np.int32)


def _lambda_scalar(lq, lam_init):
    s01 = jnp.sum(lq[0:1, :] * lq[1:2, :], axis=1, keepdims=True)
    s23 = jnp.sum(lq[2:3, :] * lq[3:4, :], axis=1, keepdims=True)
    return jnp.exp(s01) - jnp.exp(s23) + lam_init


def _sub_norm(o, w_row, lam_init):
    o = o * lax.rsqrt(jnp.mean(o * o, axis=-1, keepdims=True) + LN_EPS)
    return o * w_row * (1.0 - lam_init)


def _pattn_kernel(qi_tab, ki_tab, q_ref, k_ref, v_ref, bias_ref, lq_ref, sw_ref, o_ref,
                  qt_sc, m_sc, acc_sc, *, tq, lam_init):
    t = pl.program_id(2)
    qi = qi_tab[t]
    ki = ki_tab[t]
    tk = k_ref.shape[0]

    @pl.when(ki == 0)
    def _():
        q = q_ref[...] * (DA_HEAD_DIM ** -0.5 * LOG2E)
        lane = lax.broadcasted_iota(jnp.int32, q.shape, 1)
        qt_sc[:, :tq] = jnp.where(lane < DA_HEAD_DIM, q, 0.0).T.astype(BF16)
        qt_sc[:, tq:] = jnp.where(lane >= DA_HEAD_DIM, q, 0.0).T.astype(BF16)
        m_sc[...] = jnp.full_like(m_sc, -jnp.inf)
        acc_sc[...] = jnp.zeros_like(acc_sc)

    def accumulate(with_bias, diagonal):
        kb = k_ref[...].astype(BF16)
        vt = jnp.concatenate([v_ref[...].T, jnp.ones((ACC_PAD, tk), F32)], axis=0).astype(BF16)
        n_chunks = 2 * tq // QCOLS
        cols = [slice(c * QCOLS, (c + 1) * QCOLS) for c in range(n_chunks)]
        keys = [((c * QCOLS) % tq + QCOLS) if diagonal else tk for c in range(n_chunks)]

        def scores(c):
            s = jnp.dot(kb[:keys[c]], qt_sc[:, cols[c]], preferred_element_type=F32)
            if with_bias:
                b0 = (c * QCOLS) % tq
                s = s + bias_ref[:keys[c], b0:b0 + QCOLS]
            return s

        def softmax(c, s):
            m_old = m_sc[:, cols[c]]
            m_new = jnp.maximum(m_old, jnp.max(s, axis=0, keepdims=True))
            m_sc[:, cols[c]] = m_new
            return jnp.exp2(s - m_new).astype(BF16), jnp.exp2(m_old - m_new)

        def update(c, p, alpha):
            acc_sc[:, cols[c]] = alpha * acc_sc[:, cols[c]] + jnp.dot(vt[:, :keys[c]], p,
                                                                      preferred_element_type=F32)

        s_q = {0: scores(0), 1: scores(1)}
        for c in range(n_chunks):
            p, alpha = softmax(c, s_q.pop(c))
            if c + 2 < n_chunks:
                s_q[c + 2] = scores(c + 2)
            update(c, p, alpha)

    @pl.when(qi == ki)
    def _():
        accumulate(True, True)

    @pl.when(qi - ki == 1)
    def _():
        accumulate(True, False)

    @pl.when(qi - ki > 1)
    def _():
        accumulate(False, False)

    @pl.when(ki == qi)
    def _():
        acc = acc_sc[...]
        o_all = acc[:DA_V_DIM] * (1.0 / acc[DA_V_DIM:DA_V_DIM + 1])
        lam = _lambda_scalar(lq_ref[...], lam_init)
        o = o_all[:, :tq] - lam * o_all[:, tq:]
        o = o * lax.rsqrt(jnp.mean(o * o, axis=0, keepdims=True) + LN_EPS)
        o = o * (sw_ref[...] * (1.0 - lam_init))
        o_ref[...] = o.T.astype(o_ref.dtype)


def _prompt_attn(proj, bias_tiles, lam_qk, subln_col, layer, lam_init, batch, seq, tq):
    nq = seq // tq
    tri = [(qi, ki) for qi in range(nq) for ki in range(qi + 1)]
    qi_tab = jnp.asarray([a for a, _ in tri], jnp.int32)
    ki_tab = jnp.asarray([b for _, b in tri], jnp.int32)
    hq, hk, hv = 0, ATT_W // DA_V_DIM, 2 * ATT_W // DA_V_DIM

    def bias_idx(b, h, t, qt, kt):
        return (h, jnp.minimum(qt[t] - kt[t], 1), 0, 0)

    grid_spec = pltpu.PrefetchScalarGridSpec(
        num_scalar_prefetch=2,
        grid=(batch, DA_HEADS, len(tri)),
        in_specs=[
            pl.BlockSpec((tq, DA_V_DIM), lambda b, h, t, qt, kt: (b * nq + qt[t], hq + h)),
            pl.BlockSpec((tq, DA_V_DIM), lambda b, h, t, qt, kt: (b * nq + kt[t], hk + h)),
            pl.BlockSpec((tq, DA_V_DIM), lambda b, h, t, qt, kt: (b * nq + kt[t], hv + h)),
            pl.BlockSpec((None, None, tq, tq), bias_idx),
            pl.BlockSpec((None, 4, DA_HEAD_DIM), lambda b, h, t, qt, kt: (layer, 0, 0)),
            pl.BlockSpec((None, DA_V_DIM, 1), lambda b, h, t, qt, kt: (layer, 0, 0)),
        ],
        out_specs=pl.BlockSpec((tq, DA_V_DIM), lambda b, h, t, qt, kt: (b * nq + qt[t], h)),
        scratch_shapes=[pltpu.VMEM((DA_V_DIM, 2 * tq), BF16), pltpu.VMEM((1, 2 * tq), F32),
                        pltpu.VMEM((DA_V_DIM + ACC_PAD, 2 * tq), F32)],
    )
    return pl.pallas_call(
        functools.partial(_pattn_kernel, tq=tq, lam_init=lam_init),
        grid_spec=grid_spec,
        out_shape=jax.ShapeDtypeStruct((batch * seq, ATT_W), BF16),
        compiler_params=_cparams(("parallel", "parallel", "arbitrary")),
        name="prompt_attn",
    )(qi_tab, ki_tab, proj, proj, proj, bias_tiles, lam_qk, subln_col)


def _prompt_bias_tiles(rel_bias, tq):
    assert tq >= MAX_DISTANCE and np.all(_t5_bucket_np(np.arange(MAX_DISTANCE, 1 << 16)) == N_BUCKETS - 1)
    span = 2 * tq
    d_row = np.zeros(span + 1, np.int64)
    d_row[:tq] = np.arange(tq)
    d_row[span + 1 - np.arange(1, tq)] = -np.arange(1, tq)
    far = rel_bias[N_BUCKETS - 1]
    tiles = []
    for delta in range(2):
        dist = d_row + delta * tq
        y = (rel_bias[_t5_bucket_np(dist)] - far) * LOG2E
        y = jnp.where(jnp.asarray(dist >= 0)[:, None], y, NEG_BIG).T
        tiled = jnp.broadcast_to(y[:, None, :], (DA_HEADS, tq, span + 1)).reshape(DA_HEADS, tq * (span + 1))
        skew = tiled[:, :tq * span].reshape(DA_HEADS, tq, span)
        tiles.append(skew[:, :, :tq])
    return jnp.stack(tiles, axis=1)


def _page_rows(page_ref):
    return jnp.concatenate([page_ref[pl.ds(h, PAGE_SIZE, stride=DA_HEADS), :] for h in range(DA_HEADS)],
                           axis=1).astype(BF16)


def _sattn_kernel(pt_ref,w_ref, kn_ref, vn_ref, bl_ref, bn_ref, c_ref, lq_ref, sw_ref, *rest,
                  n_group, n_pages, n_tok, lam_init):
    k_refs = rest[:n_group]
    v_refs = rest[n_group:2 * n_group]
    o_ref = rest[2 * n_group]
    s_sc, acc_sc = rest[2 * n_group + 1:]
    phase = pl.program_id(1)
    step = pl.program_id(2)
    n_steps = pl.num_programs(2)
    past = n_pages * PAGE_SIZE
    half = DA_HEADS * n_tok

    @pl.when(phase == 0)
    def _():
        w = w_ref[...]

        @pl.when(step == 0)
        def _():
            kn = kn_ref[...]
            kn = jnp.concatenate([kn, jnp.zeros_like(kn)], axis=0).astype(BF16)
            sn = jnp.dot(kn, w, preferred_element_type=F32)[:n_tok]
            s_sc[pl.ds(past, n_tok), :] = sn + bn_ref[...]

        for g in range(n_group):
            page = step * n_group + g
            s = jnp.dot(_page_rows(k_refs[g]), w, preferred_element_type=F32)
            bias = jnp.where(page == n_pages - 1, bl_ref[...], c_ref[...])
            s_sc[pl.ds(pl.multiple_of(page * PAGE_SIZE, PAGE_SIZE), PAGE_SIZE), :] = s + bias

    @pl.when(phase == 1)
    def _():
        @pl.when(step == 0)
        def _():
            chunk = 512
            n_chunks = past // chunk

            def max_body(i, m):
                blk = s_sc[pl.ds(pl.multiple_of(i * chunk, chunk), chunk), :]
                return jnp.maximum(m, jnp.max(blk, axis=0, keepdims=True))

            s_new = s_sc[pl.ds(past, n_tok), :]
            m = lax.fori_loop(0, n_chunks, max_body, jnp.max(s_new, axis=0, keepdims=True))

            def sum_body(i, l):
                sl = pl.ds(pl.multiple_of(i * chunk, chunk), chunk)
                p = jnp.exp(s_sc[sl, :] - m)
                s_sc[sl, :] = p
                return l + jnp.sum(p, axis=0, keepdims=True)

            p_new = jnp.exp(s_new - m)
            l = lax.fori_loop(0, n_chunks, sum_body, jnp.sum(p_new, axis=0, keepdims=True))
            inv_l = 1.0 / l
            lam = _lambda_scalar(lq_ref[...], lam_init)

            def comb(p):
                pn = p * inv_l
                return pn - lam * pltpu.roll(pn, half, 1)

            def comb_body(i, c):
                sl = pl.ds(pl.multiple_of(i * chunk, chunk), chunk)
                s_sc[sl, :] = comb(s_sc[sl, :])
                return c

            lax.fori_loop(0, n_chunks, comb_body, 0)
            a_new_t = comb(p_new).T.astype(BF16).astype(F32)
            vn = vn_ref[...].astype(BF16).astype(F32)
            acc0 = a_new_t[:, 0:1] * vn[0:1, :]
            for t in range(1, n_tok):
                acc0 = acc0 + a_new_t[:, t:t + 1] * vn[t:t + 1, :]
            acc_sc[...] = acc0

        tot = jnp.zeros(acc_sc.shape, F32)
        for g in range(n_group):
            page = step * n_group + g
            a = s_sc[pl.ds(pl.multiple_of(page * PAGE_SIZE, PAGE_SIZE), PAGE_SIZE), :]
            at = a.T.astype(BF16)
            tot = tot + jnp.dot(at, _page_rows(v_refs[g]), preferred_element_type=F32)
        acc_sc[...] += tot

        @pl.when(step == n_steps - 1)
        def _():
            sw = sw_ref[...]
            for h in range(DA_HEADS):
                o = acc_sc[h * n_tok:(h + 1) * n_tok, h * DA_V_DIM:(h + 1) * DA_V_DIM]
                o_ref[:, h * DA_V_DIM:(h + 1) * DA_V_DIM] = _sub_norm(o, sw, lam_init).astype(o_ref.dtype)


def _sample_attn(proj, cache_k, cache_v, page_table, rel_bias, lam_qk, subln_w, layer, lam_init,
                 dec_batch, n_tok, n_group):
    n_pages = page_table.shape[1]
    assert n_pages % n_group == 0 and n_tok == 8
    past = n_pages * PAGE_SIZE
    n_steps = n_pages // n_group
    half = DA_HEADS * n_tok
    width = DA_HEADS * DA_V_DIM

    q = proj[:, :ATT_W].reshape(dec_batch, n_tok, DA_HEADS, 2, DA_HEAD_DIM) * (DA_HEAD_DIM ** -0.5)
    w = jnp.einsum('bthjd,hH,jJ->bhjdJHt', q, jnp.eye(DA_HEADS, dtype=F32), jnp.eye(2, dtype=F32))
    w = w.reshape(dec_batch, width, 2 * half).astype(BF16)

    tok = np.arange(n_tok)

    def col_bias(dist):
        b = rel_bias[_t5_bucket_np(dist)]
        b = jnp.where(jnp.asarray(dist >= 0)[..., None], b, NEG_BIG)
        b = jnp.transpose(b, (0, 2, 1)).reshape(dist.shape[0], half)
        return jnp.concatenate([b, b], axis=1)

    assert np.all(_t5_bucket_np(np.arange(PAGE_SIZE + 1, past + n_tok + 1)) == N_BUCKETS - 1)
    bias_last = col_bias((past + tok)[None, :] - (past - PAGE_SIZE + np.arange(PAGE_SIZE))[:, None])
    bias_new = col_bias(tok[None, :] - tok[:, None])
    bias_far = col_bias(np.full((1, n_tok), PAGE_SIZE + 1))

    page_block = (None, None, PAGE_SIZE * DA_HEADS, DA_V_DIM)
    cache_k = cache_k.reshape(cache_k.shape[0], cache_k.shape[1], PAGE_SIZE * DA_HEADS, DA_V_DIM)
    cache_v = cache_v.reshape(cache_v.shape[0], cache_v.shape[1], PAGE_SIZE * DA_HEADS, DA_V_DIM)
    kcol, vcol = ATT_W // width, 2 * ATT_W // width

    def k_idx(g):
        def idx(b, ph, s, pt):
            s_eff = jnp.where(ph == 0, s, n_steps - 1)
            return (layer, pt[b, s_eff * n_group + g], 0, 0)
        return idx

    def v_idx(g):
        def idx(b, ph, s, pt):
            s_eff = jnp.where(ph == 0, 0, s)
            return (layer, pt[b, s_eff * n_group + g], 0, 0)
        return idx

    const2 = lambda b, ph, s, pt: (0, 0)
    grid_spec = pltpu.PrefetchScalarGridSpec(
        num_scalar_prefetch=1,
        grid=(dec_batch, 2, n_steps),
        in_specs=[
            pl.BlockSpec((None, width, 2 * half), lambda b, ph, s, pt: (b, 0, 0)),
            pl.BlockSpec((n_tok, width), lambda b, ph, s, pt: (b, kcol)),
            pl.BlockSpec((n_tok, width), lambda b, ph, s, pt: (b, vcol)),
            pl.BlockSpec((PAGE_SIZE, 2 * half), const2),
            pl.BlockSpec((n_tok, 2 * half), const2),
            pl.BlockSpec((1, 2 * half), const2),
            pl.BlockSpec((None, 4, DA_HEAD_DIM), lambda b, ph, s, pt: (layer, 0, 0)),
            pl.BlockSpec((None, 1, DA_V_DIM), lambda b, ph, s, pt: (layer, 0, 0)),
        ] + [pl.BlockSpec(page_block,k_idx(g)) for g in range(n_group)]
          + [pl.BlockSpec(page_block,v_idx(g)) for g in range(n_group)],
        out_specs=pl.BlockSpec((n_tok, width), lambda b, ph, s, pt: (b, 0)),
        scratch_shapes=[pltpu.VMEM((past + n_tok, 2 * half), F32),
                        pltpu.VMEM((2 * half, width), F32)],
    )
    return pl.pallas_call(
        functools.partial(_sattn_kernel, n_group=n_group, n_pages=n_pages, n_tok=n_tok,
                          lam_init=lam_init),
        grid_spec=grid_spec,
        out_shape=jax.ShapeDtypeStruct((dec_batch * n_tok, width), BF16),
        compiler_params=_cparams(("parallel", "arbitrary", "arbitrary")),
        name="sample_attn",
    )(page_table, w, proj, proj, bias_last, bias_new, bias_far, lam_qk, subln_w,
      *([cache_k] * n_group), *([cache_v] * n_group))


def _mlstm_kernel(q_ref, k_ref, v_ref, o_gate_ref, g_ref, nw_ref, c0_ref, n0_ref, m0_ref,
                  h_ref, c_ref, n_ref, m_ref, *, rows, chunk):
    c_idx = pl.program_id(1)

    @pl.when(c_idx == 0)
    def _():
        c_ref[...] = c0_ref[...]
        n_ref[...] = n0_ref[...]
        m_ref[...] = m0_ref[...]

    row = lax.broadcasted_iota(jnp.int32, (chunk, chunk), 0)
    col = lax.broadcasted_iota(jnp.int32, (chunk, chunk), 1)
    tril = row >= col
    eye = row == col

    def padded(x):
        if rows == chunk:
            return x
        return jnp.concatenate([x, jnp.zeros((chunk - rows, x.shape[1]), x.dtype)], axis=0)

    for h in range(ML_HEADS):
        sl = slice(h * ML_DQK, (h + 1) * ML_DQK)
        q = padded(q_ref[:, sl])
        k = padded(k_ref[:, sl]) * (ML_DQK ** -0.5)
        v = padded(v_ref[:, sl])
        ig = g_ref[h:h + 1, :]
        gf = g_ref[ML_HEADS + h:ML_HEADS + h + 1, :]
        lf = jnp.minimum(gf, 0.0) - jnp.log1p(jnp.exp(-jnp.abs(gf)))
        c_state = c_ref[h]
        n_state = n_ref[h]
        m_state = m_ref[:, h:h + 1]

        b_col = jnp.sum(jnp.where(tril, lf, 0.0), axis=1, keepdims=True)
        b_row = jnp.sum(jnp.where(eye, b_col, 0.0), axis=0, keepdims=True)
        logw = jnp.where(tril, b_col - b_row + ig, NEG_BIG)
        m_inter = b_col + m_state
        m_t = jnp.maximum(jnp.max(logw, axis=1, keepdims=True), m_inter)
        inter = jnp.exp(m_inter - m_t)
        qb = q.astype(BF16)
        kb = k.astype(BF16)
        s = lax.dot_general(qb, kb, (((1,), (1,)), ((), ())), preferred_element_type=F32)
        s = s * jnp.exp(logw - m_t)
        num = jnp.dot(s.astype(BF16), v.astype(BF16), preferred_element_type=F32)
        num = num + inter * lax.dot_general(qb, c_state.astype(BF16), (((1,), (1,)), ((), ())),
                                            preferred_element_type=F32)
        den = jnp.sum(s, axis=1, keepdims=True) + inter * jnp.sum(q * n_state, axis=1, keepdims=True)
        hh = num / jnp.maximum(jnp.abs(den), jnp.exp(-m_t))

        b_last = jnp.sum(lf, axis=1, keepdims=True)
        logw_end = b_last - b_row + ig
        m_new = jnp.maximum(b_last + m_state, jnp.max(logw_end, axis=1, keepdims=True))
        w_end = jnp.exp(logw_end - m_new)
        decay = jnp.exp(b_last + m_state - m_new)
        w_col = jnp.sum(jnp.where(eye, w_end, 0.0), axis=1, keepdims=True)
        vw = (v * w_col).astype(BF16)
        c_ref[h] = decay * c_state + lax.dot_general(vw, kb, (((0,), (0,)), ((), ())),
                                                     preferred_element_type=F32)
        n_ref[h] = decay * n_state + jnp.sum(k * w_col, axis=0, keepdims=True)
        m_ref[:, h:h + 1] = m_new

        hh = hh[:rows]
        mc = hh - jnp.mean(hh, axis=1, keepdims=True)
        y = mc * lax.rsqrt(jnp.mean(mc * mc, axis=1, keepdims=True) + LN_EPS) * nw_ref[:, sl]
        h_ref[:, sl] = (jax.nn.sigmoid(o_gate_ref[:, sl]) * y).astype(h_ref.dtype)


def _mlstm(proj, gates_t, norm_w, c0, n0, m0, layer, batch, seq, rows_per_block):
    nc = seq // rows_per_block
    chunk = gates_t.shape[2] // nc
    cq, ck, cv, co = (3 * ATT_W // ML_W, 3 * ATT_W // ML_W + 1, 3 * ATT_W // ML_W + 2,
                      3 * ATT_W // ML_W + 3)
    rows = lambda col: pl.BlockSpec((rows_per_block, ML_W), lambda b, c: (b * nc + c, col))
    state_c = pl.BlockSpec((None, ML_HEADS, ML_DV, ML_DQK), lambda b, c: (b, 0, 0, 0))
    state_n = pl.BlockSpec((None, ML_HEADS, 1, ML_DQK), lambda b, c: (b, 0, 0, 0))
    state_m = pl.BlockSpec((None, 1, ML_HEADS), lambda b, c: (b, 0, 0))
    return pl.pallas_call(
        functools.partial(_mlstm_kernel, rows=rows_per_block, chunk=chunk),
        grid=(batch, nc),
        in_specs=[rows(cq), rows(ck), rows(cv), rows(co),
                  pl.BlockSpec((None, 2 * ML_HEADS, chunk), lambda b, c: (b, 0, c)),
                  pl.BlockSpec((None, 1, ML_W), lambda b, c: (layer, 0, 0)),
                  state_c, state_n, state_m],
        out_specs=[pl.BlockSpec((rows_per_block, ML_W), lambda b, c: (b * nc + c, 0)),
                   state_c, state_n, state_m],
        out_shape=[jax.ShapeDtypeStruct((batch * seq, ML_W), BF16),
                   jax.ShapeDtypeStruct((batch, ML_HEADS, ML_DV, ML_DQK), F32),
                   jax.ShapeDtypeStruct((batch, ML_HEADS, 1, ML_DQK), F32),
                   jax.ShapeDtypeStruct((batch, 1, ML_HEADS), F32)],
        compiler_params=_cparams(("parallel", "arbitrary")),
        name="mlstm",
    )(proj, proj, proj, proj, gates_t, norm_w, c0, n0, m0)


def _merge_kernel(ao_ref, mh_ref, wa_ref, wm_ref, ga_ref, gm_ref, ba_ref, bm_ref, o_ref):
    ta = jnp.dot(ao_ref[...], wa_ref[...], preferred_element_type=F32)
    tmm = jnp.dot(mh_ref[...], wm_ref[...], preferred_element_type=F32)
    ga = jax.nn.sigmoid(ga_ref[...] + ba_ref[...])
    gm = jax.nn.sigmoid(gm_ref[...] + bm_ref[...])
    o_ref[...] = (ga * ta + gm * tmm).astype(o_ref.dtype)


def _merge(ao, mh, proj, w_ba, w_bm, b_gate, layer, tm, tn):
    m = ao.shape[0]
    nd = D_MODEL // tn
    ga0 = COL_GATE // tn
    return pl.pallas_call(
        _merge_kernel,
        grid=(m // tm, nd),
        in_specs=[
            pl.BlockSpec((tm, ATT_W), lambda i, j: (i, 0)),
            pl.BlockSpec((tm, ML_W), lambda i, j: (i, 0)),
            pl.BlockSpec((None, ATT_W, tn), lambda i, j: (layer, 0, j)),
            pl.BlockSpec((None, ML_W, tn), lambda i, j: (layer, 0, j)),
            pl.BlockSpec((tm, tn), lambda i, j: (i, ga0 + j)),
            pl.BlockSpec((tm, tn), lambda i, j: (i, ga0 + nd + j)),
            pl.BlockSpec((None, 1, tn), lambda i, j: (layer, 0, j)),
            pl.BlockSpec((None, 1, tn), lambda i, j: (layer, 0, nd + j)),
        ],
        out_specs=pl.BlockSpec((tm, tn), lambda i, j: (i, j)),
        out_shape=jax.ShapeDtypeStruct((m, D_MODEL), BF16),
        compiler_params=_cparams(("parallel", "parallel")),
        name="merge",
    )(ao, mh, w_ba, w_bm, proj, proj, b_gate, b_gate)


def _out_kernel(x_ref, mg_ref, w_ref, g_ref, b_ref, o_ref, ob_ref):
    y = DN_ALPHA * x_ref[...] + jnp.dot(mg_ref[...], w_ref[...], preferred_element_type=F32)
    o = _layer_norm_rows(y, g_ref[...], b_ref[...])
    o_ref[...] = o
    ob_ref[...] = o.astype(BF16)


def _out_proj_ln(x, merged, w_out, ln_g, ln_b, layer, tm):
    m = x.shape[0]
    return pl.pallas_call(
        _out_kernel,
        grid=(m // tm,),
        in_specs=[
            pl.BlockSpec((tm, D_MODEL), lambda i: (i, 0)),
            pl.BlockSpec((tm, D_MODEL), lambda i: (i, 0)),
            pl.BlockSpec((None, D_MODEL, D_MODEL), lambda i: (layer, 0, 0)),
            pl.BlockSpec((None, None, 1, D_MODEL), lambda i: (layer, 1, 0, 0)),
            pl.BlockSpec((None, None, 1, D_MODEL), lambda i: (layer, 1, 0, 0)),
        ],
        out_specs=[pl.BlockSpec((tm, D_MODEL), lambda i: (i, 0)),
                   pl.BlockSpec((tm, D_MODEL), lambda i: (i, 0))],
        out_shape=[jax.ShapeDtypeStruct((m, D_MODEL), F32),
                   jax.ShapeDtypeStruct((m, D_MODEL), BF16)],
        compiler_params=_cparams(("parallel",)),
        name="out_proj_ln",
    )(x, merged, w_out, ln_g, ln_b)


def _gates_transposed(proj, b_if, batch, seq, pad_to):
    gif = proj[:, COL_IF:COL_IF + N_IF] + b_if
    gt = jnp.transpose(gif.reshape(batch, seq, N_IF), (0, 2, 1))
    if pad_to > seq:
        pad = jnp.concatenate([jnp.full((batch, ML_HEADS, pad_to - seq), NEG_BIG, F32),
                               jnp.full((batch, ML_HEADS, pad_to - seq), 1e4, F32)], axis=1)
        gt = jnp.concatenate([gt, pad], axis=2)
    return gt


def kernel(x_prompt, x_sample, cache_k, cache_v, page_table, state_C, state_n, state_m, rel_bias,
           w_in, b_gate, b_if, lam_qk, subln_w, mlstm_norm_w, w_branch_attn, w_branch_mlstm,
           w_out, ffn_up, ffn_down, ln_g, ln_b):
    batch, seq, _ = x_prompt.shape
    dec_batch, dec_seq, _ = x_sample.shape
    mp, ms = batch * seq, dec_batch * dec_seq
    tm_p, tm_s = 512, ms
    tq = 1024
    prompt_chunk = 256

    w_cat = jnp.concatenate(
        [w_in[:, :, :N_MAIN], w_in[:, :, N_MAIN + N_IF:], w_in[:, :, N_MAIN:N_MAIN + N_IF],
         jnp.zeros((DEPTH, D_MODEL, N_IF_PAD - N_IF), w_in.dtype)], axis=2).astype(BF16)
    up_b = ffn_up.astype(BF16)
    down_b = ffn_down.astype(BF16)
    wba_b = w_branch_attn.astype(BF16)
    wbm_b = w_branch_mlstm.astype(BF16)
    wout_b = w_out.astype(BF16)
    ln_g = ln_g.reshape(DEPTH, 3, 1, D_MODEL)
    ln_b = ln_b.reshape(DEPTH, 3, 1, D_MODEL)
    b_gate3 = b_gate.reshape(DEPTH, 1, N_GATE)
    subln3 = subln_w.reshape(DEPTH, 1, DA_V_DIM)
    subln_col = subln_w.reshape(DEPTH, DA_V_DIM, 1)
    normw3 = mlstm_norm_w.reshape(DEPTH, 1, ML_W)
    bias_tiles = _prompt_bias_tiles(rel_bias, tq)

    zero_c = jnp.zeros((batch, ML_HEADS, ML_DV, ML_DQK), F32)
    zero_n = jnp.zeros((batch, ML_HEADS, 1, ML_DQK), F32)
    zero_m = jnp.zeros((batch, 1, ML_HEADS), F32)

    xp = x_prompt.reshape(mp, D_MODEL)
    xs = x_sample.reshape(ms, D_MODEL)
    xpb = xp.astype(BF16)
    xsb = xs.astype(BF16)

    outs = {name: [] for name in ('kp', 'vp', 'ks', 'vs', 'cp', 'np', 'mp', 'cs', 'ns', 'ms')}
    for l in range(DEPTH):
        lam_init = 0.8 - 0.6 * math.exp(-0.3 * l)

        def rowwise_pre(x, xb, tm):
            x1, x1b = _ffn_ln(x, xb, up_b, down_b, ln_g, ln_b, l, 0, 0, tm, 512)
            proj = _in_proj(x1b, w_cat, l, min(2 * tm, x1b.shape[0]), 1280)
            return x1, proj

        def rowwise_post(x1, proj, ao, mh, tm):
            merged = _merge(ao, mh, proj, wba_b, wbm_b, b_gate3, l, tm, 512)
            x2, x2b = _out_proj_ln(x1, merged, wout_b, ln_g, ln_b, l, min(tm, 256))
            return _ffn_ln(x2, x2b, up_b, down_b, ln_g, ln_b, l, 1, 2, tm, 512)

        x1, proj = rowwise_pre(xp, xpb, tm_p)
        ao = _prompt_attn(proj, bias_tiles, lam_qk, subln_col, l, lam_init, batch, seq, tq)
        gt = _gates_transposed(proj, b_if[l], batch, seq, seq)
        mh, c_new, n_new, m_new = _mlstm(proj, gt, normw3, zero_c, zero_n, zero_m, l, batch, seq,
                                         prompt_chunk)
        xp, xpb = rowwise_post(x1, proj, ao, mh, tm_p)
        outs['kp'].append(proj[:, ATT_W:2 * ATT_W].reshape(batch, seq, DA_HEADS, DA_V_DIM))
        outs['vp'].append(proj[:, 2 * ATT_W:3 * ATT_W].reshape(batch, seq, DA_HEADS, DA_V_DIM))
        outs['cp'].append(c_new)
        outs['np'].append(n_new.reshape(batch, ML_HEADS, ML_DQK))
        outs['mp'].append(m_new.reshape(batch, ML_HEADS))

        x1, proj = rowwise_pre(xs, xsb, tm_s)
        ao = _sample_attn(proj, cache_k, cache_v, page_table, rel_bias, lam_qk, subln3, l, lam_init,
                          dec_batch, dec_seq, 8)
        gt = _gates_transposed(proj, b_if[l], dec_batch, dec_seq, 128)
        mh, c_new, n_new, m_new = _mlstm(proj, gt, normw3, state_C[l],
                                         state_n[l].reshape(dec_batch, ML_HEADS, 1, ML_DQK),
                                         state_m[l].reshape(dec_batch, 1, ML_HEADS), l, dec_batch,
                                         dec_seq, dec_seq)
        xs, xsb = rowwise_post(x1, proj, ao, mh, tm_s)
        outs['ks'].append(proj[:, ATT_W:2 * ATT_W].reshape(dec_batch, dec_seq, DA_HEADS, DA_V_DIM))
        outs['vs'].append(proj[:, 2 * ATT_W:3 * ATT_W].reshape(dec_batch, dec_seq, DA_HEADS, DA_V_DIM))
        outs['cs'].append(c_new)
        outs['ns'].append(n_new.reshape(dec_batch, ML_HEADS, ML_DQK))
        outs['ms'].append(m_new.reshape(dec_batch, ML_HEADS))

    st = lambda name: jnp.stack(outs[name])
    return (xp.reshape(batch, seq, D_MODEL), xs.reshape(dec_batch, dec_seq, D_MODEL),
            st('kp'), st('vp'), st('ks'), st('vs'), st('cp'), st('np'), st('mp'),
            st('cs'), st('ns'), st('ms'))
```

```python
import functools
import math

import numpy as np
import jax
import jax.numpy as jnp
from jax import lax
from jax.experimental import pallas as pl
from jax.experimental.pallas import tpu as pltpu

F32 = jnp.float32
BF16 = jnp.bfloat16

D_MODEL = 2048
DEPTH = 4
PAGE_SIZE = 128
DA_HEADS = 8
DA_HEAD_DIM = 64
DA_V_DIM = 2 * DA_HEAD_DIM
ML_HEADS = 4
ML_DQK = 256
ML_DV = 256
D_FF = 5632
N_BUCKETS = 32
MAX_DISTANCE = 128
LN_EPS = 1e-5
DN_ALPHA = (2.0 * DEPTH) ** 0.25

ATT_W = DA_HEADS * DA_V_DIM
ML_W = ML_HEADS * ML_DV
N_MAIN = 3 * ATT_W + 4 * ML_W
N_GATE = 2 * D_MODEL
N_IF = 2 * ML_HEADS
N_IF_PAD = 256
N_PROJ = N_MAIN + N_GATE
PROJ_TN = 1024
KV_COL_BLOCK = ATT_W // PROJ_TN
COL_GATE = N_MAIN

NEG_BIG = -1e30
LOG2E = math.log2(math.e)
QCOLS = 256
ACC_PAD = 16
VMEM_LIMIT = 48 * 1024 * 1024


def _cparams(sem):
    return pltpu.CompilerParams(dimension_semantics=sem, vmem_limit_bytes=VMEM_LIMIT)


def _layer_norm_rows(y, g, b):
    mu = jnp.mean(y, axis=-1, keepdims=True)
    yc = y - mu
    var = jnp.mean(yc * yc, axis=-1, keepdims=True)
    return yc * lax.rsqrt(var + LN_EPS) * g + b


def _ffn_kernel(x_ref, xb_ref, wa_ref, wb_ref, wd_ref, g_ref, b_ref, o_ref, ob_ref, acc_ref):
    j = pl.program_id(1)

    @pl.when(j == 0)
    def _():
        acc_ref[...] = jnp.zeros_like(acc_ref)

    xb = xb_ref[...]
    a = jnp.dot(xb, wa_ref[...], preferred_element_type=F32)
    b = jnp.dot(xb, wb_ref[...], preferred_element_type=F32)
    h = (a * jax.nn.sigmoid(a) * b).astype(BF16)
    acc_ref[...] += jnp.dot(h, wd_ref[...], preferred_element_type=F32)

    @pl.when(j == pl.num_programs(1) - 1)
    def _():
        y = DN_ALPHA * x_ref[...] + 0.5 * acc_ref[...]
        o = _layer_norm_rows(y, g_ref[...], b_ref[...])
        o_ref[...] = o
        ob_ref[...] = o.astype(BF16)


def _ffn_ln(x, xb, w_up, w_down, ln_g, ln_b, layer, which, ln_idx, tm, tf):
    m = x.shape[0]
    nf = D_FF // tf
    return pl.pallas_call(
        _ffn_kernel,
        grid=(m // tm, nf),
        in_specs=[
            pl.BlockSpec((tm, D_MODEL), lambda i, j: (i, 0)),
            pl.BlockSpec((tm, D_MODEL), lambda i, j: (i, 0)),
            pl.BlockSpec((None, None, D_MODEL, tf), lambda i, j: (layer, which, 0, j)),
            pl.BlockSpec((None, None, D_MODEL, tf), lambda i, j: (layer, which, 0, j + nf)),
            pl.BlockSpec((None, None, tf, D_MODEL), lambda i, j: (layer, which, j, 0)),
            pl.BlockSpec((None, None, 1, D_MODEL), lambda i, j: (layer, ln_idx, 0, 0)),
            pl.BlockSpec((None, None, 1, D_MODEL), lambda i, j: (layer, ln_idx, 0, 0)),
        ],
        out_specs=[
            pl.BlockSpec((tm, D_MODEL), lambda i, j: (i, 0)),
            pl.BlockSpec((tm, D_MODEL), lambda i, j: (i, 0)),
        ],
        out_shape=[jax.ShapeDtypeStruct((m, D_MODEL), F32),
                   jax.ShapeDtypeStruct((m, D_MODEL), BF16)],
        scratch_shapes=[pltpu.VMEM((tm, D_MODEL), F32)],
        compiler_params=_cparams(("parallel", "arbitrary")),
        name="ffn_ln",
    )(x, xb, w_up, w_up, w_down, ln_g, ln_b)


def _proj_kernel(xb_ref, w_ref, wif_ref, *rest):
    o_ref, oif_ref, kv_ref = rest[-3:]
    j = pl.program_id(1)
    o = jnp.dot(xb_ref[...], w_ref[...], preferred_element_type=F32)
    o_ref[...] = o

    @pl.when(j == 0)
    def _():
        oif_ref[...] = jnp.dot(xb_ref[...], wif_ref[...], preferred_element_type=F32)

    @pl.when((j == KV_COL_BLOCK) | (j == KV_COL_BLOCK + 1))
    def _():
        kv_ref[...] = o


def _in_proj(xb, w_cat, w_if, kv_buf, layer, tm):
    m = xb.shape[0]
    in_specs = [
        pl.BlockSpec((tm, D_MODEL), lambda i, j: (i, 0)),
        pl.BlockSpec((None, D_MODEL, PROJ_TN), lambda i, j: (layer, 0, j)),
        pl.BlockSpec((None, D_MODEL, N_IF_PAD), lambda i, j: (layer, 0, 0)),
    ]
    args = [xb, w_cat, w_if]
    aliases = {}
    if kv_buf is not None:
        in_specs.append(pl.BlockSpec(memory_space=pl.ANY))
        args.append(kv_buf)
        aliases = {3: 2}
    return pl.pallas_call(
        _proj_kernel,
        grid=(m // tm, N_PROJ // PROJ_TN),
        in_specs=in_specs,
        out_specs=[pl.BlockSpec((tm, PROJ_TN), lambda i, j: (i, j)),
                   pl.BlockSpec((tm, N_IF_PAD), lambda i, j: (i, 0)),
                   pl.BlockSpec((None, None, tm, ATT_W),
                                lambda i, j: (jnp.clip(j - KV_COL_BLOCK, 0, 1), layer, i, 0))],
        out_shape=[jax.ShapeDtypeStruct((m, N_PROJ), F32),
                   jax.ShapeDtypeStruct((m, N_IF_PAD), F32),
                   jax.ShapeDtypeStruct((2, DEPTH, m, ATT_W), F32)],
        input_output_aliases=aliases,
        compiler_params=_cparams(("parallel", "arbitrary")),
        name="in_proj",
    )(*args)


def _t5_bucket_np(dist):
    n = np.maximum(dist, 0)
    max_exact = N_BUCKETS // 2
    nf = np.maximum(n, 1).astype(np.float32)
    large = max_exact + (np.log(nf / np.float32(max_exact)) / np.float32(math.log(MAX_DISTANCE / max_exact))
                         * np.float32(N_BUCKETS - max_exact)).astype(np.int32)
    large = np.minimum(large, N_BUCKETS - 1)
    return np.where(n < max_exact, n, large).astype(np.int32)


def _lambda_scalar(lq, lam_init):
    s01 = jnp.sum(lq[0:1, :] * lq[1:2, :], axis=1, keepdims=True)
    s23 = jnp.sum(lq[2:3, :] * lq[3:4, :], axis=1, keepdims=True)
    return jnp.exp(s01) - jnp.exp(s23) + lam_init


def _sub_norm(o, w_row, lam_init):
    o = o * lax.rsqrt(jnp.mean(o * o, axis=-1, keepdims=True) + LN_EPS)
    return o * w_row * (1.0 - lam_init)


def _pattn_kernel(qi_tab, ki_tab, q_ref, k_ref, v_ref, bias_ref, lq_ref, sw_ref, o_ref,
                  qt_sc, m_sc, acc_sc, *, tq, lam_init):
    t = pl.program_id(2)
    qi = qi_tab[t]
    ki = ki_tab[t]
    tk = k_ref.shape[0]

    @pl.when(ki == 0)
    def _():
        q = q_ref[...] * (DA_HEAD_DIM ** -0.5 * LOG2E)
        lane = lax.broadcasted_iota(jnp.int32, q.shape, 1)
        qt_sc[:, :tq] = jnp.where(lane < DA_HEAD_DIM, q, 0.0).T.astype(BF16)
        qt_sc[:, tq:] = jnp.where(lane >= DA_HEAD_DIM, q, 0.0).T.astype(BF16)
        m_sc[...] = jnp.full_like(m_sc, -jnp.inf)
        acc_sc[...] = jnp.zeros_like(acc_sc)

    def accumulate(with_bias, diagonal):
        kb = k_ref[...].astype(BF16)
        vt = jnp.concatenate([v_ref[...].T, jnp.ones((ACC_PAD, tk), F32)], axis=0).astype(BF16)
        n_chunks = 2 * tq // QCOLS
        cols = [slice(c * QCOLS, (c + 1) * QCOLS) for c in range(n_chunks)]
        keys = [((c * QCOLS) % tq + QCOLS) if diagonal else tk for c in range(n_chunks)]

        def scores(c):
            s = jnp.dot(kb[:keys[c]], qt_sc[:, cols[c]], preferred_element_type=F32)
            if with_bias:
                b0 = (c * QCOLS) % tq
                s = s + bias_ref[:keys[c], b0:b0 + QCOLS]
            return s

        def softmax(c, s):
            m_old = m_sc[:, cols[c]]
            m_new = jnp.maximum(m_old, jnp.max(s, axis=0, keepdims=True))
            m_sc[:, cols[c]] = m_new
            return jnp.exp2(s - m_new).astype(BF16), jnp.exp2(m_old - m_new)

        def update(c, p, alpha):
            acc_sc[:, cols[c]] = alpha * acc_sc[:, cols[c]] + jnp.dot(vt[:, :keys[c]], p,
                                                                      preferred_element_type=F32)

        s_q = {0: scores(0), 1: scores(1)}
        for c in range(n_chunks):
            p, alpha = softmax(c, s_q.pop(c))
            if c + 2 < n_chunks:
                s_q[c + 2] = scores(c + 2)
            update(c, p, alpha)

    @pl.when(qi == ki)
    def _():
        accumulate(True, True)

    @pl.when(qi - ki == 1)
    def _():
        accumulate(True, False)

    @pl.when(qi - ki > 1)
    def _():
        accumulate(False, False)

    @pl.when(ki == qi)
    def _():
        acc = acc_sc[...]
        o_all = acc[:DA_V_DIM] * (1.0 / acc[DA_V_DIM:DA_V_DIM + 1])
        lam = _lambda_scalar(lq_ref[...], lam_init)
        o = o_all[:, :tq] - lam * o_all[:, tq:]
        o = o * lax.rsqrt(jnp.mean(o * o, axis=0, keepdims=True) + LN_EPS)
        o = o * (sw_ref[...] * (1.0 - lam_init))
        o_ref[...] = o.T.astype(o_ref.dtype)


def _prompt_attn(proj, bias_tiles, lam_qk, subln_col, layer, lam_init, batch, seq, tq):
    nq = seq // tq
    tri = [(qi, ki) for qi in range(nq) for ki in range(qi + 1)]
    qi_tab = jnp.asarray([a for a, _ in tri], jnp.int32)
    ki_tab = jnp.asarray([b for _, b in tri], jnp.int32)
    hq, hk, hv = 0, ATT_W // DA_V_DIM, 2 * ATT_W // DA_V_DIM

    def bias_idx(b, h, t, qt, kt):
        return (h, jnp.minimum(qt[t] - kt[t], 1), 0, 0)

    grid_spec = pltpu.PrefetchScalarGridSpec(
        num_scalar_prefetch=2,
        grid=(batch, DA_HEADS, len(tri)),
        in_specs=[
            pl.BlockSpec((tq, DA_V_DIM), lambda b, h, t, qt, kt: (b * nq + qt[t], hq + h)),
            pl.BlockSpec((tq, DA_V_DIM), lambda b, h, t, qt, kt: (b * nq + kt[t], hk + h)),
            pl.BlockSpec((tq, DA_V_DIM), lambda b, h, t, qt, kt: (b * nq + kt[t], hv + h)),
            pl.BlockSpec((None, None, tq, tq), bias_idx),
            pl.BlockSpec((None, 4, DA_HEAD_DIM), lambda b, h, t, qt, kt: (layer, 0, 0)),
            pl.BlockSpec((None, DA_V_DIM, 1), lambda b, h, t, qt, kt: (layer, 0, 0)),
        ],
        out_specs=pl.BlockSpec((tq, DA_V_DIM), lambda b, h, t, qt, kt: (b * nq + qt[t], h)),
        scratch_shapes=[pltpu.VMEM((DA_V_DIM, 2 * tq), BF16), pltpu.VMEM((1, 2 * tq), F32),
                        pltpu.VMEM((DA_V_DIM + ACC_PAD, 2 * tq), F32)],
    )
    return pl.pallas_call(
        functools.partial(_pattn_kernel, tq=tq, lam_init=lam_init),
        grid_spec=grid_spec,
        out_shape=jax.ShapeDtypeStruct((batch * seq, ATT_W), BF16),
        compiler_params=_cparams(("parallel", "parallel", "arbitrary")),
        name="prompt_attn",
    )(qi_tab, ki_tab, proj, proj, proj, bias_tiles, lam_qk, subln_col)


def _skew_kernel(y_ref, o_ref, *, tq, rows):
    for r0 in range(0, tq, rows):
        x = jnp.broadcast_to(y_ref[...], (rows, 2 * tq))
        o_ref[r0:r0 + rows, :] = pltpu.roll(x, r0, 1, stride=1, stride_axis=0)[:, :tq]


def _prompt_bias_tiles(rel_bias, tq):
    assert tq >= MAX_DISTANCE and np.all(_t5_bucket_np(np.arange(MAX_DISTANCE, 1 << 16)) == N_BUCKETS - 1)
    span = 2 * tq
    d_row = np.zeros(span, np.int64)
    d_row[:tq] = np.arange(tq)
    d_row[span - np.arange(1, tq)] = -np.arange(1, tq)
    far = rel_bias[N_BUCKETS - 1]
    rows = []
    for delta in range(2):
        dist = d_row + delta * tq
        y = (rel_bias[_t5_bucket_np(dist)] - far) * LOG2E
        rows.append(jnp.where(jnp.asarray(dist >= 0)[:, None], y, NEG_BIG).T)
    y = jnp.stack(rows, axis=1).reshape(DA_HEADS, 2, 1, span)
    return pl.pallas_call(
        functools.partial(_skew_kernel, tq=tq, rows=256),
        grid=(DA_HEADS, 2),
        in_specs=[pl.BlockSpec((None, None, 1, span), lambda h, d: (h, d, 0, 0))],
        out_specs=pl.BlockSpec((None, None, tq, tq), lambda h, d: (h, d, 0, 0)),
        out_shape=jax.ShapeDtypeStruct((DA_HEADS, 2, tq, tq), F32),
        compiler_params=_cparams(("parallel", "parallel")),
        name="bias_skew",
    )(y)


def _page_rows(page_ref):
    return jnp.concatenate([page_ref[pl.ds(h, PAGE_SIZE, stride=DA_HEADS), :] for h in range(DA_HEADS)],
                           axis=1).astype(BF16)


def _sattn_kernel(pt_ref,w_ref, kn_ref, vn_ref, bl_ref, bn_ref, c_ref, lq_ref, sw_ref, *rest,
                  n_group, n_pages, n_tok, lam_init):
    k_refs = rest[:n_group]
    v_refs = rest[n_group:2 * n_group]
    o_ref = rest[2 * n_group]
    s_sc, acc_sc = rest[2 * n_group + 1:]
    phase = pl.program_id(1)
    step = pl.program_id(2)
    n_steps = pl.num_programs(2)
    past = n_pages * PAGE_SIZE
    half = DA_HEADS * n_tok

    @pl.when(phase == 0)
    def _():
        w = w_ref[...]

        @pl.when(step == 0)
        def _():
            kn = kn_ref[...]
            kn = jnp.concatenate([kn, jnp.zeros_like(kn)], axis=0).astype(BF16)
            sn = jnp.dot(kn, w, preferred_element_type=F32)[:n_tok]
            s_sc[pl.ds(past, n_tok), :] = sn + bn_ref[...]

        for g in range(n_group):
            page = step * n_group + g
            s = jnp.dot(_page_rows(k_refs[g]), w, preferred_element_type=F32)
            bias = jnp.where(page == n_pages - 1, bl_ref[...], c_ref[...])
            s_sc[pl.ds(pl.multiple_of(page * PAGE_SIZE, PAGE_SIZE), PAGE_SIZE), :] = s + bias

    @pl.when(phase == 1)
    def _():
        @pl.when(step == 0)
        def _():
            chunk = 512
            n_chunks = past // chunk

            def max_body(i, m):
                blk = s_sc[pl.ds(pl.multiple_of(i * chunk, chunk), chunk), :]
                return jnp.maximum(m, jnp.max(blk, axis=0, keepdims=True))

            s_new = s_sc[pl.ds(past, n_tok), :]
            m = lax.fori_loop(0, n_chunks, max_body, jnp.max(s_new, axis=0, keepdims=True))

            def sum_body(i, l):
                sl = pl.ds(pl.multiple_of(i * chunk, chunk), chunk)
                p = jnp.exp(s_sc[sl, :] - m)
                s_sc[sl, :] = p
                return l + jnp.sum(p, axis=0, keepdims=True)

            p_new = jnp.exp(s_new - m)
            l = lax.fori_loop(0, n_chunks, sum_body, jnp.sum(p_new, axis=0, keepdims=True))
            inv_l = 1.0 / l
            lam = _lambda_scalar(lq_ref[...], lam_init)

            def comb(p):
                pn = p * inv_l
                return pn - lam * pltpu.roll(pn, half, 1)

            def comb_body(i, c):
                sl = pl.ds(pl.multiple_of(i * chunk, chunk), chunk)
                s_sc[sl, :] = comb(s_sc[sl, :])
                return c

            lax.fori_loop(0, n_chunks, comb_body, 0)
            a_new_t = comb(p_new).T.astype(BF16).astype(F32)
            vn = vn_ref[...].astype(BF16).astype(F32)
            acc0 = a_new_t[:, 0:1] * vn[0:1, :]
            for t in range(1, n_tok):
                acc0 = acc0 + a_new_t[:, t:t + 1] * vn[t:t + 1, :]
            acc_sc[...] = acc0

        tot = jnp.zeros(acc_sc.shape, F32)
        for g in range(n_group):
            page = step * n_group + g
            a = s_sc[pl.ds(pl.multiple_of(page * PAGE_SIZE, PAGE_SIZE), PAGE_SIZE), :]
            at = a.T.astype(BF16)
            tot = tot + jnp.dot(at, _page_rows(v_refs[g]), preferred_element_type=F32)
        acc_sc[...] += tot

        @pl.when(step == n_steps - 1)
        def _():
            sw = sw_ref[...]
            for h in range(DA_HEADS):
                o = acc_sc[h * n_tok:(h + 1) * n_tok, h * DA_V_DIM:(h + 1) * DA_V_DIM]
                o_ref[:, h * DA_V_DIM:(h + 1) * DA_V_DIM] = _sub_norm(o, sw, lam_init).astype(o_ref.dtype)


def _sample_attn(proj, cache_k, cache_v, page_table, rel_bias, lam_qk, subln_w, layer, lam_init,
                 dec_batch, n_tok, n_group):
    n_pages = page_table.shape[1]
    assert n_pages % n_group == 0 and n_tok == 8
    past = n_pages * PAGE_SIZE
    n_steps = n_pages // n_group
    half = DA_HEADS * n_tok
    width = DA_HEADS * DA_V_DIM

    q = proj[:, :ATT_W].reshape(dec_batch, n_tok, DA_HEADS, 2, DA_HEAD_DIM) * (DA_HEAD_DIM ** -0.5)
    w = jnp.einsum('bthjd,hH,jJ->bhjdJHt', q, jnp.eye(DA_HEADS, dtype=F32), jnp.eye(2, dtype=F32))
    w = w.reshape(dec_batch, width, 2 * half).astype(BF16)

    tok = np.arange(n_tok)

    def col_bias(dist):
        b = rel_bias[_t5_bucket_np(dist)]
        b = jnp.where(jnp.asarray(dist >= 0)[..., None], b, NEG_BIG)
        b = jnp.transpose(b, (0, 2, 1)).reshape(dist.shape[0], half)
        return jnp.concatenate([b, b], axis=1)

    assert np.all(_t5_bucket_np(np.arange(PAGE_SIZE + 1, past + n_tok + 1)) == N_BUCKETS - 1)
    bias_last = col_bias((past + tok)[None, :] - (past - PAGE_SIZE + np.arange(PAGE_SIZE))[:, None])
    bias_new = col_bias(tok[None, :] - tok[:, None])
    bias_far = col_bias(np.full((1, n_tok), PAGE_SIZE + 1))

    page_block = (None, None, PAGE_SIZE * DA_HEADS, DA_V_DIM)
    cache_k = cache_k.reshape(cache_k.shape[0], cache_k.shape[1], PAGE_SIZE * DA_HEADS, DA_V_DIM)
    cache_v = cache_v.reshape(cache_v.shape[0], cache_v.shape[1], PAGE_SIZE * DA_HEADS, DA_V_DIM)
    kcol, vcol = ATT_W // width, 2 * ATT_W // width

    def k_idx(g):
        def idx(b, ph, s, pt):
            s_eff = jnp.where(ph == 0, s, n_steps - 1)
            return (layer, pt[b, s_eff * n_group + g], 0, 0)
        return idx

    def v_idx(g):
        def idx(b, ph, s, pt):
            s_eff = jnp.where(ph == 0, 0, s)
            return (layer, pt[b, s_eff * n_group + g], 0, 0)
        return idx

    const2 = lambda b, ph, s, pt: (0, 0)
    grid_spec = pltpu.PrefetchScalarGridSpec(
        num_scalar_prefetch=1,
        grid=(dec_batch, 2, n_steps),
        in_specs=[
            pl.BlockSpec((None, width, 2 * half), lambda b, ph, s, pt: (b, 0, 0)),
            pl.BlockSpec((n_tok, width), lambda b, ph, s, pt: (b, kcol)),
            pl.BlockSpec((n_tok, width), lambda b, ph, s, pt: (b, vcol)),
            pl.BlockSpec((PAGE_SIZE, 2 * half), const2),
            pl.BlockSpec((n_tok, 2 * half), const2),
            pl.BlockSpec((1, 2 * half), const2),
            pl.BlockSpec((None, 4, DA_HEAD_DIM), lambda b, ph, s, pt: (layer, 0, 0)),
            pl.BlockSpec((None, 1, DA_V_DIM), lambda b, ph, s, pt: (layer, 0, 0)),
        ] + [pl.BlockSpec(page_block,k_idx(g)) for g in range(n_group)]
          + [pl.BlockSpec(page_block,v_idx(g)) for g in range(n_group)],
        out_specs=pl.BlockSpec((n_tok, width), lambda b, ph, s, pt: (b, 0)),
        scratch_shapes=[pltpu.VMEM((past + n_tok, 2 * half), F32),
                        pltpu.VMEM((2 * half, width), F32)],
    )
    return pl.pallas_call(
        functools.partial(_sattn_kernel, n_group=n_group, n_pages=n_pages, n_tok=n_tok,
                          lam_init=lam_init),
        grid_spec=grid_spec,
        out_shape=jax.ShapeDtypeStruct((dec_batch * n_tok, width), BF16),
        compiler_params=_cparams(("parallel", "arbitrary", "arbitrary")),
        name="sample_attn",
    )(page_table, w, proj, proj, bias_last, bias_new, bias_far, lam_qk, subln_w,
      *([cache_k] * n_group), *([cache_v] * n_group))


def _mlstm_kernel(q_ref, k_ref, v_ref, o_gate_ref, g_ref, nw_ref, c0_ref, n0_ref, m0_ref,
                  h_ref, c_ref, n_ref, m_ref, *, rows, chunk):
    c_idx = pl.program_id(1)

    @pl.when(c_idx == 0)
    def _():
        c_ref[...] = c0_ref[...]
        n_ref[...] = n0_ref[...]
        m_ref[...] = m0_ref[...]

    row = lax.broadcasted_iota(jnp.int32, (chunk, chunk), 0)
    col = lax.broadcasted_iota(jnp.int32, (chunk, chunk), 1)
    tril = row >= col
    eye = row == col

    def padded(x):
        if rows == chunk:
            return x
        return jnp.concatenate([x, jnp.zeros((chunk - rows, x.shape[1]), x.dtype)], axis=0)

    for h in range(ML_HEADS):
        sl = slice(h * ML_DQK, (h + 1) * ML_DQK)
        q = padded(q_ref[:, sl])
        k = padded(k_ref[:, sl]) * (ML_DQK ** -0.5)
        v = padded(v_ref[:, sl])
        ig = g_ref[h:h + 1, :]
        gf = g_ref[ML_HEADS + h:ML_HEADS + h + 1, :]
        lf = jnp.minimum(gf, 0.0) - jnp.log1p(jnp.exp(-jnp.abs(gf)))
        c_state = c_ref[h]
        n_state = n_ref[h]
        m_state = m_ref[:, h:h + 1]

        b_col = jnp.sum(jnp.where(tril, lf, 0.0), axis=1, keepdims=True)
        b_row = jnp.sum(jnp.where(eye, b_col, 0.0), axis=0, keepdims=True)
        logw = jnp.where(tril, b_col - b_row + ig, NEG_BIG)
        m_inter = b_col + m_state
        m_t = jnp.maximum(jnp.max(logw, axis=1, keepdims=True), m_inter)
        inter = jnp.exp(m_inter - m_t)
        qb = q.astype(BF16)
        kb = k.astype(BF16)
        s = lax.dot_general(qb, kb, (((1,), (1,)), ((), ())), preferred_element_type=F32)
        s = s * jnp.exp(logw - m_t)
        num = jnp.dot(s.astype(BF16), v.astype(BF16), preferred_element_type=F32)
        num = num + inter * lax.dot_general(qb, c_state.astype(BF16), (((1,), (1,)), ((), ())),
                                            preferred_element_type=F32)
        den = jnp.sum(s, axis=1, keepdims=True) + inter * jnp.sum(q * n_state, axis=1, keepdims=True)
        hh = num / jnp.maximum(jnp.abs(den), jnp.exp(-m_t))

        b_last = jnp.sum(lf, axis=1, keepdims=True)
        logw_end = b_last - b_row + ig
        m_new = jnp.maximum(b_last + m_state, jnp.max(logw_end, axis=1, keepdims=True))
        w_end = jnp.exp(logw_end - m_new)
        decay = jnp.exp(b_last + m_state - m_new)
        w_col = jnp.sum(jnp.where(eye, w_end, 0.0), axis=1, keepdims=True)
        vw = (v * w_col).astype(BF16)
        c_ref[h] = decay * c_state + lax.dot_general(vw, kb, (((0,), (0,)), ((), ())),
                                                     preferred_element_type=F32)
        n_ref[h] = decay * n_state + jnp.sum(k * w_col, axis=0, keepdims=True)
        m_ref[:, h:h + 1] = m_new

        hh = hh[:rows]
        mc = hh - jnp.mean(hh, axis=1, keepdims=True)
        y = mc * lax.rsqrt(jnp.mean(mc * mc, axis=1, keepdims=True) + LN_EPS) * nw_ref[:, sl]
        h_ref[:, sl] = (jax.nn.sigmoid(o_gate_ref[:, sl]) * y).astype(h_ref.dtype)


def _mlstm(proj, gates_t, norm_w, c0, n0, m0, layer, batch, seq, rows_per_block):
    nc = seq // rows_per_block
    chunk = gates_t.shape[2] // nc
    cq, ck, cv, co = (3 * ATT_W // ML_W, 3 * ATT_W // ML_W + 1, 3 * ATT_W // ML_W + 2,
                      3 * ATT_W // ML_W + 3)
    rows = lambda col: pl.BlockSpec((rows_per_block, ML_W), lambda b, c: (b * nc + c, col))
    state_c = pl.BlockSpec((None, ML_HEADS, ML_DV, ML_DQK), lambda b, c: (b, 0, 0, 0))
    state_n = pl.BlockSpec((None, ML_HEADS, 1, ML_DQK), lambda b, c: (b, 0, 0, 0))
    state_m = pl.BlockSpec((None, 1, ML_HEADS), lambda b, c: (b, 0, 0))
    return pl.pallas_call(
        functools.partial(_mlstm_kernel, rows=rows_per_block, chunk=chunk),
        grid=(batch, nc),
        in_specs=[rows(cq), rows(ck), rows(cv), rows(co),
                  pl.BlockSpec((None, 2 * ML_HEADS, chunk), lambda b, c: (b, 0, c)),
                  pl.BlockSpec((None, 1, ML_W), lambda b, c: (layer, 0, 0)),
                  state_c, state_n, state_m],
        out_specs=[pl.BlockSpec((rows_per_block, ML_W), lambda b, c: (b * nc + c, 0)),
                   state_c, state_n, state_m],
        out_shape=[jax.ShapeDtypeStruct((batch * seq, ML_W), BF16),
                   jax.ShapeDtypeStruct((batch, ML_HEADS, ML_DV, ML_DQK), F32),
                   jax.ShapeDtypeStruct((batch, ML_HEADS, 1, ML_DQK), F32),
                   jax.ShapeDtypeStruct((batch, 1, ML_HEADS), F32)],
        compiler_params=_cparams(("parallel", "arbitrary")),
        name="mlstm",
    )(proj, proj, proj, proj, gates_t, norm_w, c0, n0, m0)


def _merge_kernel(ao_ref, mh_ref, wa_ref, wm_ref, ga_ref, gm_ref, ba_ref, bm_ref, o_ref):
    ta = jnp.dot(ao_ref[...], wa_ref[...], preferred_element_type=F32)
    tmm = jnp.dot(mh_ref[...], wm_ref[...], preferred_element_type=F32)
    ga = jax.nn.sigmoid(ga_ref[...] + ba_ref[...])
    gm = jax.nn.sigmoid(gm_ref[...] + bm_ref[...])
    o_ref[...] = (ga * ta + gm * tmm).astype(o_ref.dtype)


def _merge(ao, mh, proj, w_ba, w_bm, b_gate, layer, tm, tn):
    m = ao.shape[0]
    nd = D_MODEL // tn
    ga0 = COL_GATE // tn
    return pl.pallas_call(
        _merge_kernel,
        grid=(m // tm, nd),
        in_specs=[
            pl.BlockSpec((tm, ATT_W), lambda i, j: (i, 0)),
            pl.BlockSpec((tm, ML_W), lambda i, j: (i, 0)),
            pl.BlockSpec((None, ATT_W, tn), lambda i, j: (layer, 0, j)),
            pl.BlockSpec((None, ML_W, tn), lambda i, j: (layer, 0, j)),
            pl.BlockSpec((tm, tn), lambda i, j: (i, ga0 + j)),
            pl.BlockSpec((tm, tn), lambda i, j: (i, ga0 + nd + j)),
            pl.BlockSpec((None, 1, tn), lambda i, j: (layer, 0, j)),
            pl.BlockSpec((None, 1, tn), lambda i, j: (layer, 0, nd + j)),
        ],
        out_specs=pl.BlockSpec((tm, tn), lambda i, j: (i, j)),
        out_shape=jax.ShapeDtypeStruct((m, D_MODEL), BF16),
        compiler_params=_cparams(("parallel", "parallel")),
        name="merge",
    )(ao, mh, w_ba, w_bm, proj, proj, b_gate, b_gate)


def _out_kernel(x_ref, mg_ref, w_ref, g_ref, b_ref, o_ref, ob_ref):
    y = DN_ALPHA * x_ref[...] + jnp.dot(mg_ref[...], w_ref[...], preferred_element_type=F32)
    o = _layer_norm_rows(y, g_ref[...], b_ref[...])
    o_ref[...] = o
    ob_ref[...] = o.astype(BF16)


def _out_proj_ln(x, merged, w_out, ln_g, ln_b, layer, tm):
    m = x.shape[0]
    return pl.pallas_call(
        _out_kernel,
        grid=(m // tm,),
        in_specs=[
            pl.BlockSpec((tm, D_MODEL), lambda i: (i, 0)),
            pl.BlockSpec((tm, D_MODEL), lambda i: (i, 0)),
            pl.BlockSpec((None, D_MODEL, D_MODEL), lambda i: (layer, 0, 0)),
            pl.BlockSpec((None, None, 1, D_MODEL), lambda i: (layer, 1, 0, 0)),
            pl.BlockSpec((None, None, 1, D_MODEL), lambda i: (layer, 1, 0, 0)),
        ],
        out_specs=[pl.BlockSpec((tm, D_MODEL), lambda i: (i, 0)),
                   pl.BlockSpec((tm, D_MODEL), lambda i: (i, 0))],
        out_shape=[jax.ShapeDtypeStruct((m, D_MODEL), F32),
                   jax.ShapeDtypeStruct((m, D_MODEL), BF16)],
        compiler_params=_cparams(("parallel",)),
        name="out_proj_ln",
    )(x, merged, w_out, ln_g, ln_b)


def _gates_transposed(pif, b_if, batch, seq, pad_to):
    gif = pif[:, :N_IF] + b_if
    gt = jnp.transpose(gif.reshape(batch, seq, N_IF), (0, 2, 1))
    if pad_to > seq:
        pad = jnp.concatenate([jnp.full((batch, ML_HEADS, pad_to - seq), NEG_BIG, F32),
                               jnp.full((batch, ML_HEADS, pad_to - seq), 1e4, F32)], axis=1)
        gt = jnp.concatenate([gt, pad], axis=2)
    return gt


def kernel(x_prompt, x_sample, cache_k, cache_v, page_table, state_C, state_n, state_m, rel_bias,
           w_in, b_gate, b_if, lam_qk, subln_w, mlstm_norm_w, w_branch_attn, w_branch_mlstm,
           w_out, ffn_up, ffn_down, ln_g, ln_b):
    batch, seq, _ = x_prompt.shape
    dec_batch, dec_seq, _ = x_sample.shape
    mp, ms = batch * seq, dec_batch * dec_seq
    tm_p, tm_s = 512, ms
    tq = 1024
    prompt_chunk = 256

    w_cat = jnp.concatenate([w_in[:, :, :N_MAIN], w_in[:, :, N_MAIN + N_IF:]], axis=2).astype(BF16)
    w_if = jnp.pad(w_in[:, :, N_MAIN:N_MAIN + N_IF], ((0, 0), (0, 0), (0, N_IF_PAD - N_IF))).astype(BF16)
    up_b = ffn_up.astype(BF16)
    down_b = ffn_down.astype(BF16)
    wba_b = w_branch_attn.astype(BF16)
    wbm_b = w_branch_mlstm.astype(BF16)
    wout_b = w_out.astype(BF16)
    ln_g = ln_g.reshape(DEPTH, 3, 1, D_MODEL)
    ln_b = ln_b.reshape(DEPTH, 3, 1, D_MODEL)
    b_gate3 = b_gate.reshape(DEPTH, 1, N_GATE)
    subln3 = subln_w.reshape(DEPTH, 1, DA_V_DIM)
    subln_col = subln_w.reshape(DEPTH, DA_V_DIM, 1)
    normw3 = mlstm_norm_w.reshape(DEPTH, 1, ML_W)
    bias_tiles = _prompt_bias_tiles(rel_bias, tq)

    zero_c = jnp.zeros((batch, ML_HEADS, ML_DV, ML_DQK), F32)
    zero_n = jnp.zeros((batch, ML_HEADS, 1, ML_DQK), F32)
    zero_m = jnp.zeros((batch, 1, ML_HEADS), F32)

    xp = x_prompt.reshape(mp, D_MODEL)
    xs = x_sample.reshape(ms, D_MODEL)
    xpb = xp.astype(BF16)
    xsb = xs.astype(BF16)

    outs = {name: [] for name in ('cp', 'np', 'mp', 'cs', 'ns', 'ms')}
    kv_p = kv_s = None
    for l in range(DEPTH):
        lam_init = 0.8 - 0.6 * math.exp(-0.3 * l)

        def rowwise_pre(x, xb, kv, tm):
            x1, x1b = _ffn_ln(x, xb, up_b, down_b, ln_g, ln_b, l, 0, 0, tm, 512)
            proj, pif, kv = _in_proj(x1b, w_cat, w_if, kv, l, min(2 * tm, x1b.shape[0]))
            return x1, proj, pif, kv

        def rowwise_post(x1, proj, ao, mh, tm):
            merged = _merge(ao, mh, proj, wba_b, wbm_b, b_gate3, l, tm, 512)
            x2, x2b = _out_proj_ln(x1, merged, wout_b, ln_g, ln_b, l, min(tm, 256))
            return _ffn_ln(x2, x2b, up_b, down_b, ln_g, ln_b, l, 1, 2, tm, 512)

        x1, proj, pif, kv_p = rowwise_pre(xp, xpb, kv_p, tm_p)
        ao = _prompt_attn(proj, bias_tiles, lam_qk, subln_col, l, lam_init, batch, seq, tq)
        gt = _gates_transposed(pif, b_if[l], batch, seq, seq)
        mh, c_new, n_new, m_new = _mlstm(proj, gt, normw3, zero_c, zero_n, zero_m, l, batch, seq,
                                         prompt_chunk)
        xp, xpb = rowwise_post(x1, proj, ao, mh, tm_p)
        outs['cp'].append(c_new)
        outs['np'].append(n_new.reshape(batch, ML_HEADS, ML_DQK))
        outs['mp'].append(m_new.reshape(batch, ML_HEADS))

        x1, proj, pif, kv_s = rowwise_pre(xs, xsb, kv_s, tm_s)
        ao = _sample_attn(proj, cache_k, cache_v, page_table, rel_bias, lam_qk, subln3, l, lam_init,
                          dec_batch, dec_seq, 8)
        gt = _gates_transposed(pif, b_if[l], dec_batch, dec_seq, 128)
        mh, c_new, n_new, m_new = _mlstm(proj, gt, normw3, state_C[l],
                                         state_n[l].reshape(dec_batch, ML_HEADS, 1, ML_DQK),
                                         state_m[l].reshape(dec_batch, 1, ML_HEADS), l, dec_batch,
                                         dec_seq, dec_seq)
        xs, xsb = rowwise_post(x1, proj, ao, mh, tm_s)
        outs['cs'].append(c_new)
        outs['ns'].append(n_new.reshape(dec_batch, ML_HEADS, ML_DQK))
        outs['ms'].append(m_new.reshape(dec_batch, ML_HEADS))

    st = lambda name: jnp.stack(outs[name])
    kv_shape_p = (DEPTH, batch, seq, DA_HEADS, DA_V_DIM)
    kv_shape_s = (DEPTH, dec_batch, dec_seq, DA_HEADS, DA_V_DIM)
    return (xp.reshape(batch, seq, D_MODEL), xs.reshape(dec_batch, dec_seq, D_MODEL),
            kv_p[0].reshape(kv_shape_p), kv_p[1].reshape(kv_shape_p),
            kv_s[0].reshape(kv_shape_s), kv_s[1].reshape(kv_shape_s), st('cp'), st('np'), st('mp'),
            st('cs'), st('ns'), st('ms'))
```

```python
import functools
import math

import numpy as np
import jax
import jax.numpy as jnp
from jax import lax
from jax.experimental import pallas as pl
from jax.experimental.pallas import tpu as pltpu

F32 = jnp.float32
BF16 = jnp.bfloat16

D_MODEL = 2048
DEPTH = 4
PAGE_SIZE = 128
DA_HEADS = 8
DA_HEAD_DIM = 64
DA_V_DIM = 2 * DA_HEAD_DIM
ML_HEADS = 4
ML_DQK = 256
ML_DV = 256
D_FF = 5632
N_BUCKETS = 32
MAX_DISTANCE = 128
LN_EPS = 1e-5
DN_ALPHA = (2.0 * DEPTH) ** 0.25

ATT_W = DA_HEADS * DA_V_DIM
ML_W = ML_HEADS * ML_DV
N_MAIN = 3 * ATT_W + 4 * ML_W
N_GATE = 2 * D_MODEL
N_IF = 2 * ML_HEADS
N_IF_PAD = 256
N_PROJ = N_MAIN + N_GATE
PROJ_TN = 1024
KV_COL_BLOCK = ATT_W // PROJ_TN
COL_GATE = N_MAIN

NEG_BIG = -1e30
LOG2E = math.log2(math.e)
QCOLS = 512
ACC_PAD = 16
VMEM_LIMIT = 48 * 1024 * 1024


def _cparams(sem):
    return pltpu.CompilerParams(dimension_semantics=sem, vmem_limit_bytes=VMEM_LIMIT)


def _layer_norm_rows(y, g, b):
    mu = jnp.mean(y, axis=-1, keepdims=True)
    yc = y - mu
    var = jnp.mean(yc * yc, axis=-1, keepdims=True)
    return yc * lax.rsqrt(var + LN_EPS) * g + b


def _ffn_kernel(x_ref, xb_ref, wa_ref, wb_ref, wd_ref, g_ref, b_ref, o_ref, ob_ref, acc_ref):
    j = pl.program_id(1)

    @pl.when(j == 0)
    def _():
        acc_ref[...] = jnp.zeros_like(acc_ref)

    xb = xb_ref[...]
    a = jnp.dot(xb, wa_ref[...], preferred_element_type=F32)
    b = jnp.dot(xb, wb_ref[...], preferred_element_type=F32)
    h = (a * jax.nn.sigmoid(a) * b).astype(BF16)
    acc_ref[...] += jnp.dot(h, wd_ref[...], preferred_element_type=F32)

    @pl.when(j == pl.num_programs(1) - 1)
    def _():
        y = DN_ALPHA * x_ref[...] + 0.5 * acc_ref[...]
        o = _layer_norm_rows(y, g_ref[...], b_ref[...])
        o_ref[...] = o
        ob_ref[...] = o.astype(BF16)


def _ffn_ln(x, xb, w_up, w_down, ln_g, ln_b, layer, which, ln_idx, tm, tf):
    m = x.shape[0]
    nf = D_FF // tf
    return pl.pallas_call(
        _ffn_kernel,
        grid=(m // tm, nf),
        in_specs=[
            pl.BlockSpec((tm, D_MODEL), lambda i, j: (i, 0)),
            pl.BlockSpec((tm, D_MODEL), lambda i, j: (i, 0)),
            pl.BlockSpec((None, None, D_MODEL, tf), lambda i, j: (layer, which, 0, j)),
            pl.BlockSpec((None, None, D_MODEL, tf), lambda i, j: (layer, which, 0, j + nf)),
            pl.BlockSpec((None, None, tf, D_MODEL), lambda i, j: (layer, which, j, 0)),
            pl.BlockSpec((None, None, 1, D_MODEL), lambda i, j: (layer, ln_idx, 0, 0)),
            pl.BlockSpec((None, None, 1, D_MODEL), lambda i, j: (layer, ln_idx, 0, 0)),
        ],
        out_specs=[
            pl.BlockSpec((tm, D_MODEL), lambda i, j: (i, 0)),
            pl.BlockSpec((tm, D_MODEL), lambda i, j: (i, 0)),
        ],
        out_shape=[jax.ShapeDtypeStruct((m, D_MODEL), F32),
                   jax.ShapeDtypeStruct((m, D_MODEL), BF16)],
        scratch_shapes=[pltpu.VMEM((tm, D_MODEL), F32)],
        compiler_params=_cparams(("parallel", "arbitrary")),
        name="ffn_ln",
    )(x, xb, w_up, w_up, w_down, ln_g, ln_b)


def _proj_kernel(xb_ref, w_ref, wif_ref, *rest):
    o_ref, oif_ref, kv_ref = rest[-3:]
    j = pl.program_id(1)
    o = jnp.dot(xb_ref[...], w_ref[...], preferred_element_type=F32)
    o_ref[...] = o

    @pl.when(j == 0)
    def _():
        oif_ref[...] = jnp.dot(xb_ref[...], wif_ref[...], preferred_element_type=F32)

    @pl.when((j == KV_COL_BLOCK) | (j == KV_COL_BLOCK + 1))
    def _():
        kv_ref[...] = o


def _in_proj(xb, w_cat, w_if, kv_buf, layer, tm):
    m = xb.shape[0]
    in_specs = [
        pl.BlockSpec((tm, D_MODEL), lambda i, j: (i, 0)),
        pl.BlockSpec((None, D_MODEL, PROJ_TN), lambda i, j: (layer, 0, j)),
        pl.BlockSpec((None, D_MODEL, N_IF_PAD), lambda i, j: (layer, 0, 0)),
    ]
    args = [xb, w_cat, w_if]
    aliases = {}
    if kv_buf is not None:
        in_specs.append(pl.BlockSpec(memory_space=pl.ANY))
        args.append(kv_buf)
        aliases = {3: 2}
    return pl.pallas_call(
        _proj_kernel,
        grid=(m // tm, N_PROJ // PROJ_TN),
        in_specs=in_specs,
        out_specs=[pl.BlockSpec((tm, PROJ_TN), lambda i, j: (i, j)),
                   pl.BlockSpec((tm, N_IF_PAD), lambda i, j: (i, 0)),
                   pl.BlockSpec((None, None, tm, ATT_W),
                                lambda i, j: (jnp.clip(j - KV_COL_BLOCK, 0, 1), layer, i, 0))],
        out_shape=[jax.ShapeDtypeStruct((m, N_PROJ), F32),
                   jax.ShapeDtypeStruct((m, N_IF_PAD), F32),
                   jax.ShapeDtypeStruct((2, DEPTH, m, ATT_W), F32)],
        input_output_aliases=aliases,
        compiler_params=_cparams(("parallel", "arbitrary")),
        name="in_proj",
    )(*args)


def _t5_bucket_np(dist):
    n = np.maximum(dist, 0)
    max_exact = N_BUCKETS // 2
    nf = np.maximum(n, 1).astype(np.float32)
    large = max_exact + (np.log(nf / np.float32(max_exact)) / np.float32(math.log(MAX_DISTANCE / max_exact))
                         * np.float32(N_BUCKETS - max_exact)).astype(np.int32)
    large = np.minimum(large, N_BUCKETS - 1)
    return np.where(n < max_exact, n, large).astype(np.int32)


def _lambda_scalar(lq, lam_init):
    s01 = jnp.sum(lq[0:1, :] * lq[1:2, :], axis=1, keepdims=True)
    s23 = jnp.sum(lq[2:3, :] * lq[3:4, :], axis=1, keepdims=True)
    return jnp.exp(s01) - jnp.exp(s23) + lam_init


def _sub_norm(o, w_row, lam_init):
    o = o * lax.rsqrt(jnp.mean(o * o, axis=-1, keepdims=True) + LN_EPS)
    return o * w_row * (1.0 - lam_init)


def _pattn_kernel(qi_tab, ki_tab, q_ref, k_ref, v_ref, bias_ref, lq_ref, sw_ref, o_ref,
                  qt_sc, m_sc, acc_sc, *, tq, lam_init):
    t = pl.program_id(2)
    qi = qi_tab[t]
    ki = ki_tab[t]
    tk = k_ref.shape[0]

    @pl.when(ki == 0)
    def _():
        q = q_ref[...] * (DA_HEAD_DIM ** -0.5 * LOG2E)
        lane = lax.broadcasted_iota(jnp.int32, q.shape, 1)
        qt_sc[:, :tq] = jnp.where(lane < DA_HEAD_DIM, q, 0.0).T.astype(BF16)
        qt_sc[:, tq:] = jnp.where(lane >= DA_HEAD_DIM, q, 0.0).T.astype(BF16)
        m_sc[...] = jnp.full_like(m_sc, -jnp.inf)
        acc_sc[...] = jnp.zeros_like(acc_sc)

    def accumulate(with_bias, diagonal):
        kb = k_ref[...].astype(BF16)
        vt = jnp.concatenate([v_ref[...].T, jnp.ones((ACC_PAD, tk), F32)], axis=0).astype(BF16)
        n_chunks = 2 * tq // QCOLS
        cols = [slice(c * QCOLS, (c + 1) * QCOLS) for c in range(n_chunks)]
        keys = [((c * QCOLS) % tq + QCOLS) if diagonal else tk for c in range(n_chunks)]

        def scores(c):
            s = jnp.dot(kb[:keys[c]], qt_sc[:, cols[c]], preferred_element_type=F32)
            if with_bias:
                b0 = (c * QCOLS) % tq
                s = s + bias_ref[:keys[c], b0:b0 + QCOLS]
            return s

        def softmax(c, s):
            m_old = m_sc[:, cols[c]]
            m_new = jnp.maximum(m_old, jnp.max(s, axis=0, keepdims=True))
            m_sc[:, cols[c]] = m_new
            return jnp.exp2(s - m_new).astype(BF16), jnp.exp2(m_old - m_new)

        def update(c, p, alpha):
            acc_sc[:, cols[c]] = alpha * acc_sc[:, cols[c]] + jnp.dot(vt[:, :keys[c]], p,
                                                                      preferred_element_type=F32)

        s_q = {0: scores(0), 1: scores(1)}
        for c in range(n_chunks):
            p, alpha = softmax(c, s_q.pop(c))
            if c + 2 < n_chunks:
                s_q[c + 2] = scores(c + 2)
            update(c, p, alpha)

    @pl.when(qi == ki)
    def _():
        accumulate(True, True)

    @pl.when(qi - ki == 1)
    def _():
        accumulate(True, False)

    @pl.when(qi - ki > 1)
    def _():
        accumulate(False, False)

    @pl.when(ki == qi)
    def _():
        acc = acc_sc[...]
        o_all = acc[:DA_V_DIM] * (1.0 / acc[DA_V_DIM:DA_V_DIM + 1])
        lam = _lambda_scalar(lq_ref[...], lam_init)
        o = o_all[:, :tq] - lam * o_all[:, tq:]
        o = o * lax.rsqrt(jnp.mean(o * o, axis=0, keepdims=True) + LN_EPS)
        o = o * (sw_ref[...] * (1.0 - lam_init))
        o_ref[...] = o.T.astype(o_ref.dtype)


def _prompt_attn(proj, bias_tiles, lam_qk, subln_col, layer, lam_init, batch, seq, tq):
    nq = seq // tq
    tri = [(qi, ki) for qi in range(nq) for ki in range(qi + 1)]
    qi_tab = jnp.asarray([a for a, _ in tri], jnp.int32)
    ki_tab = jnp.asarray([b for _, b in tri], jnp.int32)
    hq, hk, hv = 0, ATT_W // DA_V_DIM, 2 * ATT_W // DA_V_DIM

    def bias_idx(b, h, t, qt, kt):
        return (h, jnp.minimum(qt[t] - kt[t], 1), 0, 0)

    grid_spec = pltpu.PrefetchScalarGridSpec(
        num_scalar_prefetch=2,
        grid=(batch, DA_HEADS, len(tri)),
        in_specs=[
            pl.BlockSpec((tq, DA_V_DIM), lambda b, h, t, qt, kt: (b * nq + qt[t], hq + h)),
            pl.BlockSpec((tq, DA_V_DIM), lambda b, h, t, qt, kt: (b * nq + kt[t], hk + h)),
            pl.BlockSpec((tq, DA_V_DIM), lambda b, h, t, qt, kt: (b * nq + kt[t], hv + h)),
            pl.BlockSpec((None, None, tq, tq), bias_idx),
            pl.BlockSpec((None, 4, DA_HEAD_DIM), lambda b, h, t, qt, kt: (layer, 0, 0)),
            pl.BlockSpec((None, DA_V_DIM, 1), lambda b, h, t, qt, kt: (layer, 0, 0)),
        ],
        out_specs=pl.BlockSpec((tq, DA_V_DIM), lambda b, h, t, qt, kt: (b * nq + qt[t], h)),
        scratch_shapes=[pltpu.VMEM((DA_V_DIM, 2 * tq), BF16), pltpu.VMEM((1, 2 * tq), F32),
                        pltpu.VMEM((DA_V_DIM + ACC_PAD, 2 * tq), F32)],
    )
    return pl.pallas_call(
        functools.partial(_pattn_kernel, tq=tq, lam_init=lam_init),
        grid_spec=grid_spec,
        out_shape=jax.ShapeDtypeStruct((batch * seq, ATT_W), BF16),
        compiler_params=_cparams(("parallel", "parallel", "arbitrary")),
        name="prompt_attn",
    )(qi_tab, ki_tab, proj, proj, proj, bias_tiles, lam_qk, subln_col)


def _skew_kernel(y_ref, o_ref, *, tq, rows):
    for r0 in range(0, tq, rows):
        x = jnp.broadcast_to(y_ref[...], (rows, 2 * tq))
        o_ref[r0:r0 + rows, :] = pltpu.roll(x, r0, 1, stride=1, stride_axis=0)[:, :tq]


def _prompt_bias_tiles(rel_bias, tq):
    assert tq >= MAX_DISTANCE and np.all(_t5_bucket_np(np.arange(MAX_DISTANCE, 1 << 16)) == N_BUCKETS - 1)
    span = 2 * tq
    d_row = np.zeros(span, np.int64)
    d_row[:tq] = np.arange(tq)
    d_row[span - np.arange(1, tq)] = -np.arange(1, tq)
    far = rel_bias[N_BUCKETS - 1]
    rows = []
    for delta in range(2):
        dist = d_row + delta * tq
        y = (rel_bias[_t5_bucket_np(dist)] - far) * LOG2E
        rows.append(jnp.where(jnp.asarray(dist >= 0)[:, None], y, NEG_BIG).T)
    y = jnp.stack(rows, axis=1).reshape(DA_HEADS, 2, 1, span)
    return pl.pallas_call(
        functools.partial(_skew_kernel, tq=tq, rows=256),
        grid=(DA_HEADS, 2),
        in_specs=[pl.BlockSpec((None, None, 1, span), lambda h, d: (h, d, 0, 0))],
        out_specs=pl.BlockSpec((None, None, tq, tq), lambda h, d: (h, d, 0, 0)),
        out_shape=jax.ShapeDtypeStruct((DA_HEADS, 2, tq, tq), F32),
        compiler_params=_cparams(("parallel", "parallel")),
        name="bias_skew",
    )(y)


def _page_rows(page_ref):
    return jnp.concatenate([page_ref[pl.ds(h, PAGE_SIZE, stride=DA_HEADS), :] for h in range(DA_HEADS)],
                           axis=1).astype(BF16)


def _sattn_kernel(pt_ref, w_ref, kn_ref, vn_ref, bl_ref, bn_ref, bf_ref, lq_ref, sw_ref, *rest,
                  n_group, n_pages, n_tok, lam_init):
    k_refs = rest[:n_group]
    v_refs = rest[n_group:2 * n_group]
    o_ref = rest[2 * n_group]
    m_sc, l_sc, acc_sc = rest[2 * n_group + 1:]
    step = pl.program_id(1)
    n_steps = pl.num_programs(1)
    hcols = 2 * n_tok
    w = w_ref[...]

    @pl.when(step == 0)
    def _():
        m_sc[...] = jnp.full_like(m_sc, -jnp.inf)
        l_sc[...] = jnp.zeros_like(l_sc)
        acc_sc[...] = jnp.zeros_like(acc_sc)

    def online_update(s, v_rows):
        m_old = m_sc[...]
        m_new = jnp.maximum(m_old, jnp.max(s, axis=0, keepdims=True))
        p = jnp.exp2(s - m_new)
        alpha = jnp.exp2(m_old - m_new)
        l_sc[...] = alpha * l_sc[...] + jnp.sum(p, axis=0, keepdims=True)
        m_sc[...] = m_new
        pt = p.T.astype(BF16)
        alpha_col = jnp.broadcast_to(alpha, (8, alpha.shape[1])).T[:, 0:1]
        for h in range(DA_HEADS):
            rows = slice(h * hcols, (h + 1) * hcols)
            d = jnp.dot(pt[rows], v_rows[:, h * DA_V_DIM:(h + 1) * DA_V_DIM], preferred_element_type=F32)
            acc_sc[h] = alpha_col[rows] * acc_sc[h] + d

    k_rows = jnp.concatenate([_page_rows(k_refs[g]) for g in range(n_group)], axis=0)
    s = jnp.dot(k_rows, w, preferred_element_type=F32)
    tail = jnp.where(step == n_steps - 1, bl_ref[...], bf_ref[...])
    s = s + jnp.concatenate([bf_ref[...]] * (n_group - 1) + [tail], axis=0)
    online_update(s, jnp.concatenate([_page_rows(v_refs[g]) for g in range(n_group)], axis=0))

    @pl.when(step == n_steps - 1)
    def _():
        pad = jnp.zeros((2 * n_tok - n_tok, kn_ref.shape[1]), F32)
        kn = jnp.concatenate([kn_ref[...], pad], axis=0).astype(BF16)
        vn = jnp.concatenate([vn_ref[...], pad], axis=0).astype(BF16)
        online_update(jnp.dot(kn, w, preferred_element_type=F32) + bn_ref[...], vn)
        lam = _lambda_scalar(lq_ref[...], lam_init)
        inv_l = jnp.broadcast_to(1.0 / l_sc[...], (8, l_sc.shape[1])).T[:, 0:1]
        for h in range(DA_HEADS):
            o = acc_sc[h] * inv_l[h * hcols:(h + 1) * hcols]
            o = o[:n_tok] - lam * o[n_tok:]
            o_ref[:, h * DA_V_DIM:(h + 1) * DA_V_DIM] = _sub_norm(o, sw_ref[...], lam_init).astype(o_ref.dtype)


def _sample_attn(proj, cache_k, cache_v, page_table, rel_bias, lam_qk, subln_w, layer, lam_init,
                 dec_batch, n_tok, n_group):
    n_pages = page_table.shape[1]
    assert n_pages % n_group == 0 and 2 * DA_HEADS * n_tok == DA_V_DIM
    past = n_pages * PAGE_SIZE
    n_steps = n_pages // n_group
    ncol = 2 * DA_HEADS * n_tok
    page_rows = PAGE_SIZE * DA_HEADS

    q = proj[:, :ATT_W].reshape(dec_batch, n_tok, DA_HEADS, 2, DA_HEAD_DIM) * (DA_HEAD_DIM ** -0.5 * LOG2E)
    w = jnp.einsum('bthjd,hH,jJ->bhjdHJt', q, jnp.eye(DA_HEADS, dtype=F32), jnp.eye(2, dtype=F32))
    w = w.reshape(dec_batch, ATT_W, ncol).astype(BF16)

    tok = np.arange(n_tok)

    def col_bias(dist, pad_rows=0):
        b = rel_bias[_t5_bucket_np(dist)] * LOG2E
        b = jnp.where(jnp.asarray(dist >= 0)[..., None], b, NEG_BIG)
        b = jnp.transpose(b, (0, 2, 1))[:, :, None, :]
        b = jnp.broadcast_to(b, (dist.shape[0], DA_HEADS, 2, n_tok)).reshape(dist.shape[0], ncol)
        if pad_rows:
            b = jnp.concatenate([b, jnp.full((pad_rows, ncol), NEG_BIG, F32)], axis=0)
        return b

    assert np.all(_t5_bucket_np(np.arange(PAGE_SIZE + 1, past + n_tok + 1)) == N_BUCKETS - 1)
    bias_last = col_bias((past + tok)[None, :] - (past - PAGE_SIZE + np.arange(PAGE_SIZE))[:, None])
    bias_new = col_bias(tok[None, :] - tok[:, None], pad_rows=n_tok)
    bias_far = col_bias(np.full((PAGE_SIZE, n_tok), PAGE_SIZE + 1))

    page_block = (None, None, page_rows, DA_V_DIM)
    cache_k = cache_k.reshape(cache_k.shape[0], cache_k.shape[1], page_rows, DA_V_DIM)
    cache_v = cache_v.reshape(cache_v.shape[0], cache_v.shape[1], page_rows, DA_V_DIM)

    def page_idx(g):
        return lambda b, s, pt: (layer, pt[b, s * n_group + g], 0, 0)

    const2 = lambda b, s, pt: (0, 0)
    grid_spec = pltpu.PrefetchScalarGridSpec(
        num_scalar_prefetch=1,
        grid=(dec_batch, n_steps),
        in_specs=[
            pl.BlockSpec((None, ATT_W, ncol), lambda b, s, pt: (b, 0, 0)),
            pl.BlockSpec((n_tok, ATT_W), lambda b, s, pt: (b, 1)),
            pl.BlockSpec((n_tok, ATT_W), lambda b, s, pt: (b, 2)),
            pl.BlockSpec((PAGE_SIZE, ncol), const2),
            pl.BlockSpec((2 * n_tok, ncol), const2),
            pl.BlockSpec((PAGE_SIZE, ncol), const2),
            pl.BlockSpec((None, 4, DA_HEAD_DIM), lambda b, s, pt: (layer, 0, 0)),
            pl.BlockSpec((None, 1, DA_V_DIM), lambda b, s, pt: (layer, 0, 0)),
        ] + [pl.BlockSpec(page_block, page_idx(g)) for g in range(n_group)] * 2,
        out_specs=pl.BlockSpec((n_tok, ATT_W), lambda b, s, pt: (b, 0)),
        scratch_shapes=[pltpu.VMEM((1, ncol), F32), pltpu.VMEM((1, ncol), F32),
                        pltpu.VMEM((DA_HEADS, 2 * n_tok, DA_V_DIM), F32)],
    )
    return pl.pallas_call(
        functools.partial(_sattn_kernel, n_group=n_group, n_pages=n_pages, n_tok=n_tok,
                          lam_init=lam_init),
        grid_spec=grid_spec,
        out_shape=jax.ShapeDtypeStruct((dec_batch * n_tok, ATT_W), BF16),
        compiler_params=_cparams(("parallel", "arbitrary")),
        name="sample_attn",
    )(page_table, w, proj, proj, bias_last, bias_new, bias_far, lam_qk, subln_w,
      *([cache_k] * n_group), *([cache_v] * n_group))


def _mlstm_kernel(q_ref, k_ref, v_ref, o_gate_ref, g_ref, nw_ref, c0_ref, n0_ref, m0_ref,
                  h_ref, c_ref, n_ref, m_ref, *, rows, chunk):
    c_idx = pl.program_id(1)

    @pl.when(c_idx == 0)
    def _():
        c_ref[...] = c0_ref[...]
        n_ref[...] = n0_ref[...]
        m_ref[...] = m0_ref[...]

    row = lax.broadcasted_iota(jnp.int32, (chunk, chunk), 0)
    col = lax.broadcasted_iota(jnp.int32, (chunk, chunk), 1)
    tril = row >= col
    eye = row == col

    def padded(x):
        if rows == chunk:
            return x
        return jnp.concatenate([x, jnp.zeros((chunk - rows, x.shape[1]), x.dtype)], axis=0)

    for h in range(ML_HEADS):
        sl = slice(h * ML_DQK, (h + 1) * ML_DQK)
        q = padded(q_ref[:, sl])
        k = padded(k_ref[:, sl]) * (ML_DQK ** -0.5)
        v = padded(v_ref[:, sl])
        ig = g_ref[h:h + 1, :]
        gf = g_ref[ML_HEADS + h:ML_HEADS + h + 1, :]
        lf = jnp.minimum(gf, 0.0) - jnp.log1p(jnp.exp(-jnp.abs(gf)))
        c_state = c_ref[h]
        n_state = n_ref[h]
        m_state = m_ref[:, h:h + 1]

        b_col = jnp.sum(jnp.where(tril, lf, 0.0), axis=1, keepdims=True)
        b_row = jnp.sum(jnp.where(eye, b_col, 0.0), axis=0, keepdims=True)
        logw = jnp.where(tril, b_col - b_row + ig, NEG_BIG)
        m_inter = b_col + m_state
        m_t = jnp.maximum(jnp.max(logw, axis=1, keepdims=True), m_inter)
        inter = jnp.exp(m_inter - m_t)
        qb = q.astype(BF16)
        kb = k.astype(BF16)
        s = lax.dot_general(qb, kb, (((1,), (1,)), ((), ())), preferred_element_type=F32)
        s = s * jnp.exp(logw - m_t)
        num = jnp.dot(s.astype(BF16), v.astype(BF16), preferred_element_type=F32)
        num = num + inter * lax.dot_general(qb, c_state.astype(BF16), (((1,), (1,)), ((), ())),
                                            preferred_element_type=F32)
        den = jnp.sum(s, axis=1, keepdims=True) + inter * jnp.sum(q * n_state, axis=1, keepdims=True)
        hh = num / jnp.maximum(jnp.abs(den), jnp.exp(-m_t))

        b_last = jnp.sum(lf, axis=1, keepdims=True)
        logw_end = b_last - b_row + ig
        m_new = jnp.maximum(b_last + m_state, jnp.max(logw_end, axis=1, keepdims=True))
        w_end = jnp.exp(logw_end - m_new)
        decay = jnp.exp(b_last + m_state - m_new)
        w_col = jnp.sum(jnp.where(eye, w_end, 0.0), axis=1, keepdims=True)
        vw = (v * w_col).astype(BF16)
        c_ref[h] = decay * c_state + lax.dot_general(vw, kb, (((0,), (0,)), ((), ())),
                                                     preferred_element_type=F32)
        n_ref[h] = decay * n_state + jnp.sum(k * w_col, axis=0, keepdims=True)
        m_ref[:, h:h + 1] = m_new

        hh = hh[:rows]
        mc = hh - jnp.mean(hh, axis=1, keepdims=True)
        y = mc * lax.rsqrt(jnp.mean(mc * mc, axis=1, keepdims=True) + LN_EPS) * nw_ref[:, sl]
        h_ref[:, sl] = (jax.nn.sigmoid(o_gate_ref[:, sl]) * y).astype(h_ref.dtype)


def _mlstm(proj, gates_t, norm_w, c0, n0, m0, layer, batch, seq, rows_per_block):
    nc = seq // rows_per_block
    chunk = gates_t.shape[2] // nc
    cq, ck, cv, co = (3 * ATT_W // ML_W, 3 * ATT_W // ML_W + 1, 3 * ATT_W // ML_W + 2,
                      3 * ATT_W // ML_W + 3)
    rows = lambda col: pl.BlockSpec((rows_per_block, ML_W), lambda b, c: (b * nc + c, col))
    state_c = pl.BlockSpec((None, ML_HEADS, ML_DV, ML_DQK), lambda b, c: (b, 0, 0, 0))
    state_n = pl.BlockSpec((None, ML_HEADS, 1, ML_DQK), lambda b, c: (b, 0, 0, 0))
    state_m = pl.BlockSpec((None, 1, ML_HEADS), lambda b, c: (b, 0, 0))
    return pl.pallas_call(
        functools.partial(_mlstm_kernel, rows=rows_per_block, chunk=chunk),
        grid=(batch, nc),
        in_specs=[rows(cq), rows(ck), rows(cv), rows(co),
                  pl.BlockSpec((None, 2 * ML_HEADS, chunk), lambda b, c: (b, 0, c)),
                  pl.BlockSpec((None, 1, ML_W), lambda b, c: (layer, 0, 0)),
                  state_c, state_n, state_m],
        out_specs=[pl.BlockSpec((rows_per_block, ML_W), lambda b, c: (b * nc + c, 0)),
                   state_c, state_n, state_m],
        out_shape=[jax.ShapeDtypeStruct((batch * seq, ML_W), BF16),
                   jax.ShapeDtypeStruct((batch, ML_HEADS, ML_DV, ML_DQK), F32),
                   jax.ShapeDtypeStruct((batch, ML_HEADS, 1, ML_DQK), F32),
                   jax.ShapeDtypeStruct((batch, 1, ML_HEADS), F32)],
        compiler_params=_cparams(("parallel", "arbitrary")),
        name="mlstm",
    )(proj, proj, proj, proj, gates_t, norm_w, c0, n0, m0)


def _merge_kernel(ao_ref, mh_ref, wa_ref, wm_ref, ga_ref, gm_ref, ba_ref, bm_ref, o_ref):
    ta = jnp.dot(ao_ref[...], wa_ref[...], preferred_element_type=F32)
    tmm = jnp.dot(mh_ref[...], wm_ref[...], preferred_element_type=F32)
    ga = jax.nn.sigmoid(ga_ref[...] + ba_ref[...])
    gm = jax.nn.sigmoid(gm_ref[...] + bm_ref[...])
    o_ref[...] = (ga * ta + gm * tmm).astype(o_ref.dtype)


def _merge(ao, mh, proj, w_ba, w_bm, b_gate, layer, tm, tn):
    m = ao.shape[0]
    nd = D_MODEL // tn
    ga0 = COL_GATE // tn
    return pl.pallas_call(
        _merge_kernel,
        grid=(m // tm, nd),
        in_specs=[
            pl.BlockSpec((tm, ATT_W), lambda i, j: (i, 0)),
            pl.BlockSpec((tm, ML_W), lambda i, j: (i, 0)),
            pl.BlockSpec((None, ATT_W, tn), lambda i, j: (layer, 0, j)),
            pl.BlockSpec((None, ML_W, tn), lambda i, j: (layer, 0, j)),
            pl.BlockSpec((tm, tn), lambda i, j: (i, ga0 + j)),
            pl.BlockSpec((tm, tn), lambda i, j: (i, ga0 + nd + j)),
            pl.BlockSpec((None, 1, tn), lambda i, j: (layer, 0, j)),
            pl.BlockSpec((None, 1, tn), lambda i, j: (layer, 0, nd + j)),
        ],
        out_specs=pl.BlockSpec((tm, tn), lambda i, j: (i, j)),
        out_shape=jax.ShapeDtypeStruct((m, D_MODEL), BF16),
        compiler_params=_cparams(("parallel", "parallel")),
        name="merge",
    )(ao, mh, w_ba, w_bm, proj, proj, b_gate, b_gate)


def _out_kernel(x_ref, mg_ref, w_ref, g_ref, b_ref, o_ref, ob_ref):
    y = DN_ALPHA * x_ref[...] + jnp.dot(mg_ref[...], w_ref[...], preferred_element_type=F32)
    o = _layer_norm_rows(y, g_ref[...], b_ref[...])
    o_ref[...] = o
    ob_ref[...] = o.astype(BF16)


def _out_proj_ln(x, merged, w_out, ln_g, ln_b, layer, tm):
    m = x.shape[0]
    return pl.pallas_call(
        _out_kernel,
        grid=(m // tm,),
        in_specs=[
            pl.BlockSpec((tm, D_MODEL), lambda i: (i, 0)),
            pl.BlockSpec((tm, D_MODEL), lambda i: (i, 0)),
            pl.BlockSpec((None, D_MODEL, D_MODEL), lambda i: (layer, 0, 0)),
            pl.BlockSpec((None, None, 1, D_MODEL), lambda i: (layer, 1, 0, 0)),
            pl.BlockSpec((None, None, 1, D_MODEL), lambda i: (layer, 1, 0, 0)),
        ],
        out_specs=[pl.BlockSpec((tm, D_MODEL), lambda i: (i, 0)),
                   pl.BlockSpec((tm, D_MODEL), lambda i: (i, 0))],
        out_shape=[jax.ShapeDtypeStruct((m, D_MODEL), F32),
                   jax.ShapeDtypeStruct((m, D_MODEL), BF16)],
        compiler_params=_cparams(("parallel",)),
        name="out_proj_ln",
    )(x, merged, w_out, ln_g, ln_b)


def _gates_transposed(pif, b_if, batch, seq, pad_to):
    gif = pif[:, :N_IF] + b_if
    gt = jnp.transpose(gif.reshape(batch, seq, N_IF), (0, 2, 1))
    if pad_to > seq:
        pad = jnp.concatenate([jnp.full((batch, ML_HEADS, pad_to - seq), NEG_BIG, F32),
                               jnp.full((batch, ML_HEADS, pad_to - seq), 1e4, F32)], axis=1)
        gt = jnp.concatenate([gt, pad], axis=2)
    return gt


def kernel(x_prompt, x_sample, cache_k, cache_v, page_table, state_C, state_n, state_m, rel_bias,
           w_in, b_gate, b_if, lam_qk, subln_w, mlstm_norm_w, w_branch_attn, w_branch_mlstm,
           w_out, ffn_up, ffn_down, ln_g, ln_b):
    batch, seq, _ = x_prompt.shape
    dec_batch, dec_seq, _ = x_sample.shape
    mp, ms = batch * seq, dec_batch * dec_seq
    tm_p, tm_s = 512, ms
    tq = 1024
    prompt_chunk = 256

    w_cat = jnp.concatenate([w_in[:, :, :N_MAIN], w_in[:, :, N_MAIN + N_IF:]], axis=2).astype(BF16)
    w_if = jnp.pad(w_in[:, :, N_MAIN:N_MAIN + N_IF], ((0, 0), (0, 0), (0, N_IF_PAD - N_IF))).astype(BF16)
    up_b = ffn_up.astype(BF16)
    down_b = ffn_down.astype(BF16)
    wba_b = w_branch_attn.astype(BF16)
    wbm_b = w_branch_mlstm.astype(BF16)
    wout_b = w_out.astype(BF16)
    ln_g = ln_g.reshape(DEPTH, 3, 1, D_MODEL)
    ln_b = ln_b.reshape(DEPTH, 3, 1, D_MODEL)
    b_gate3 = b_gate.reshape(DEPTH, 1, N_GATE)
    subln3 = subln_w.reshape(DEPTH, 1, DA_V_DIM)
    subln_col = subln_w.reshape(DEPTH, DA_V_DIM, 1)
    normw3 = mlstm_norm_w.reshape(DEPTH, 1, ML_W)
    bias_tiles = _prompt_bias_tiles(rel_bias, tq)

    zero_c = jnp.zeros((batch, ML_HEADS, ML_DV, ML_DQK), F32)
    zero_n = jnp.zeros((batch, ML_HEADS, 1, ML_DQK), F32)
    zero_m = jnp.zeros((batch, 1, ML_HEADS), F32)

    xp = x_prompt.reshape(mp, D_MODEL)
    xs = x_sample.reshape(ms, D_MODEL)
    xpb = xp.astype(BF16)
    xsb = xs.astype(BF16)

    outs = {name: [] for name in ('cp', 'np', 'mp', 'cs', 'ns', 'ms')}
    kv_p = kv_s = None
    for l in range(DEPTH):
        lam_init = 0.8 - 0.6 * math.exp(-0.3 * l)

        def rowwise_pre(x, xb, kv, tm):
            x1, x1b = _ffn_ln(x, xb, up_b, down_b, ln_g, ln_b, l, 0, 0, tm, 512)
            proj, pif, kv = _in_proj(x1b, w_cat, w_if, kv, l, min(2 * tm, x1b.shape[0]))
            return x1, proj, pif, kv

        def rowwise_post(x1, proj, ao, mh, tm):
            merged = _merge(ao, mh, proj, wba_b, wbm_b, b_gate3, l, tm, 512)
            x2, x2b = _out_proj_ln(x1, merged, wout_b, ln_g, ln_b, l, min(tm, 256))
            return _ffn_ln(x2, x2b, up_b, down_b, ln_g, ln_b, l, 1, 2, tm, 512)

        x1, proj, pif, kv_p = rowwise_pre(xp, xpb, kv_p, tm_p)
        ao = _prompt_attn(proj, bias_tiles, lam_qk, subln_col, l, lam_init, batch, seq, tq)
        gt = _gates_transposed(pif, b_if[l], batch, seq, seq)
        mh, c_new, n_new, m_new = _mlstm(proj, gt, normw3, zero_c, zero_n, zero_m, l, batch, seq,
                                         prompt_chunk)
        xp, xpb = rowwise_post(x1, proj, ao, mh, tm_p)
        outs['cp'].append(c_new)
        outs['np'].append(n_new.reshape(batch, ML_HEADS, ML_DQK))
        outs['mp'].append(m_new.reshape(batch, ML_HEADS))

        x1, proj, pif, kv_s = rowwise_pre(xs, xsb, kv_s, tm_s)
        ao = _sample_attn(proj, cache_k, cache_v, page_table, rel_bias, lam_qk, subln3, l, lam_init,
                          dec_batch, dec_seq, 8)
        gt = _gates_transposed(pif, b_if[l], dec_batch, dec_seq, 128)
        mh, c_new, n_new, m_new = _mlstm(proj, gt, normw3, state_C[l],
                                         state_n[l].reshape(dec_batch, ML_HEADS, 1, ML_DQK),
                                         state_m[l].reshape(dec_batch, 1, ML_HEADS), l, dec_batch,
                                         dec_seq, dec_seq)
        xs, xsb = rowwise_post(x1, proj, ao, mh, tm_s)
        outs['cs'].append(c_new)
        outs['ns'].append(n_new.reshape(dec_batch, ML_HEADS, ML_DQK))
        outs['ms'].append(m_new.reshape(dec_batch, ML_HEADS))

    st = lambda name: jnp.stack(outs[name])
    kv_shape_p = (DEPTH, batch, seq, DA_HEADS, DA_V_DIM)
    kv_shape_s = (DEPTH, dec_batch, dec_seq, DA_HEADS, DA_V_DIM)
    return (xp.reshape(batch, seq, D_MODEL), xs.reshape(dec_batch, dec_seq, D_MODEL),
            kv_p[0].reshape(kv_shape_p), kv_p[1].reshape(kv_shape_p),
            kv_s[0].reshape(kv_shape_s), kv_s[1].reshape(kv_shape_s), st('cp'), st('np'), st('mp'),
            st('cs'), st('ns'), st('ms'))
```

```python
import functools
import math

import numpy as np
import jax
import jax.numpy as jnp
from jax import lax
from jax.experimental import pallas as pl
from jax.experimental.pallas import tpu as pltpu

F32 = jnp.float32
BF16 = jnp.bfloat16

D_MODEL = 2048
DEPTH = 4
PAGE_SIZE = 128
DA_HEADS = 8
DA_HEAD_DIM = 64
DA_V_DIM = 2 * DA_HEAD_DIM
ML_HEADS = 4
ML_DQK = 256
ML_DV = 256
D_FF = 5632
N_BUCKETS = 32
MAX_DISTANCE = 128
LN_EPS = 1e-5
DN_ALPHA = (2.0 * DEPTH) ** 0.25

ATT_W = DA_HEADS * DA_V_DIM
ML_W = ML_HEADS * ML_DV
N_MAIN = 3 * ATT_W + 4 * ML_W
N_GATE = 2 * D_MODEL
N_IF = 2 * ML_HEADS
N_IF_PAD = 128
N_PROJ = N_MAIN + N_GATE
PROJ_TN = 1024
KV_COL_BLOCK = ATT_W // PROJ_TN
COL_GATE = N_MAIN

NEG_BIG = -1e30
LOG2E = math.log2(math.e)
QCOLS = 512
ACC_PAD = 16
VMEM_LIMIT = 48 * 1024 * 1024


def _cparams(sem):
    return pltpu.CompilerParams(dimension_semantics=sem, vmem_limit_bytes=VMEM_LIMIT)


def _layer_norm_rows(y, g, b):
    mu = jnp.mean(y, axis=-1, keepdims=True)
    yc = y - mu
    var = jnp.mean(yc * yc, axis=-1, keepdims=True)
    return yc * lax.rsqrt(var + LN_EPS) * g + b


def _ffn_kernel(x_ref, xb_ref, wa_ref, wb_ref, wd_ref, g_ref, b_ref, o_ref, ob_ref, *rest):
    acc_ref = rest[-1]
    j = pl.program_id(1)

    @pl.when(j == 0)
    def _():
        acc_ref[...] = jnp.zeros_like(acc_ref)

    wa, wb, wd = wa_ref[...].astype(BF16), wb_ref[...].astype(BF16), wd_ref[...].astype(BF16)
    for w_out_ref, w in zip(rest[:-1], (wa, wb, wd)):
        w_out_ref[...] = w
    xb = xb_ref[...]
    a = jnp.dot(xb, wa, preferred_element_type=F32)
    b = jnp.dot(xb, wb, preferred_element_type=F32)
    h = (a * jax.nn.sigmoid(a) * b).astype(BF16)
    acc_ref[...] += jnp.dot(h, wd, preferred_element_type=F32)

    @pl.when(j == pl.num_programs(1) - 1)
    def _():
        y = DN_ALPHA * x_ref[...] + 0.5 * acc_ref[...]
        o = _layer_norm_rows(y, g_ref[...], b_ref[...])
        o_ref[...] = o
        ob_ref[...] = o.astype(BF16)


def _ffn_ln(x, xb, weights, ln_g, ln_b, layer, which, ln_idx, tm, tf):
    m = x.shape[0]
    nf = D_FF // tf
    assert m == tm or len(weights) == 3, "bf16 weight copies need a single row tile (each tile written once)"
    row_spec = pl.BlockSpec((tm, D_MODEL), lambda i, j: (i, 0))
    ln_spec = pl.BlockSpec((None, None, 1, D_MODEL), lambda i, j: (layer, ln_idx, 0, 0))
    up_spec = pl.BlockSpec((D_MODEL, tf), lambda i, j: (0, j))
    down_spec = pl.BlockSpec((tf, D_MODEL), lambda i, j: (j, 0))
    out_specs = [row_spec, row_spec]
    out_shape = [jax.ShapeDtypeStruct((m, D_MODEL), F32), jax.ShapeDtypeStruct((m, D_MODEL), BF16)]
    if len(weights) == 2:
        w_up, w_down = weights
        w_args = (w_up, w_up, w_down)
        w_specs = [pl.BlockSpec((None, None, D_MODEL, tf), lambda i, j: (layer, which, 0, j)),
                   pl.BlockSpec((None, None, D_MODEL, tf), lambda i, j: (layer, which, 0, j + nf)),
                   pl.BlockSpec((None, None, tf, D_MODEL), lambda i, j: (layer, which, j, 0))]
        out_specs += [up_spec, up_spec, down_spec]
        out_shape += [jax.ShapeDtypeStruct((D_MODEL, D_FF), BF16)] * 2 + [jax.ShapeDtypeStruct((D_FF, D_MODEL), BF16)]
    else:
        w_args = tuple(weights)
        w_specs = [up_spec, up_spec, down_spec]
    outs = pl.pallas_call(
        _ffn_kernel,
        grid=(m // tm, nf),
        in_specs=[row_spec, row_spec] + w_specs + [ln_spec, ln_spec],
        out_specs=out_specs,
        out_shape=out_shape,
        scratch_shapes=[pltpu.VMEM((tm, D_MODEL), F32)],
        compiler_params=_cparams(("parallel", "arbitrary")),
        name="ffn_ln",
    )(x, xb, *w_args, ln_g, ln_b)
    return outs[0], outs[1], tuple(outs[2:])


def _proj_kernel(xb_ref, w_ref, wif_ref, *rest):
    o_ref, oif_ref, kv_ref = rest[-3:]
    j = pl.program_id(1)
    o = jnp.dot(xb_ref[...], w_ref[...], preferred_element_type=F32)
    o_ref[...] = o

    @pl.when(j == 0)
    def _():
        oif_ref[...] = jnp.dot(xb_ref[...], wif_ref[...], preferred_element_type=F32)

    @pl.when((j == KV_COL_BLOCK) | (j == KV_COL_BLOCK + 1))
    def _():
        kv_ref[...] = o


def _in_proj(xb, w_cat, w_if, kv_buf, layer, tm):
    m = xb.shape[0]
    in_specs = [
        pl.BlockSpec((tm, D_MODEL), lambda i, j: (i, 0)),
        pl.BlockSpec((None, D_MODEL, PROJ_TN), lambda i, j: (layer, 0, j)),
        pl.BlockSpec((None, D_MODEL, N_IF_PAD), lambda i, j: (layer, 0, 0)),
    ]
    args = [xb, w_cat, w_if]
    aliases = {}
    if kv_buf is not None:
        in_specs.append(pl.BlockSpec(memory_space=pl.ANY))
        args.append(kv_buf)
        aliases = {3: 2}
    return pl.pallas_call(
        _proj_kernel,
        grid=(m // tm, N_PROJ // PROJ_TN),
        in_specs=in_specs,
        out_specs=[pl.BlockSpec((tm, PROJ_TN), lambda i, j: (i, j)),
                   pl.BlockSpec((tm, N_IF_PAD), lambda i, j: (i, 0)),
                   pl.BlockSpec((None, None, tm, ATT_W),
                                lambda i, j: (jnp.clip(j - KV_COL_BLOCK, 0, 1), layer, i, 0))],
        out_shape=[jax.ShapeDtypeStruct((m, N_PROJ), F32),
                   jax.ShapeDtypeStruct((m, N_IF_PAD), F32),
                   jax.ShapeDtypeStruct((2, DEPTH, m, ATT_W), F32)],
        input_output_aliases=aliases,
        compiler_params=_cparams(("parallel", "arbitrary")),
        name="in_proj",
    )(*args)


def _t5_bucket_np(dist):
    n = np.maximum(dist, 0)
    max_exact = N_BUCKETS // 2
    nf = np.maximum(n, 1).astype(np.float32)
    large = max_exact + (np.log(nf / np.float32(max_exact)) / np.float32(math.log(MAX_DISTANCE / max_exact))
                         * np.float32(N_BUCKETS - max_exact)).astype(np.int32)
    large = np.minimum(large, N_BUCKETS - 1)
    return np.where(n < max_exact, n, large).astype(np.int32)


def _lambda_scalar(lq, lam_init):
    s01 = jnp.sum(lq[0:1, :] * lq[1:2, :], axis=1, keepdims=True)
    s23 = jnp.sum(lq[2:3, :] * lq[3:4, :], axis=1, keepdims=True)
    return jnp.exp(s01) - jnp.exp(s23) + lam_init


def _sub_norm(o, w_row, lam_init):
    o = o * lax.rsqrt(jnp.mean(o * o, axis=-1, keepdims=True) + LN_EPS)
    return o * w_row * (1.0 - lam_init)


def _pattn_kernel(qi_tab, ki_tab, q_ref, k_ref, v_ref, bias_ref, lq_ref, sw_ref, o_ref,
                  qt_sc, m_sc, acc_sc, *, tq, lam_init):
    t = pl.program_id(2)
    qi = qi_tab[t]
    ki = ki_tab[t]
    tk = k_ref.shape[0]

    @pl.when(ki == 0)
    def _():
        q = q_ref[...] * (DA_HEAD_DIM ** -0.5 * LOG2E)
        lane = lax.broadcasted_iota(jnp.int32, q.shape, 1)
        qt_sc[:, :tq] = jnp.where(lane < DA_HEAD_DIM, q, 0.0).T.astype(BF16)
        qt_sc[:, tq:] = jnp.where(lane >= DA_HEAD_DIM, q, 0.0).T.astype(BF16)
        m_sc[...] = jnp.full_like(m_sc, -jnp.inf)
        acc_sc[...] = jnp.zeros_like(acc_sc)

    def accumulate(with_bias, diagonal):
        kb = k_ref[...].astype(BF16)
        vt = jnp.concatenate([v_ref[...].T, jnp.ones((ACC_PAD, tk), F32)], axis=0).astype(BF16)
        n_chunks = 2 * tq // QCOLS
        cols = [slice(c * QCOLS, (c + 1) * QCOLS) for c in range(n_chunks)]
        keys = [((c * QCOLS) % tq + QCOLS) if diagonal else tk for c in range(n_chunks)]

        def scores(c):
            s = jnp.dot(kb[:keys[c]], qt_sc[:, cols[c]], preferred_element_type=F32)
            if with_bias:
                b0 = (c * QCOLS) % tq
                s = s + bias_ref[:keys[c], b0:b0 + QCOLS]
            return s

        def softmax(c, s):
            m_old = m_sc[:, cols[c]]
            m_new = jnp.maximum(m_old, jnp.max(s, axis=0, keepdims=True))
            m_sc[:, cols[c]] = m_new
            return jnp.exp2(s - m_new).astype(BF16), jnp.exp2(m_old - m_new)

        def update(c, p, alpha):
            acc_sc[:, cols[c]] = alpha * acc_sc[:, cols[c]] + jnp.dot(vt[:, :keys[c]], p,
                                                                      preferred_element_type=F32)

        s_q = {0: scores(0), 1: scores(1)}
        for c in range(n_chunks):
            p, alpha = softmax(c, s_q.pop(c))
            if c + 2 < n_chunks:
                s_q[c + 2] = scores(c + 2)
            update(c, p, alpha)

    @pl.when(qi == ki)
    def _():
        accumulate(True, True)

    @pl.when(qi - ki == 1)
    def _():
        accumulate(True, False)

    @pl.when(qi - ki > 1)
    def _():
        accumulate(False, False)

    @pl.when(ki == qi)
    def _():
        acc = acc_sc[...]
        o_all = acc[:DA_V_DIM] * (1.0 / acc[DA_V_DIM:DA_V_DIM + 1])
        lam = _lambda_scalar(lq_ref[...], lam_init)
        o = o_all[:, :tq] - lam * o_all[:, tq:]
        o = o * lax.rsqrt(jnp.mean(o * o, axis=0, keepdims=True) + LN_EPS)
        o = o * (sw_ref[...] * (1.0 - lam_init))
        o_ref[...] = o.T.astype(o_ref.dtype)


def _prompt_attn(proj, bias_tiles, lam_qk, subln_col, layer, lam_init, batch, seq, tq):
    nq = seq // tq
    tri = [(qi, ki) for qi in range(nq) for ki in range(qi + 1)]
    qi_tab = jnp.asarray([a for a, _ in tri], jnp.int32)
    ki_tab = jnp.asarray([b for _, b in tri], jnp.int32)
    hq, hk, hv = 0, ATT_W // DA_V_DIM, 2 * ATT_W // DA_V_DIM

    def bias_idx(b, h, t, qt, kt):
        return (h, jnp.minimum(qt[t] - kt[t], 1), 0, 0)

    grid_spec = pltpu.PrefetchScalarGridSpec(
        num_scalar_prefetch=2,
        grid=(batch, DA_HEADS, len(tri)),
        in_specs=[
            pl.BlockSpec((tq, DA_V_DIM), lambda b, h, t, qt, kt: (b * nq + qt[t], hq + h)),
            pl.BlockSpec((tq, DA_V_DIM), lambda b, h, t, qt, kt: (b * nq + kt[t], hk + h)),
            pl.BlockSpec((tq, DA_V_DIM), lambda b, h, t, qt, kt: (b * nq + kt[t], hv + h)),
            pl.BlockSpec((None, None, tq, tq), bias_idx),
            pl.BlockSpec((None, 4, DA_HEAD_DIM), lambda b, h, t, qt, kt: (layer, 0, 0)),
            pl.BlockSpec((None, DA_V_DIM, 1), lambda b, h, t, qt, kt: (layer, 0, 0)),
        ],
        out_specs=pl.BlockSpec((tq, DA_V_DIM), lambda b, h, t, qt, kt: (b * nq + qt[t], h)),
        scratch_shapes=[pltpu.VMEM((DA_V_DIM, 2 * tq), BF16), pltpu.VMEM((1, 2 * tq), F32),
                        pltpu.VMEM((DA_V_DIM + ACC_PAD, 2 * tq), F32)],
    )
    return pl.pallas_call(
        functools.partial(_pattn_kernel, tq=tq, lam_init=lam_init),
        grid_spec=grid_spec,
        out_shape=jax.ShapeDtypeStruct((batch * seq, ATT_W), BF16),
        compiler_params=_cparams(("parallel", "parallel", "arbitrary")),
        name="prompt_attn",
    )(qi_tab, ki_tab, proj, proj, proj, bias_tiles, lam_qk, subln_col)


def _skew_kernel(y_ref, o_ref, *, tq, rows):
    for r0 in range(0, tq, rows):
        x = jnp.broadcast_to(y_ref[...], (rows, 2 * tq))
        o_ref[r0:r0 + rows, :] = pltpu.roll(x, r0, 1, stride=1, stride_axis=0)[:, :tq]


def _prompt_bias_tiles(rel_bias, tq):
    assert tq >= MAX_DISTANCE and np.all(_t5_bucket_np(np.arange(MAX_DISTANCE, 1 << 16)) == N_BUCKETS - 1)
    span = 2 * tq
    d_row = np.zeros(span, np.int64)
    d_row[:tq] = np.arange(tq)
    d_row[span - np.arange(1, tq)] = -np.arange(1, tq)
    far = rel_bias[N_BUCKETS - 1]
    rows = []
    for delta in range(2):
        dist = d_row + delta * tq
        y = (rel_bias[_t5_bucket_np(dist)] - far) * LOG2E
        rows.append(jnp.where(jnp.asarray(dist >= 0)[:, None], y, NEG_BIG).T)
    y = jnp.stack(rows, axis=1).reshape(DA_HEADS, 2, 1, span)
    return pl.pallas_call(
        functools.partial(_skew_kernel, tq=tq, rows=256),
        grid=(DA_HEADS, 2),
        in_specs=[pl.BlockSpec((None, None, 1, span), lambda h, d: (h, d, 0, 0))],
        out_specs=pl.BlockSpec((None, None, tq, tq), lambda h, d: (h, d, 0, 0)),
        out_shape=jax.ShapeDtypeStruct((DA_HEADS, 2, tq, tq), F32),
        compiler_params=_cparams(("parallel", "parallel")),
        name="bias_skew",
    )(y)


def _page_rows(page_ref):
    return jnp.concatenate([page_ref[pl.ds(h, PAGE_SIZE, stride=DA_HEADS), :] for h in range(DA_HEADS)],
                           axis=1).astype(BF16)


def _sattn_kernel(pt_ref, w_ref, kn_ref, vn_ref, bl_ref, bn_ref, bf_ref, lq_ref, sw_ref, *rest,
                  n_group, n_pages, n_tok, lam_init):
    k_refs = rest[:n_group]
    v_refs = rest[n_group:2 * n_group]
    o_ref = rest[2 * n_group]
    m_sc, l_sc, acc_sc = rest[2 * n_group + 1:]
    step = pl.program_id(1)
    n_steps = pl.num_programs(1)
    hcols = 2 * n_tok
    w = w_ref[...]

    @pl.when(step == 0)
    def _():
        m_sc[...] = jnp.full_like(m_sc, -jnp.inf)
        l_sc[...] = jnp.zeros_like(l_sc)
        acc_sc[...] = jnp.zeros_like(acc_sc)

    def online_update(s, v_rows):
        m_old = m_sc[...]
        m_new = jnp.maximum(m_old, jnp.max(s, axis=0, keepdims=True))
        p = jnp.exp2(s - m_new)
        alpha = jnp.exp2(m_old - m_new)
        l_sc[...] = alpha * l_sc[...] + jnp.sum(p, axis=0, keepdims=True)
        m_sc[...] = m_new
        pt = p.T.astype(BF16)
        alpha_col = jnp.broadcast_to(alpha, (8, alpha.shape[1])).T[:, 0:1]
        for h in range(DA_HEADS):
            rows = slice(h * hcols, (h + 1) * hcols)
            d = jnp.dot(pt[rows], v_rows[:, h * DA_V_DIM:(h + 1) * DA_V_DIM], preferred_element_type=F32)
            acc_sc[h] = alpha_col[rows] * acc_sc[h] + d

    k_rows = jnp.concatenate([_page_rows(k_refs[g]) for g in range(n_group)], axis=0)
    s = jnp.dot(k_rows, w, preferred_element_type=F32)
    tail = jnp.where(step == n_steps - 1, bl_ref[...], bf_ref[...])
    s = s + jnp.concatenate([bf_ref[...]] * (n_group - 1) + [tail], axis=0)
    online_update(s, jnp.concatenate([_page_rows(v_refs[g]) for g in range(n_group)], axis=0))

    @pl.when(step == n_steps - 1)
    def _():
        pad = jnp.zeros((2 * n_tok - n_tok, kn_ref.shape[1]), F32)
        kn = jnp.concatenate([kn_ref[...], pad], axis=0).astype(BF16)
        vn = jnp.concatenate([vn_ref[...], pad], axis=0).astype(BF16)
        online_update(jnp.dot(kn, w, preferred_element_type=F32) + bn_ref[...], vn)
        lam = _lambda_scalar(lq_ref[...], lam_init)
        inv_l = jnp.broadcast_to(1.0 / l_sc[...], (8, l_sc.shape[1])).T[:, 0:1]
        for h in range(DA_HEADS):
            o = acc_sc[h] * inv_l[h * hcols:(h + 1) * hcols]
            o = o[:n_tok] - lam * o[n_tok:]
            o_ref[:, h * DA_V_DIM:(h + 1) * DA_V_DIM] = _sub_norm(o, sw_ref[...], lam_init).astype(o_ref.dtype)


def _sample_attn(proj, cache_k, cache_v, page_table, rel_bias, lam_qk, subln_w, layer, lam_init,
                 dec_batch, n_tok, n_group):
    n_pages = page_table.shape[1]
    assert n_pages % n_group == 0 and 2 * DA_HEADS * n_tok == DA_V_DIM
    past = n_pages * PAGE_SIZE
    n_steps = n_pages // n_group
    ncol = 2 * DA_HEADS * n_tok
    page_rows = PAGE_SIZE * DA_HEADS

    q = proj[:, :ATT_W].reshape(dec_batch, n_tok, DA_HEADS, 2, DA_HEAD_DIM) * (DA_HEAD_DIM ** -0.5 * LOG2E)
    w = jnp.einsum('bthjd,hH,jJ->bhjdHJt', q, jnp.eye(DA_HEADS, dtype=F32), jnp.eye(2, dtype=F32))
    w = w.reshape(dec_batch, ATT_W, ncol).astype(BF16)

    tok = np.arange(n_tok)

    def col_bias(dist, pad_rows=0):
        b = rel_bias[_t5_bucket_np(dist)] * LOG2E
        b = jnp.where(jnp.asarray(dist >= 0)[..., None], b, NEG_BIG)
        b = jnp.transpose(b, (0, 2, 1))[:, :, None, :]
        b = jnp.broadcast_to(b, (dist.shape[0], DA_HEADS, 2, n_tok)).reshape(dist.shape[0], ncol)
        if pad_rows:
            b = jnp.concatenate([b, jnp.full((pad_rows, ncol), NEG_BIG, F32)], axis=0)
        return b

    assert np.all(_t5_bucket_np(np.arange(PAGE_SIZE + 1, past + n_tok + 1)) == N_BUCKETS - 1)
    bias_last = col_bias((past + tok)[None, :] - (past - PAGE_SIZE + np.arange(PAGE_SIZE))[:, None])
    bias_new = col_bias(tok[None, :] - tok[:, None], pad_rows=n_tok)
    bias_far = col_bias(np.full((PAGE_SIZE, n_tok), PAGE_SIZE + 1))

    page_block = (None, None, page_rows, DA_V_DIM)
    cache_k = cache_k.reshape(cache_k.shape[0], cache_k.shape[1], page_rows, DA_V_DIM)
    cache_v = cache_v.reshape(cache_v.shape[0], cache_v.shape[1], page_rows, DA_V_DIM)

    def page_idx(g):
        return lambda b, s, pt: (layer, pt[b, s * n_group + g], 0, 0)

    const2 = lambda b, s, pt: (0, 0)
    grid_spec = pltpu.PrefetchScalarGridSpec(
        num_scalar_prefetch=1,
        grid=(dec_batch, n_steps),
        in_specs=[
            pl.BlockSpec((None, ATT_W, ncol), lambda b, s, pt: (b, 0, 0)),
            pl.BlockSpec((n_tok, ATT_W), lambda b, s, pt: (b, 1)),
            pl.BlockSpec((n_tok, ATT_W), lambda b, s, pt: (b, 2)),
            pl.BlockSpec((PAGE_SIZE, ncol), const2),
            pl.BlockSpec((2 * n_tok, ncol), const2),
            pl.BlockSpec((PAGE_SIZE, ncol), const2),
            pl.BlockSpec((None, 4, DA_HEAD_DIM), lambda b, s, pt: (layer, 0, 0)),
            pl.BlockSpec((None, 1, DA_V_DIM), lambda b, s, pt: (layer, 0, 0)),
        ] + [pl.BlockSpec(page_block, page_idx(g)) for g in range(n_group)] * 2,
        out_specs=pl.BlockSpec((n_tok, ATT_W), lambda b, s, pt: (b, 0)),
        scratch_shapes=[pltpu.VMEM((1, ncol), F32), pltpu.VMEM((1, ncol), F32),
                        pltpu.VMEM((DA_HEADS, 2 * n_tok, DA_V_DIM), F32)],
    )
    return pl.pallas_call(
        functools.partial(_sattn_kernel, n_group=n_group, n_pages=n_pages, n_tok=n_tok,
                          lam_init=lam_init),
        grid_spec=grid_spec,
        out_shape=jax.ShapeDtypeStruct((dec_batch * n_tok, ATT_W), BF16),
        compiler_params=_cparams(("parallel", "arbitrary")),
        name="sample_attn",
    )(page_table, w, proj, proj, bias_last, bias_new, bias_far, lam_qk, subln_w,
      *([cache_k] * n_group), *([cache_v] * n_group))


def _mlstm_kernel(q_ref, k_ref, v_ref, o_gate_ref, g_ref, nw_ref, c0_ref, n0_ref, m0_ref,
                  h_ref, c_ref, n_ref, m_ref, *, rows, chunk):
    c_idx = pl.program_id(1)

    @pl.when(c_idx == 0)
    def _():
        c_ref[...] = c0_ref[...]
        n_ref[...] = n0_ref[...]
        m_ref[...] = m0_ref[...]

    row = lax.broadcasted_iota(jnp.int32, (chunk, chunk), 0)
    col = lax.broadcasted_iota(jnp.int32, (chunk, chunk), 1)
    tril = row >= col
    eye = row == col

    def padded(x):
        if rows == chunk:
            return x
        return jnp.concatenate([x, jnp.zeros((chunk - rows, x.shape[1]), x.dtype)], axis=0)

    for h in range(ML_HEADS):
        sl = slice(h * ML_DQK, (h + 1) * ML_DQK)
        q = padded(q_ref[:, sl])
        k = padded(k_ref[:, sl]) * (ML_DQK ** -0.5)
        v = padded(v_ref[:, sl])
        ig = g_ref[h:h + 1, :]
        gf = g_ref[ML_HEADS + h:ML_HEADS + h + 1, :]
        lf = jnp.minimum(gf, 0.0) - jnp.log1p(jnp.exp(-jnp.abs(gf)))
        c_state = c_ref[h]
        n_state = n_ref[h]
        m_state = m_ref[:, h:h + 1]

        b_col = jnp.sum(jnp.where(tril, lf, 0.0), axis=1, keepdims=True)
        b_row = jnp.sum(jnp.where(eye, b_col, 0.0), axis=0, keepdims=True)
        logw = jnp.where(tril, b_col - b_row + ig, NEG_BIG)
        m_inter = b_col + m_state
        m_t = jnp.maximum(jnp.max(logw, axis=1, keepdims=True), m_inter)
        inter = jnp.exp(m_inter - m_t)
        qb = q.astype(BF16)
        kb = k.astype(BF16)
        s = lax.dot_general(qb, kb, (((1,), (1,)), ((), ())), preferred_element_type=F32)
        s = s * jnp.exp(logw - m_t)
        num = jnp.dot(s.astype(BF16), v.astype(BF16), preferred_element_type=F32)
        num = num + inter * lax.dot_general(qb, c_state.astype(BF16), (((1,), (1,)), ((), ())),
                                            preferred_element_type=F32)
        den = jnp.sum(s, axis=1, keepdims=True) + inter * jnp.sum(q * n_state, axis=1, keepdims=True)
        hh = num / jnp.maximum(jnp.abs(den), jnp.exp(-m_t))

        b_last = jnp.sum(lf, axis=1, keepdims=True)
        logw_end = b_last - b_row + ig
        m_new = jnp.maximum(b_last + m_state, jnp.max(logw_end, axis=1, keepdims=True))
        w_end = jnp.exp(logw_end - m_new)
        decay = jnp.exp(b_last + m_state - m_new)
        w_col = jnp.sum(jnp.where(eye, w_end, 0.0), axis=1, keepdims=True)
        vw = (v * w_col).astype(BF16)
        c_ref[h] = decay * c_state + lax.dot_general(vw, kb, (((0,), (0,)), ((), ())),
                                                     preferred_element_type=F32)
        n_ref[h] = decay * n_state + jnp.sum(k * w_col, axis=0, keepdims=True)
        m_ref[:, h:h + 1] = m_new

        hh = hh[:rows]
        mc = hh - jnp.mean(hh, axis=1, keepdims=True)
        y = mc * lax.rsqrt(jnp.mean(mc * mc, axis=1, keepdims=True) + LN_EPS) * nw_ref[:, sl]
        h_ref[:, sl] = (jax.nn.sigmoid(o_gate_ref[:, sl]) * y).astype(h_ref.dtype)


def _mlstm(proj, gates_t, norm_w, c0, n0, m0, layer, batch, seq, rows_per_block):
    nc = seq // rows_per_block
    chunk = gates_t.shape[2] // nc
    cq, ck, cv, co = (3 * ATT_W // ML_W, 3 * ATT_W // ML_W + 1, 3 * ATT_W // ML_W + 2,
                      3 * ATT_W // ML_W + 3)
    rows = lambda col: pl.BlockSpec((rows_per_block, ML_W), lambda b, c: (b * nc + c, col))
    state_c = pl.BlockSpec((None, ML_HEADS, ML_DV, ML_DQK), lambda b, c: (b, 0, 0, 0))
    state_n = pl.BlockSpec((None, ML_HEADS, 1, ML_DQK), lambda b, c: (b, 0, 0, 0))
    state_m = pl.BlockSpec((None, 1, ML_HEADS), lambda b, c: (b, 0, 0))
    return pl.pallas_call(
        functools.partial(_mlstm_kernel, rows=rows_per_block, chunk=chunk),
        grid=(batch, nc),
        in_specs=[rows(cq), rows(ck), rows(cv), rows(co),
                  pl.BlockSpec((None, 2 * ML_HEADS, chunk), lambda b, c: (b, 0, c)),
                  pl.BlockSpec((None, 1, ML_W), lambda b, c: (layer, 0, 0)),
                  state_c, state_n, state_m],
        out_specs=[pl.BlockSpec((rows_per_block, ML_W), lambda b, c: (b * nc + c, 0)),
                   state_c, state_n, state_m],
        out_shape=[jax.ShapeDtypeStruct((batch * seq, ML_W), BF16),
                   jax.ShapeDtypeStruct((batch, ML_HEADS, ML_DV, ML_DQK), F32),
                   jax.ShapeDtypeStruct((batch, ML_HEADS, 1, ML_DQK), F32),
                   jax.ShapeDtypeStruct((batch, 1, ML_HEADS), F32)],
        compiler_params=_cparams(("parallel", "arbitrary")),
        name="mlstm",
    )(proj, proj, proj, proj, gates_t, norm_w, c0, n0, m0)


def _merge_kernel(ao_ref, mh_ref, wa_ref, wm_ref, ga_ref, gm_ref, ba_ref, bm_ref, o_ref):
    ta = jnp.dot(ao_ref[...], wa_ref[...], preferred_element_type=F32)
    tmm = jnp.dot(mh_ref[...], wm_ref[...], preferred_element_type=F32)
    ga = jax.nn.sigmoid(ga_ref[...] + ba_ref[...])
    gm = jax.nn.sigmoid(gm_ref[...] + bm_ref[...])
    o_ref[...] = (ga * ta + gm * tmm).astype(o_ref.dtype)


def _merge(ao, mh, proj, w_ba, w_bm, b_gate, layer, tm, tn):
    m = ao.shape[0]
    nd = D_MODEL // tn
    ga0 = COL_GATE // tn
    return pl.pallas_call(
        _merge_kernel,
        grid=(m // tm, nd),
        in_specs=[
            pl.BlockSpec((tm, ATT_W), lambda i, j: (i, 0)),
            pl.BlockSpec((tm, ML_W), lambda i, j: (i, 0)),
            pl.BlockSpec((None, ATT_W, tn), lambda i, j: (layer, 0, j)),
            pl.BlockSpec((None, ML_W, tn), lambda i, j: (layer, 0, j)),
            pl.BlockSpec((tm, tn), lambda i, j: (i, ga0 + j)),
            pl.BlockSpec((tm, tn), lambda i, j: (i, ga0 + nd + j)),
            pl.BlockSpec((None, 1, tn), lambda i, j: (layer, 0, j)),
            pl.BlockSpec((None, 1, tn), lambda i, j: (layer, 0, nd + j)),
        ],
        out_specs=pl.BlockSpec((tm, tn), lambda i, j: (i, j)),
        out_shape=jax.ShapeDtypeStruct((m, D_MODEL), BF16),
        compiler_params=_cparams(("parallel", "parallel")),
        name="merge",
    )(ao, mh, w_ba, w_bm, proj, proj, b_gate, b_gate)


def _out_kernel(x_ref, mg_ref, w_ref, g_ref, b_ref, o_ref, ob_ref):
    y = DN_ALPHA * x_ref[...] + jnp.dot(mg_ref[...], w_ref[...], preferred_element_type=F32)
    o = _layer_norm_rows(y, g_ref[...], b_ref[...])
    o_ref[...] = o
    ob_ref[...] = o.astype(BF16)


def _out_proj_ln(x, merged, w_out, ln_g, ln_b, layer, tm):
    m = x.shape[0]
    return pl.pallas_call(
        _out_kernel,
        grid=(m // tm,),
        in_specs=[
            pl.BlockSpec((tm, D_MODEL), lambda i: (i, 0)),
            pl.BlockSpec((tm, D_MODEL), lambda i: (i, 0)),
            pl.BlockSpec((None, D_MODEL, D_MODEL), lambda i: (layer, 0, 0)),
            pl.BlockSpec((None, None, 1, D_MODEL), lambda i: (layer, 1, 0, 0)),
            pl.BlockSpec((None, None, 1, D_MODEL), lambda i: (layer, 1, 0, 0)),
        ],
        out_specs=[pl.BlockSpec((tm, D_MODEL), lambda i: (i, 0)),
                   pl.BlockSpec((tm, D_MODEL), lambda i: (i, 0))],
        out_shape=[jax.ShapeDtypeStruct((m, D_MODEL), F32),
                   jax.ShapeDtypeStruct((m, D_MODEL), BF16)],
        compiler_params=_cparams(("parallel",)),
        name="out_proj_ln",
    )(x, merged, w_out, ln_g, ln_b)


def _gates_transposed(pif, b_if, batch, seq, pad_to):
    gif = pif[:, :N_IF] + b_if
    gt = jnp.transpose(gif.reshape(batch, seq, N_IF), (0, 2, 1))
    if pad_to > seq:
        pad = jnp.concatenate([jnp.full((batch, ML_HEADS, pad_to - seq), NEG_BIG, F32),
                               jnp.full((batch, ML_HEADS, pad_to - seq), 1e4, F32)], axis=1)
        gt = jnp.concatenate([gt, pad], axis=2)
    return gt


def kernel(x_prompt, x_sample, cache_k, cache_v, page_table, state_C, state_n, state_m, rel_bias,
           w_in, b_gate, b_if, lam_qk, subln_w, mlstm_norm_w, w_branch_attn, w_branch_mlstm,
           w_out, ffn_up, ffn_down, ln_g, ln_b):
    batch, seq, _ = x_prompt.shape
    dec_batch, dec_seq, _ = x_sample.shape
    mp, ms = batch * seq, dec_batch * dec_seq
    tm_p, tm_s = 512, ms
    tq = 1024
    prompt_chunk = 256

    w_cat = jnp.concatenate([w_in[:, :, :N_MAIN], w_in[:, :, N_MAIN + N_IF:]], axis=2).astype(BF16)
    w_if = w_in[:, :, N_MAIN:N_MAIN + N_IF_PAD].astype(BF16)
    wba_b = w_branch_attn.astype(BF16)
    wbm_b = w_branch_mlstm.astype(BF16)
    wout_b = w_out.astype(BF16)
    ln_g = ln_g.reshape(DEPTH, 3, 1, D_MODEL)
    ln_b = ln_b.reshape(DEPTH, 3, 1, D_MODEL)
    b_gate3 = b_gate.reshape(DEPTH, 1, N_GATE)
    subln3 = subln_w.reshape(DEPTH, 1, DA_V_DIM)
    subln_col = subln_w.reshape(DEPTH, DA_V_DIM, 1)
    normw3 = mlstm_norm_w.reshape(DEPTH, 1, ML_W)
    bias_tiles = _prompt_bias_tiles(rel_bias, tq)

    zero_c = jnp.zeros((batch, ML_HEADS, ML_DV, ML_DQK), F32)
    zero_n = jnp.zeros((batch, ML_HEADS, 1, ML_DQK), F32)
    zero_m = jnp.zeros((batch, 1, ML_HEADS), F32)

    xp = x_prompt.reshape(mp, D_MODEL)
    xs = x_sample.reshape(ms, D_MODEL)
    xpb = xp.astype(BF16)
    xsb = xs.astype(BF16)

    outs = {name: [] for name in ('cp', 'np', 'mp', 'cs', 'ns', 'ms')}
    kv_p = kv_s = None
    for l in range(DEPTH):
        lam_init = 0.8 - 0.6 * math.exp(-0.3 * l)

        def rowwise_pre(x, xb, kv, ffn_w, tm):
            x1, x1b, ffn_w = _ffn_ln(x, xb, ffn_w or (ffn_up, ffn_down), ln_g, ln_b, l, 0, 0, tm, 512)
            proj, pif, kv = _in_proj(x1b, w_cat, w_if, kv, l, min(2 * tm, x1b.shape[0]))
            return x1, proj, pif, kv, ffn_w

        def rowwise_post(x1, proj, ao, mh, ffn_w, tm):
            merged = _merge(ao, mh, proj, wba_b, wbm_b, b_gate3, l, tm, 512)
            x2, x2b = _out_proj_ln(x1, merged, wout_b, ln_g, ln_b, l, min(tm, 256))
            return _ffn_ln(x2, x2b, ffn_w or (ffn_up, ffn_down), ln_g, ln_b, l, 1, 2, tm, 512)

        s_x1, s_proj, pif, kv_s, ffn_w1 = rowwise_pre(xs, xsb, kv_s, None, tm_s)
        s_ao = _sample_attn(s_proj, cache_k, cache_v, page_table, rel_bias, lam_qk, subln3, l, lam_init,
                            dec_batch, dec_seq, 16)
        gt = _gates_transposed(pif, b_if[l], dec_batch, dec_seq, 128)
        s_mh, c_new, n_new, m_new = _mlstm(s_proj, gt, normw3, state_C[l],
                                           state_n[l].reshape(dec_batch, ML_HEADS, 1, ML_DQK),
                                           state_m[l].reshape(dec_batch, 1, ML_HEADS), l, dec_batch,
                                           dec_seq, dec_seq)
        outs['cs'].append(c_new)
        outs['ns'].append(n_new.reshape(dec_batch, ML_HEADS, ML_DQK))
        outs['ms'].append(m_new.reshape(dec_batch, ML_HEADS))

        x1, proj, pif, kv_p, _ = rowwise_pre(xp, xpb, kv_p, ffn_w1, tm_p)
        ao = _prompt_attn(proj, bias_tiles, lam_qk, subln_col, l, lam_init, batch, seq, tq)
        gt = _gates_transposed(pif, b_if[l], batch, seq, seq)
        mh, c_new, n_new, m_new = _mlstm(proj, gt, normw3, zero_c, zero_n, zero_m, l, batch, seq,
                                         prompt_chunk)
        outs['cp'].append(c_new)
        outs['np'].append(n_new.reshape(batch, ML_HEADS, ML_DQK))
        outs['mp'].append(m_new.reshape(batch, ML_HEADS))

        xs, xsb, ffn_w2 = rowwise_post(s_x1, s_proj, s_ao, s_mh, None, tm_s)
        xp, xpb, _ = rowwise_post(x1, proj, ao, mh, ffn_w2, tm_p)

    st = lambda name: jnp.stack(outs[name])
    kv_shape_p = (DEPTH, batch, seq, DA_HEADS, DA_V_DIM)
    kv_shape_s = (DEPTH, dec_batch, dec_seq, DA_HEADS, DA_V_DIM)
    return (xp.reshape(batch, seq, D_MODEL), xs.reshape(dec_batch, dec_seq, D_MODEL),
            kv_p[0].reshape(kv_shape_p), kv_p[1].reshape(kv_shape_p),
            kv_s[0].reshape(kv_shape_s), kv_s[1].reshape(kv_shape_s), st('cp'), st('np'), st('mp'),
            st('cs'), st('ns'), st('ms'))
```

```python
import functools
import math

import numpy as np
import jax
import jax.numpy as jnp
from jax import lax
from jax.experimental import pallas as pl
from jax.experimental.pallas import tpu as pltpu

F32 = jnp.float32
BF16 = jnp.bfloat16

D_MODEL = 2048
DEPTH = 4
PAGE_SIZE = 128
DA_HEADS = 8
DA_HEAD_DIM = 64
DA_V_DIM = 2 * DA_HEAD_DIM
ML_HEADS = 4
ML_DQK = 256
ML_DV = 256
D_FF = 5632
N_BUCKETS = 32
MAX_DISTANCE = 128
LN_EPS = 1e-5
DN_ALPHA = (2.0 * DEPTH) ** 0.25

ATT_W = DA_HEADS * DA_V_DIM
ML_W = ML_HEADS * ML_DV
N_MAIN = 3 * ATT_W + 4 * ML_W
N_GATE = 2 * D_MODEL
N_IF = 2 * ML_HEADS
N_IF_PAD = 128
N_PROJ = N_MAIN + N_GATE
PROJ_TN = 1024
KV_COL_BLOCK = ATT_W // PROJ_TN
COL_GATE = N_MAIN

NEG_BIG = -1e30
LOG2E = math.log2(math.e)
QCOLS = 512
ACC_PAD = 16
VMEM_LIMIT = 48 * 1024 * 1024


def _cparams(sem):
    return pltpu.CompilerParams(dimension_semantics=sem, vmem_limit_bytes=VMEM_LIMIT)


def _layer_norm_rows(y, g, b):
    mu = jnp.mean(y, axis=-1, keepdims=True)
    yc = y - mu
    var = jnp.mean(yc * yc, axis=-1, keepdims=True)
    return yc * lax.rsqrt(var + LN_EPS) * g + b


def _ffn_kernel(x_ref, xb_ref, wa_ref, wb_ref, wd_ref, g_ref, b_ref, o_ref, ob_ref, *rest):
    acc_ref = rest[-1]
    j = pl.program_id(1)

    @pl.when(j == 0)
    def _():
        acc_ref[...] = jnp.zeros_like(acc_ref)

    wa, wb, wd = wa_ref[...].astype(BF16), wb_ref[...].astype(BF16), wd_ref[...].astype(BF16)
    for w_out_ref, w in zip(rest[:-1], (wa, wb, wd)):
        w_out_ref[...] = w
    xb = xb_ref[...]
    a = jnp.dot(xb, wa, preferred_element_type=F32)
    b = jnp.dot(xb, wb, preferred_element_type=F32)
    h = (a * jax.nn.sigmoid(a) * b).astype(BF16)
    acc_ref[...] += jnp.dot(h, wd, preferred_element_type=F32)

    @pl.when(j == pl.num_programs(1) - 1)
    def _():
        y = DN_ALPHA * x_ref[...] + 0.5 * acc_ref[...]
        o = _layer_norm_rows(y, g_ref[...], b_ref[...])
        o_ref[...] = o
        ob_ref[...] = o.astype(BF16)


def _ffn_ln(x, xb, weights, ln_g, ln_b, layer, which, ln_idx, tm, tf):
    m = x.shape[0]
    nf = D_FF // tf
    assert m == tm or len(weights) == 3, "bf16 weight copies need a single row tile (each tile written once)"
    row_spec = pl.BlockSpec((tm, D_MODEL), lambda i, j: (i, 0))
    ln_spec = pl.BlockSpec((None, None, 1, D_MODEL), lambda i, j: (layer, ln_idx, 0, 0))
    up_spec = pl.BlockSpec((D_MODEL, tf), lambda i, j: (0, j))
    down_spec = pl.BlockSpec((tf, D_MODEL), lambda i, j: (j, 0))
    out_specs = [row_spec, row_spec]
    out_shape = [jax.ShapeDtypeStruct((m, D_MODEL), F32), jax.ShapeDtypeStruct((m, D_MODEL), BF16)]
    if len(weights) == 2:
        w_up, w_down = weights
        w_args = (w_up, w_up, w_down)
        w_specs = [pl.BlockSpec((None, None, D_MODEL, tf), lambda i, j: (layer, which, 0, j)),
                   pl.BlockSpec((None, None, D_MODEL, tf), lambda i, j: (layer, which, 0, j + nf)),
                   pl.BlockSpec((None, None, tf, D_MODEL), lambda i, j: (layer, which, j, 0))]
        out_specs += [up_spec, up_spec, down_spec]
        out_shape += [jax.ShapeDtypeStruct((D_MODEL, D_FF), BF16)] * 2 + [jax.ShapeDtypeStruct((D_FF, D_MODEL), BF16)]
    else:
        w_args = tuple(weights)
        w_specs = [up_spec, up_spec, down_spec]
    outs = pl.pallas_call(
        _ffn_kernel,
        grid=(m // tm, nf),
        in_specs=[row_spec, row_spec] + w_specs + [ln_spec, ln_spec],
        out_specs=out_specs,
        out_shape=out_shape,
        scratch_shapes=[pltpu.VMEM((tm, D_MODEL), F32)],
        compiler_params=_cparams(("parallel", "arbitrary")),
        name="ffn_ln",
    )(x, xb, *w_args, ln_g, ln_b)
    return outs[0], outs[1], tuple(outs[2:])


def _proj_kernel(xb_ref, w_ref, wx_ref, *rest, from_f32):
    n_out = 5 if from_f32 else 3
    o_ref, oif_ref, kv_ref = rest[len(rest) - n_out:len(rest) - n_out + 3]
    j = pl.program_id(1)
    xb = xb_ref[...]
    if from_f32:
        wcat_ref, wif_ref = rest[-2:]
        first_gate_block = N_MAIN // PROJ_TN
        a = w_ref[...]
        r = pltpu.roll(a, PROJ_TN - N_IF, 1)
        rx = pltpu.roll(wx_ref[...], N_IF_PAD - N_IF, 1)
        lane = lax.broadcasted_iota(jnp.int32, rx.shape, 1)
        tail = jnp.where(lane < N_IF_PAD - N_IF, r[:, PROJ_TN - N_IF_PAD:], rx)
        shifted = jnp.concatenate([r[:, :PROJ_TN - N_IF_PAD], tail], axis=1)
        w = jnp.where(j >= first_gate_block, shifted, a).astype(BF16)
        wcat_ref[...] = w

        @pl.when(j == first_gate_block)
        def _():
            slab = a[:, :N_IF_PAD].astype(BF16)
            wif_ref[...] = slab
            oif_ref[...] = jnp.dot(xb, slab, preferred_element_type=F32)
    else:
        w = w_ref[...]

        @pl.when(j == 0)
        def _():
            oif_ref[...] = jnp.dot(xb, wx_ref[...], preferred_element_type=F32)

    o = jnp.dot(xb, w, preferred_element_type=F32)
    o_ref[...] = o

    @pl.when((j == KV_COL_BLOCK) | (j == KV_COL_BLOCK + 1))
    def _():
        kv_ref[...] = o


def _in_proj(xb, weights, kv_buf, layer, tm):
    m = xb.shape[0]
    from_f32 = not isinstance(weights, tuple)
    assert m == tm or not from_f32, "bf16 weight copies need a single row tile (each block written once)"
    if from_f32:
        slabs_per_block = PROJ_TN // N_IF_PAD
        w_specs = [pl.BlockSpec((None, D_MODEL, PROJ_TN), lambda i, j: (layer, 0, j)),
                   pl.BlockSpec((None, D_MODEL, N_IF_PAD), lambda i, j: (layer, 0, (j + 1) * slabs_per_block))]
        args = [xb, weights, weights]
    else:
        w_specs = [pl.BlockSpec((D_MODEL, PROJ_TN), lambda i, j: (0, j)),
                   pl.BlockSpec((D_MODEL, N_IF_PAD), lambda i, j: (0, 0))]
        args = [xb, *weights]
    in_specs = [pl.BlockSpec((tm, D_MODEL), lambda i, j: (i, 0))] + w_specs
    aliases = {}
    if kv_buf is not None:
        in_specs.append(pl.BlockSpec(memory_space=pl.ANY))
        args.append(kv_buf)
        aliases = {3: 2}
    out_specs = [pl.BlockSpec((tm, PROJ_TN), lambda i, j: (i, j)),
                 pl.BlockSpec((tm, N_IF_PAD), lambda i, j: (i, 0)),
                 pl.BlockSpec((None, None, tm, ATT_W),
                              lambda i, j: (jnp.clip(j - KV_COL_BLOCK, 0, 1), layer, i, 0))]
    out_shape = [jax.ShapeDtypeStruct((m, N_PROJ), F32),
                 jax.ShapeDtypeStruct((m, N_IF_PAD), F32),
                 jax.ShapeDtypeStruct((2, DEPTH, m, ATT_W), F32)]
    if from_f32:
        out_specs += [pl.BlockSpec((D_MODEL, PROJ_TN), lambda i, j: (0, j)),
                      pl.BlockSpec((D_MODEL, N_IF_PAD), lambda i, j: (0, 0))]
        out_shape += [jax.ShapeDtypeStruct((D_MODEL, N_PROJ), BF16),
                      jax.ShapeDtypeStruct((D_MODEL, N_IF_PAD), BF16)]
    outs = pl.pallas_call(
        functools.partial(_proj_kernel, from_f32=from_f32),
        grid=(m // tm, N_PROJ // PROJ_TN),
        in_specs=in_specs,
        out_specs=out_specs,
        out_shape=out_shape,
        input_output_aliases=aliases,
        compiler_params=_cparams(("parallel", "arbitrary")),
        name="in_proj",
    )(*args)
    return outs[0], outs[1], outs[2], tuple(outs[3:])


def _t5_bucket_np(dist):
    n = np.maximum(dist, 0)
    max_exact = N_BUCKETS // 2
    nf = np.maximum(n, 1).astype(np.float32)
    large = max_exact + (np.log(nf / np.float32(max_exact)) / np.float32(math.log(MAX_DISTANCE / max_exact))
                         * np.float32(N_BUCKETS - max_exact)).astype(np.int32)
    large = np.minimum(large, N_BUCKETS - 1)
    return np.where(n < max_exact, n, large).astype(np.int32)


def _lambda_scalar(lq, lam_init):
    s01 = jnp.sum(lq[0:1, :] * lq[1:2, :], axis=1, keepdims=True)
    s23 = jnp.sum(lq[2:3, :] * lq[3:4, :], axis=1, keepdims=True)
    return jnp.exp(s01) - jnp.exp(s23) + lam_init


def _sub_norm(o, w_row, lam_init):
    o = o * lax.rsqrt(jnp.mean(o * o, axis=-1, keepdims=True) + LN_EPS)
    return o * w_row * (1.0 - lam_init)


def _pattn_kernel(qi_tab, ki_tab, q_ref, k_ref, v_ref, bias_ref, lq_ref, sw_ref, o_ref,
                  qt_sc, m_sc, acc_sc, *, tq, lam_init):
    t = pl.program_id(2)
    qi = qi_tab[t]
    ki = ki_tab[t]
    tk = k_ref.shape[0]

    @pl.when(ki == 0)
    def _():
        q = q_ref[...] * (DA_HEAD_DIM ** -0.5 * LOG2E)
        lane = lax.broadcasted_iota(jnp.int32, q.shape, 1)
        qt_sc[:, :tq] = jnp.where(lane < DA_HEAD_DIM, q, 0.0).T.astype(BF16)
        qt_sc[:, tq:] = jnp.where(lane >= DA_HEAD_DIM, q, 0.0).T.astype(BF16)
        m_sc[...] = jnp.full_like(m_sc, -jnp.inf)
        acc_sc[...] = jnp.zeros_like(acc_sc)

    def accumulate(with_bias, diagonal):
        kb = k_ref[...].astype(BF16)
        vt = jnp.concatenate([v_ref[...].T, jnp.ones((ACC_PAD, tk), F32)], axis=0).astype(BF16)
        n_chunks = 2 * tq // QCOLS
        cols = [slice(c * QCOLS, (c + 1) * QCOLS) for c in range(n_chunks)]
        keys = [((c * QCOLS) % tq + QCOLS) if diagonal else tk for c in range(n_chunks)]

        def scores(c):
            s = jnp.dot(kb[:keys[c]], qt_sc[:, cols[c]], preferred_element_type=F32)
            if with_bias:
                b0 = (c * QCOLS) % tq
                s = s + bias_ref[:keys[c], b0:b0 + QCOLS]
            return s

        def softmax(c, s):
            m_old = m_sc[:, cols[c]]
            m_new = jnp.maximum(m_old, jnp.max(s, axis=0, keepdims=True))
            m_sc[:, cols[c]] = m_new
            return jnp.exp2(s - m_new).astype(BF16), jnp.exp2(m_old - m_new)

        def update(c, p, alpha):
            acc_sc[:, cols[c]] = alpha * acc_sc[:, cols[c]] + jnp.dot(vt[:, :keys[c]], p,
                                                                      preferred_element_type=F32)

        s_q = {0: scores(0), 1: scores(1)}
        for c in range(n_chunks):
            p, alpha = softmax(c, s_q.pop(c))
            if c + 2 < n_chunks:
                s_q[c + 2] = scores(c + 2)
            update(c, p, alpha)

    @pl.when(qi == ki)
    def _():
        accumulate(True, True)

    @pl.when(qi - ki == 1)
    def _():
        accumulate(True, False)

    @pl.when(qi - ki > 1)
    def _():
        accumulate(False, False)

    @pl.when(ki == qi)
    def _():
        acc = acc_sc[...]
        o_all = acc[:DA_V_DIM] * (1.0 / acc[DA_V_DIM:DA_V_DIM + 1])
        lam = _lambda_scalar(lq_ref[...], lam_init)
        o = o_all[:, :tq] - lam * o_all[:, tq:]
        o = o * lax.rsqrt(jnp.mean(o * o, axis=0, keepdims=True) + LN_EPS)
        o = o * (sw_ref[...] * (1.0 - lam_init))
        o_ref[...] = o.T.astype(o_ref.dtype)


def _prompt_attn(proj, bias_tiles, lam_qk, subln_col, layer, lam_init, batch, seq, tq):
    nq = seq // tq
    tri = [(qi, ki) for qi in range(nq) for ki in range(qi + 1)]
    qi_tab = jnp.asarray([a for a, _ in tri], jnp.int32)
    ki_tab = jnp.asarray([b for _, b in tri], jnp.int32)
    hq, hk, hv = 0, ATT_W // DA_V_DIM, 2 * ATT_W // DA_V_DIM

    def bias_idx(b, h, t, qt, kt):
        return (h, jnp.minimum(qt[t] - kt[t], 1), 0, 0)

    grid_spec = pltpu.PrefetchScalarGridSpec(
        num_scalar_prefetch=2,
        grid=(batch, DA_HEADS, len(tri)),
        in_specs=[
            pl.BlockSpec((tq, DA_V_DIM), lambda b, h, t, qt, kt: (b * nq + qt[t], hq + h)),
            pl.BlockSpec((tq, DA_V_DIM), lambda b, h, t, qt, kt: (b * nq + kt[t], hk + h)),
            pl.BlockSpec((tq, DA_V_DIM), lambda b, h, t, qt, kt: (b * nq + kt[t], hv + h)),
            pl.BlockSpec((None, None, tq, tq), bias_idx),
            pl.BlockSpec((None, 4, DA_HEAD_DIM), lambda b, h, t, qt, kt: (layer, 0, 0)),
            pl.BlockSpec((None, DA_V_DIM, 1), lambda b, h, t, qt, kt: (layer, 0, 0)),
        ],
        out_specs=pl.BlockSpec((tq, DA_V_DIM), lambda b, h, t, qt, kt: (b * nq + qt[t], h)),
        scratch_shapes=[pltpu.VMEM((DA_V_DIM, 2 * tq), BF16), pltpu.VMEM((1, 2 * tq), F32),
                        pltpu.VMEM((DA_V_DIM + ACC_PAD, 2 * tq), F32)],
    )
    return pl.pallas_call(
        functools.partial(_pattn_kernel, tq=tq, lam_init=lam_init),
        grid_spec=grid_spec,
        out_shape=jax.ShapeDtypeStruct((batch * seq, ATT_W), BF16),
        compiler_params=_cparams(("parallel", "parallel", "arbitrary")),
        name="prompt_attn",
    )(qi_tab, ki_tab, proj, proj, proj, bias_tiles, lam_qk, subln_col)


def _skew_kernel(y_ref, o_ref, *, tq, rows):
    for r0 in range(0, tq, rows):
        x = jnp.broadcast_to(y_ref[...], (rows, 2 * tq))
        o_ref[r0:r0 + rows, :] = pltpu.roll(x, r0, 1, stride=1, stride_axis=0)[:, :tq]


def _prompt_bias_tiles(rel_bias, tq):
    assert tq >= MAX_DISTANCE and np.all(_t5_bucket_np(np.arange(MAX_DISTANCE, 1 << 16)) == N_BUCKETS - 1)
    span = 2 * tq
    d_row = np.zeros(span, np.int64)
    d_row[:tq] = np.arange(tq)
    d_row[span - np.arange(1, tq)] = -np.arange(1, tq)
    far = rel_bias[N_BUCKETS - 1]
    rows = []
    for delta in range(2):
        dist = d_row + delta * tq
        y = (rel_bias[_t5_bucket_np(dist)] - far) * LOG2E
        rows.append(jnp.where(jnp.asarray(dist >= 0)[:, None], y, NEG_BIG).T)
    y = jnp.stack(rows, axis=1).reshape(DA_HEADS, 2, 1, span)
    return pl.pallas_call(
        functools.partial(_skew_kernel, tq=tq, rows=256),
        grid=(DA_HEADS, 2),
        in_specs=[pl.BlockSpec((None, None, 1, span), lambda h, d: (h, d, 0, 0))],
        out_specs=pl.BlockSpec((None, None, tq, tq), lambda h, d: (h, d, 0, 0)),
        out_shape=jax.ShapeDtypeStruct((DA_HEADS, 2, tq, tq), F32),
        compiler_params=_cparams(("parallel", "parallel")),
        name="bias_skew",
    )(y)


def _page_rows(page_ref):
    return jnp.concatenate([page_ref[pl.ds(h, PAGE_SIZE, stride=DA_HEADS), :] for h in range(DA_HEADS)],
                           axis=1).astype(BF16)


def _sattn_kernel(pt_ref, w_ref, kn_ref, vn_ref, bl_ref, bn_ref, bf_ref, lq_ref, sw_ref, *rest,
                  n_group, n_pages, n_tok, lam_init):
    k_refs = rest[:n_group]
    v_refs = rest[n_group:2 * n_group]
    o_ref = rest[2 * n_group]
    m_sc, l_sc, acc_sc = rest[2 * n_group + 1:]
    step = pl.program_id(1)
    n_steps = pl.num_programs(1)
    hcols = 2 * n_tok
    w = w_ref[...]

    @pl.when(step == 0)
    def _():
        m_sc[...] = jnp.full_like(m_sc, -jnp.inf)
        l_sc[...] = jnp.zeros_like(l_sc)
        acc_sc[...] = jnp.zeros_like(acc_sc)

    def online_update(s, v_rows):
        m_old = m_sc[...]
        m_new = jnp.maximum(m_old, jnp.max(s, axis=0, keepdims=True))
        p = jnp.exp2(s - m_new)
        alpha = jnp.exp2(m_old - m_new)
        l_sc[...] = alpha * l_sc[...] + jnp.sum(p, axis=0, keepdims=True)
        m_sc[...] = m_new
        pt = p.T.astype(BF16)
        alpha_col = jnp.broadcast_to(alpha, (8, alpha.shape[1])).T[:, 0:1]
        for h in range(DA_HEADS):
            rows = slice(h * hcols, (h + 1) * hcols)
            d = jnp.dot(pt[rows], v_rows[:, h * DA_V_DIM:(h + 1) * DA_V_DIM], preferred_element_type=F32)
            acc_sc[h] = alpha_col[rows] * acc_sc[h] + d

    k_rows = jnp.concatenate([_page_rows(k_refs[g]) for g in range(n_group)], axis=0)
    s = jnp.dot(k_rows, w, preferred_element_type=F32)
    tail = jnp.where(step == n_steps - 1, bl_ref[...], bf_ref[...])
    s = s + jnp.concatenate([bf_ref[...]] * (n_group - 1) + [tail], axis=0)
    online_update(s, jnp.concatenate([_page_rows(v_refs[g]) for g in range(n_group)], axis=0))

    @pl.when(step == n_steps - 1)
    def _():
        pad = jnp.zeros((2 * n_tok - n_tok, kn_ref.shape[1]), F32)
        kn = jnp.concatenate([kn_ref[...], pad], axis=0).astype(BF16)
        vn = jnp.concatenate([vn_ref[...], pad], axis=0).astype(BF16)
        online_update(jnp.dot(kn, w, preferred_element_type=F32) + bn_ref[...], vn)
        lam = _lambda_scalar(lq_ref[...], lam_init)
        inv_l = jnp.broadcast_to(1.0 / l_sc[...], (8, l_sc.shape[1])).T[:, 0:1]
        for h in range(DA_HEADS):
            o = acc_sc[h] * inv_l[h * hcols:(h + 1) * hcols]
            o = o[:n_tok] - lam * o[n_tok:]
            o_ref[:, h * DA_V_DIM:(h + 1) * DA_V_DIM] = _sub_norm(o, sw_ref[...], lam_init).astype(o_ref.dtype)


def _sample_attn(proj, cache_k, cache_v, page_table, rel_bias, lam_qk, subln_w, layer, lam_init,
                 dec_batch, n_tok, n_group):
    n_pages = page_table.shape[1]
    assert n_pages % n_group == 0 and 2 * DA_HEADS * n_tok == DA_V_DIM
    past = n_pages * PAGE_SIZE
    n_steps = n_pages // n_group
    ncol = 2 * DA_HEADS * n_tok
    page_rows = PAGE_SIZE * DA_HEADS

    q = proj[:, :ATT_W].reshape(dec_batch, n_tok, DA_HEADS, 2, DA_HEAD_DIM) * (DA_HEAD_DIM ** -0.5 * LOG2E)
    w = jnp.einsum('bthjd,hH,jJ->bhjdHJt', q, jnp.eye(DA_HEADS, dtype=F32), jnp.eye(2, dtype=F32))
    w = w.reshape(dec_batch, ATT_W, ncol).astype(BF16)

    tok = np.arange(n_tok)

    def col_bias(dist, pad_rows=0):
        b = rel_bias[_t5_bucket_np(dist)] * LOG2E
        b = jnp.where(jnp.asarray(dist >= 0)[..., None], b, NEG_BIG)
        b = jnp.transpose(b, (0, 2, 1))[:, :, None, :]
        b = jnp.broadcast_to(b, (dist.shape[0], DA_HEADS, 2, n_tok)).reshape(dist.shape[0], ncol)
        if pad_rows:
            b = jnp.concatenate([b, jnp.full((pad_rows, ncol), NEG_BIG, F32)], axis=0)
        return b

    assert np.all(_t5_bucket_np(np.arange(PAGE_SIZE + 1, past + n_tok + 1)) == N_BUCKETS - 1)
    bias_last = col_bias((past + tok)[None, :] - (past - PAGE_SIZE + np.arange(PAGE_SIZE))[:, None])
    bias_new = col_bias(tok[None, :] - tok[:, None], pad_rows=n_tok)
    bias_far = col_bias(np.full((PAGE_SIZE, n_tok), PAGE_SIZE + 1))

    page_block = (None, None, page_rows, DA_V_DIM)
    cache_k = cache_k.reshape(cache_k.shape[0], cache_k.shape[1], page_rows, DA_V_DIM)
    cache_v = cache_v.reshape(cache_v.shape[0], cache_v.shape[1], page_rows, DA_V_DIM)

    def page_idx(g):
        return lambda b, s, pt: (layer, pt[b, s * n_group + g], 0, 0)

    const2 = lambda b, s, pt: (0, 0)
    grid_spec = pltpu.PrefetchScalarGridSpec(
        num_scalar_prefetch=1,
        grid=(dec_batch, n_steps),
        in_specs=[
            pl.BlockSpec((None, ATT_W, ncol), lambda b, s, pt: (b, 0, 0)),
            pl.BlockSpec((n_tok, ATT_W), lambda b, s, pt: (b, 1)),
            pl.BlockSpec((n_tok, ATT_W), lambda b, s, pt: (b, 2)),
            pl.BlockSpec((PAGE_SIZE, ncol), const2),
            pl.BlockSpec((2 * n_tok, ncol), const2),
            pl.BlockSpec((PAGE_SIZE, ncol), const2),
            pl.BlockSpec((None, 4, DA_HEAD_DIM), lambda b, s, pt: (layer, 0, 0)),
            pl.BlockSpec((None, 1, DA_V_DIM), lambda b, s, pt: (layer, 0, 0)),
        ] + [pl.BlockSpec(page_block, page_idx(g)) for g in range(n_group)] * 2,
        out_specs=pl.BlockSpec((n_tok, ATT_W), lambda b, s, pt: (b, 0)),
        scratch_shapes=[pltpu.VMEM((1, ncol), F32), pltpu.VMEM((1, ncol), F32),
                        pltpu.VMEM((DA_HEADS, 2 * n_tok, DA_V_DIM), F32)],
    )
    return pl.pallas_call(
        functools.partial(_sattn_kernel, n_group=n_group, n_pages=n_pages, n_tok=n_tok,
                          lam_init=lam_init),
        grid_spec=grid_spec,
        out_shape=jax.ShapeDtypeStruct((dec_batch * n_tok, ATT_W), BF16),
        compiler_params=_cparams(("parallel", "arbitrary")),
        name="sample_attn",
    )(page_table, w, proj, proj, bias_last, bias_new, bias_far, lam_qk, subln_w,
      *([cache_k] * n_group), *([cache_v] * n_group))


def _mlstm_kernel(q_ref, k_ref, v_ref, o_gate_ref, g_ref, nw_ref, c0_ref, n0_ref, m0_ref,
                  h_ref, c_ref, n_ref, m_ref, *, rows, chunk):
    c_idx = pl.program_id(1)

    @pl.when(c_idx == 0)
    def _():
        c_ref[...] = c0_ref[...]
        n_ref[...] = n0_ref[...]
        m_ref[...] = m0_ref[...]

    row = lax.broadcasted_iota(jnp.int32, (chunk, chunk), 0)
    col = lax.broadcasted_iota(jnp.int32, (chunk, chunk), 1)
    tril = row >= col
    eye = row == col

    def padded(x):
        if rows == chunk:
            return x
        return jnp.concatenate([x, jnp.zeros((chunk - rows, x.shape[1]), x.dtype)], axis=0)

    for h in range(ML_HEADS):
        sl = slice(h * ML_DQK, (h + 1) * ML_DQK)
        q = padded(q_ref[:, sl])
        k = padded(k_ref[:, sl]) * (ML_DQK ** -0.5)
        v = padded(v_ref[:, sl])
        ig = g_ref[h:h + 1, :]
        gf = g_ref[ML_HEADS + h:ML_HEADS + h + 1, :]
        lf = jnp.minimum(gf, 0.0) - jnp.log1p(jnp.exp(-jnp.abs(gf)))
        c_state = c_ref[h]
        n_state = n_ref[h]
        m_state = m_ref[:, h:h + 1]

        b_col = jnp.sum(jnp.where(tril, lf, 0.0), axis=1, keepdims=True)
        b_row = jnp.sum(jnp.where(eye, b_col, 0.0), axis=0, keepdims=True)
        logw = jnp.where(tril, b_col - b_row + ig, NEG_BIG)
        m_inter = b_col + m_state
        m_t = jnp.maximum(jnp.max(logw, axis=1, keepdims=True), m_inter)
        inter = jnp.exp(m_inter - m_t)
        qb = q.astype(BF16)
        kb = k.astype(BF16)
        s = lax.dot_general(qb, kb, (((1,), (1,)), ((), ())), preferred_element_type=F32)
        s = s * jnp.exp(logw - m_t)
        num = jnp.dot(s.astype(BF16), v.astype(BF16), preferred_element_type=F32)
        num = num + inter * lax.dot_general(qb, c_state.astype(BF16), (((1,), (1,)), ((), ())),
                                            preferred_element_type=F32)
        den = jnp.sum(s, axis=1, keepdims=True) + inter * jnp.sum(q * n_state, axis=1, keepdims=True)
        hh = num / jnp.maximum(jnp.abs(den), jnp.exp(-m_t))

        b_last = jnp.sum(lf, axis=1, keepdims=True)
        logw_end = b_last - b_row + ig
        m_new = jnp.maximum(b_last + m_state, jnp.max(logw_end, axis=1, keepdims=True))
        w_end = jnp.exp(logw_end - m_new)
        decay = jnp.exp(b_last + m_state - m_new)
        w_col = jnp.sum(jnp.where(eye, w_end, 0.0), axis=1, keepdims=True)
        vw = (v * w_col).astype(BF16)
        c_ref[h] = decay * c_state + lax.dot_general(vw, kb, (((0,), (0,)), ((), ())),
                                                     preferred_element_type=F32)
        n_ref[h] = decay * n_state + jnp.sum(k * w_col, axis=0, keepdims=True)
        m_ref[:, h:h + 1] = m_new

        hh = hh[:rows]
        mc = hh - jnp.mean(hh, axis=1, keepdims=True)
        y = mc * lax.rsqrt(jnp.mean(mc * mc, axis=1, keepdims=True) + LN_EPS) * nw_ref[:, sl]
        h_ref[:, sl] = (jax.nn.sigmoid(o_gate_ref[:, sl]) * y).astype(h_ref.dtype)


def _mlstm(proj, gates_t, norm_w, c0, n0, m0, layer, batch, seq, rows_per_block):
    nc = seq // rows_per_block
    chunk = gates_t.shape[2] // nc
    cq, ck, cv, co = (3 * ATT_W // ML_W, 3 * ATT_W // ML_W + 1, 3 * ATT_W // ML_W + 2,
                      3 * ATT_W // ML_W + 3)
    rows = lambda col: pl.BlockSpec((rows_per_block, ML_W), lambda b, c: (b * nc + c, col))
    state_c = pl.BlockSpec((None, ML_HEADS, ML_DV, ML_DQK), lambda b, c: (b, 0, 0, 0))
    state_n = pl.BlockSpec((None, ML_HEADS, 1, ML_DQK), lambda b, c: (b, 0, 0, 0))
    state_m = pl.BlockSpec((None, 1, ML_HEADS), lambda b, c: (b, 0, 0))
    return pl.pallas_call(
        functools.partial(_mlstm_kernel, rows=rows_per_block, chunk=chunk),
        grid=(batch, nc),
        in_specs=[rows(cq), rows(ck), rows(cv), rows(co),
                  pl.BlockSpec((None, 2 * ML_HEADS, chunk), lambda b, c: (b, 0, c)),
                  pl.BlockSpec((None, 1, ML_W), lambda b, c: (layer, 0, 0)),
                  state_c, state_n, state_m],
        out_specs=[pl.BlockSpec((rows_per_block, ML_W), lambda b, c: (b * nc + c, 0)),
                   state_c, state_n, state_m],
        out_shape=[jax.ShapeDtypeStruct((batch * seq, ML_W), BF16),
                   jax.ShapeDtypeStruct((batch, ML_HEADS, ML_DV, ML_DQK), F32),
                   jax.ShapeDtypeStruct((batch, ML_HEADS, 1, ML_DQK), F32),
                   jax.ShapeDtypeStruct((batch, 1, ML_HEADS), F32)],
        compiler_params=_cparams(("parallel", "arbitrary")),
        name="mlstm",
    )(proj, proj, proj, proj, gates_t, norm_w, c0, n0, m0)


def _merge_kernel(ao_ref, mh_ref, wa_ref, wm_ref, ga_ref, gm_ref, ba_ref, bm_ref, o_ref):
    ta = jnp.dot(ao_ref[...], wa_ref[...], preferred_element_type=F32)
    tmm = jnp.dot(mh_ref[...], wm_ref[...], preferred_element_type=F32)
    ga = jax.nn.sigmoid(ga_ref[...] + ba_ref[...])
    gm = jax.nn.sigmoid(gm_ref[...] + bm_ref[...])
    o_ref[...] = (ga * ta + gm * tmm).astype(o_ref.dtype)


def _merge(ao, mh, proj, w_ba, w_bm, b_gate, layer, tm, tn):
    m = ao.shape[0]
    nd = D_MODEL // tn
    ga0 = COL_GATE // tn
    return pl.pallas_call(
        _merge_kernel,
        grid=(m // tm, nd),
        in_specs=[
            pl.BlockSpec((tm, ATT_W), lambda i, j: (i, 0)),
            pl.BlockSpec((tm, ML_W), lambda i, j: (i, 0)),
            pl.BlockSpec((None, ATT_W, tn), lambda i, j: (layer, 0, j)),
            pl.BlockSpec((None, ML_W, tn), lambda i, j: (layer, 0, j)),
            pl.BlockSpec((tm, tn), lambda i, j: (i, ga0 + j)),
            pl.BlockSpec((tm, tn), lambda i, j: (i, ga0 + nd + j)),
            pl.BlockSpec((None, 1, tn), lambda i, j: (layer, 0, j)),
            pl.BlockSpec((None, 1, tn), lambda i, j: (layer, 0, nd + j)),
        ],
        out_specs=pl.BlockSpec((tm, tn), lambda i, j: (i, j)),
        out_shape=jax.ShapeDtypeStruct((m, D_MODEL), BF16),
        compiler_params=_cparams(("parallel", "parallel")),
        name="merge",
    )(ao, mh, w_ba, w_bm, proj, proj, b_gate, b_gate)


def _out_kernel(x_ref, mg_ref, w_ref, g_ref, b_ref, o_ref, ob_ref):
    y = DN_ALPHA * x_ref[...] + jnp.dot(mg_ref[...], w_ref[...], preferred_element_type=F32)
    o = _layer_norm_rows(y, g_ref[...], b_ref[...])
    o_ref[...] = o
    ob_ref[...] = o.astype(BF16)


def _out_proj_ln(x, merged, w_out, ln_g, ln_b, layer, tm):
    m = x.shape[0]
    return pl.pallas_call(
        _out_kernel,
        grid=(m // tm,),
        in_specs=[
            pl.BlockSpec((tm, D_MODEL), lambda i: (i, 0)),
            pl.BlockSpec((tm, D_MODEL), lambda i: (i, 0)),
            pl.BlockSpec((None, D_MODEL, D_MODEL), lambda i: (layer, 0, 0)),
            pl.BlockSpec((None, None, 1, D_MODEL), lambda i: (layer, 1, 0, 0)),
            pl.BlockSpec((None, None, 1, D_MODEL), lambda i: (layer, 1, 0, 0)),
        ],
        out_specs=[pl.BlockSpec((tm, D_MODEL), lambda i: (i, 0)),
                   pl.BlockSpec((tm, D_MODEL), lambda i: (i, 0))],
        out_shape=[jax.ShapeDtypeStruct((m, D_MODEL), F32),
                   jax.ShapeDtypeStruct((m, D_MODEL), BF16)],
        compiler_params=_cparams(("parallel",)),
        name="out_proj_ln",
    )(x, merged, w_out, ln_g, ln_b)


def _gates_transposed(pif, b_if, batch, seq, pad_to):
    gif = pif[:, :N_IF] + b_if
    gt = jnp.transpose(gif.reshape(batch, seq, N_IF), (0, 2, 1))
    if pad_to > seq:
        pad = jnp.concatenate([jnp.full((batch, ML_HEADS, pad_to - seq), NEG_BIG, F32),
                               jnp.full((batch, ML_HEADS, pad_to - seq), 1e4, F32)], axis=1)
        gt = jnp.concatenate([gt, pad], axis=2)
    return gt


def kernel(x_prompt, x_sample, cache_k, cache_v, page_table, state_C, state_n, state_m, rel_bias,
           w_in, b_gate, b_if, lam_qk, subln_w, mlstm_norm_w, w_branch_attn, w_branch_mlstm,
           w_out, ffn_up, ffn_down, ln_g, ln_b):
    batch, seq, _ = x_prompt.shape
    dec_batch, dec_seq, _ = x_sample.shape
    mp, ms = batch * seq, dec_batch * dec_seq
    tm_p, tm_s = 512, ms
    tq = 1024
    prompt_chunk = 256

    wba_b = w_branch_attn.astype(BF16)
    wbm_b = w_branch_mlstm.astype(BF16)
    wout_b = w_out.astype(BF16)
    ln_g = ln_g.reshape(DEPTH, 3, 1, D_MODEL)
    ln_b = ln_b.reshape(DEPTH, 3, 1, D_MODEL)
    b_gate3 = b_gate.reshape(DEPTH, 1, N_GATE)
    subln3 = subln_w.reshape(DEPTH, 1, DA_V_DIM)
    subln_col = subln_w.reshape(DEPTH, DA_V_DIM, 1)
    normw3 = mlstm_norm_w.reshape(DEPTH, 1, ML_W)
    bias_tiles = _prompt_bias_tiles(rel_bias, tq)

    zero_c = jnp.zeros((batch, ML_HEADS, ML_DV, ML_DQK), F32)
    zero_n = jnp.zeros((batch, ML_HEADS, 1, ML_DQK), F32)
    zero_m = jnp.zeros((batch, 1, ML_HEADS), F32)

    xp = x_prompt.reshape(mp, D_MODEL)
    xs = x_sample.reshape(ms, D_MODEL)
    xpb = xp.astype(BF16)
    xsb = xs.astype(BF16)

    outs = {name: [] for name in ('cp', 'np', 'mp', 'cs', 'ns', 'ms')}
    kv_p = kv_s = None
    for l in range(DEPTH):
        lam_init = 0.8 - 0.6 * math.exp(-0.3 * l)

        def rowwise_pre(x, xb, kv, ffn_w, proj_w, tm):
            x1, x1b, ffn_w = _ffn_ln(x, xb, ffn_w or (ffn_up, ffn_down), ln_g, ln_b, l, 0, 0, tm, 512)
            proj, pif, kv, proj_w = _in_proj(x1b, proj_w or w_in, kv, l, min(2 * tm, x1b.shape[0]))
            return x1, proj, pif, kv, ffn_w, proj_w

        def rowwise_post(x1, proj, ao, mh, ffn_w, tm):
            merged = _merge(ao, mh, proj, wba_b, wbm_b, b_gate3, l, tm, 512)
            x2, x2b = _out_proj_ln(x1, merged, wout_b, ln_g, ln_b, l, min(tm, 256))
            return _ffn_ln(x2, x2b, ffn_w or (ffn_up, ffn_down), ln_g, ln_b, l, 1, 2, tm, 512)

        s_x1, s_proj, pif, kv_s, ffn_w1, proj_w = rowwise_pre(xs, xsb, kv_s, None, None, tm_s)
        s_ao = _sample_attn(s_proj, cache_k, cache_v, page_table, rel_bias, lam_qk, subln3, l, lam_init,
                            dec_batch, dec_seq, 16)
        gt = _gates_transposed(pif, b_if[l], dec_batch, dec_seq, 128)
        s_mh, c_new, n_new, m_new = _mlstm(s_proj, gt, normw3, state_C[l],
                                           state_n[l].reshape(dec_batch, ML_HEADS, 1, ML_DQK),
                                           state_m[l].reshape(dec_batch, 1, ML_HEADS), l, dec_batch,
                                           dec_seq, dec_seq)
        outs['cs'].append(c_new)
        outs['ns'].append(n_new.reshape(dec_batch, ML_HEADS, ML_DQK))
        outs['ms'].append(m_new.reshape(dec_batch, ML_HEADS))

        x1, proj, pif, kv_p, _, _ = rowwise_pre(xp, xpb, kv_p, ffn_w1, proj_w, tm_p)
        ao = _prompt_attn(proj, bias_tiles, lam_qk, subln_col, l, lam_init, batch, seq, tq)
        gt = _gates_transposed(pif, b_if[l], batch, seq, seq)
        mh, c_new, n_new, m_new = _mlstm(proj, gt, normw3, zero_c, zero_n, zero_m, l, batch, seq,
                                         prompt_chunk)
        outs['cp'].append(c_new)
        outs['np'].append(n_new.reshape(batch, ML_HEADS, ML_DQK))
        outs['mp'].append(m_new.reshape(batch, ML_HEADS))

        xs, xsb, ffn_w2 = rowwise_post(s_x1, s_proj, s_ao, s_mh, None, tm_s)
        xp, xpb, _ = rowwise_post(x1, proj, ao, mh, ffn_w2, tm_p)

    st = lambda name: jnp.stack(outs[name])
    kv_shape_p = (DEPTH, batch, seq, DA_HEADS, DA_V_DIM)
    kv_shape_s = (DEPTH, dec_batch, dec_seq, DA_HEADS, DA_V_DIM)
    return (xp.reshape(batch, seq, D_MODEL), xs.reshape(dec_batch, dec_seq, D_MODEL),
            kv_p[0].reshape(kv_shape_p), kv_p[1].reshape(kv_shape_p),
            kv_s[0].reshape(kv_shape_s), kv_s[1].reshape(kv_shape_s), st('cp'), st('np'), st('mp'),
            st('cs'), st('ns'), st('ms'))
```

```python
import functools
import math

import numpy as np
import jax
import jax.numpy as jnp
from jax import lax
from jax.experimental import pallas as pl
from jax.experimental.pallas import tpu as pltpu

F32 = jnp.float32
BF16 = jnp.bfloat16

D_MODEL = 2048
DEPTH = 4
PAGE_SIZE = 128
DA_HEADS = 8
DA_HEAD_DIM = 64
DA_V_DIM = 2 * DA_HEAD_DIM
ML_HEADS = 4
ML_DQK = 256
ML_DV = 256
D_FF = 5632
N_BUCKETS = 32
MAX_DISTANCE = 128
LN_EPS = 1e-5
DN_ALPHA = (2.0 * DEPTH) ** 0.25

ATT_W = DA_HEADS * DA_V_DIM
ML_W = ML_HEADS * ML_DV
N_MAIN = 3 * ATT_W + 4 * ML_W
N_GATE = 2 * D_MODEL
N_IF = 2 * ML_HEADS
N_IF_PAD = 128
N_PROJ = N_MAIN + N_GATE
PROJ_TN = 1024
KV_COL_BLOCK = ATT_W // PROJ_TN
COL_GATE = N_MAIN

NEG_BIG = -1e30
LOG2E = math.log2(math.e)
QCOLS = 512
ACC_PAD = 16
VMEM_LIMIT = 48 * 1024 * 1024


def _cparams(sem):
    return pltpu.CompilerParams(dimension_semantics=sem, vmem_limit_bytes=VMEM_LIMIT)


def _layer_norm_rows(y, g, b):
    mu = jnp.mean(y, axis=-1, keepdims=True)
    yc = y - mu
    var = jnp.mean(yc * yc, axis=-1, keepdims=True)
    return yc * lax.rsqrt(var + LN_EPS) * g + b


def _ffn_kernel(x_ref, xb_ref, wa_ref, wb_ref, wd_ref, g_ref, b_ref, o_ref, ob_ref, *rest):
    acc_ref = rest[-1]
    j = pl.program_id(1)

    @pl.when(j == 0)
    def _():
        acc_ref[...] = jnp.zeros_like(acc_ref)

    wa, wb, wd = wa_ref[...].astype(BF16), wb_ref[...].astype(BF16), wd_ref[...].astype(BF16)
    for w_out_ref, w in zip(rest[:-1], (wa, wb, wd)):
        w_out_ref[...] = w
    xb = xb_ref[...]
    a = jnp.dot(xb, wa, preferred_element_type=F32)
    b = jnp.dot(xb, wb, preferred_element_type=F32)
    h = (a * jax.nn.sigmoid(a) * b).astype(BF16)
    acc_ref[...] += jnp.dot(h, wd, preferred_element_type=F32)

    @pl.when(j == pl.num_programs(1) - 1)
    def _():
        y = DN_ALPHA * x_ref[...] + 0.5 * acc_ref[...]
        o = _layer_norm_rows(y, g_ref[...], b_ref[...])
        o_ref[...] = o
        ob_ref[...] = o.astype(BF16)


def _ffn_ln(x, xb, weights, ln_g, ln_b, layer, which, ln_idx, tm, tf):
    m = x.shape[0]
    nf = D_FF // tf
    assert m == tm or len(weights) == 3, "bf16 weight copies need a single row tile (each tile written once)"
    row_spec = pl.BlockSpec((tm, D_MODEL), lambda i, j: (i, 0))
    ln_spec = pl.BlockSpec((None, None, 1, D_MODEL), lambda i, j: (layer, ln_idx, 0, 0))
    up_spec = pl.BlockSpec((D_MODEL, tf), lambda i, j: (0, j))
    down_spec = pl.BlockSpec((tf, D_MODEL), lambda i, j: (j, 0))
    out_specs = [row_spec, row_spec]
    out_shape = [jax.ShapeDtypeStruct((m, D_MODEL), F32), jax.ShapeDtypeStruct((m, D_MODEL), BF16)]
    if len(weights) == 2:
        w_up, w_down = weights
        w_args = (w_up, w_up, w_down)
        w_specs = [pl.BlockSpec((None, None, D_MODEL, tf), lambda i, j: (layer, which, 0, j)),
                   pl.BlockSpec((None, None, D_MODEL, tf), lambda i, j: (layer, which, 0, j + nf)),
                   pl.BlockSpec((None, None, tf, D_MODEL), lambda i, j: (layer, which, j, 0))]
        out_specs += [up_spec, up_spec, down_spec]
        out_shape += [jax.ShapeDtypeStruct((D_MODEL, D_FF), BF16)] * 2 + [jax.ShapeDtypeStruct((D_FF, D_MODEL), BF16)]
    else:
        w_args = tuple(weights)
        w_specs = [up_spec, up_spec, down_spec]
    outs = pl.pallas_call(
        _ffn_kernel,
        grid=(m // tm, nf),
        in_specs=[row_spec, row_spec] + w_specs + [ln_spec, ln_spec],
        out_specs=out_specs,
        out_shape=out_shape,
        scratch_shapes=[pltpu.VMEM((tm, D_MODEL), F32)],
        compiler_params=_cparams(("parallel", "arbitrary")),
        name="ffn_ln",
    )(x, xb, *w_args, ln_g, ln_b)
    return outs[0], outs[1], tuple(outs[2:])


def _gate_weight_kernel(a_ref, x_ref, o_ref):
    r = pltpu.roll(a_ref[...], PROJ_TN - N_IF, 1)
    rx = pltpu.roll(x_ref[...], N_IF_PAD - N_IF, 1)
    lane = lax.broadcasted_iota(jnp.int32, rx.shape, 1)
    tail = jnp.where(lane < N_IF_PAD - N_IF, r[:, PROJ_TN - N_IF_PAD:], rx)
    o_ref[...] = jnp.concatenate([r[:, :PROJ_TN - N_IF_PAD], tail], axis=1).astype(BF16)


def _gate_weights(w_slab):
    slabs_per_block = PROJ_TN // N_IF_PAD
    return pl.pallas_call(
        _gate_weight_kernel,
        grid=(DEPTH, N_GATE // PROJ_TN),
        in_specs=[pl.BlockSpec((None, D_MODEL, PROJ_TN), lambda l, j: (l, 0, j)),
                  pl.BlockSpec((None, D_MODEL, N_IF_PAD), lambda l, j: (l, 0, (j + 1) * slabs_per_block))],
        out_specs=pl.BlockSpec((None, D_MODEL, PROJ_TN), lambda l, j: (l, 0, j)),
        out_shape=jax.ShapeDtypeStruct((DEPTH, D_MODEL, N_GATE), BF16),
        compiler_params=_cparams(("parallel", "parallel")),
        name="gate_weights",
    )(w_slab, w_slab)


def _proj_kernel(xb_ref, wm_ref, wg_ref, wif_ref, *rest):
    o_ref, oif_ref, kv_ref = rest[-3:]
    j = pl.program_id(1)
    w = jnp.where(j < N_MAIN // PROJ_TN, wm_ref[...], wg_ref[...])
    o = jnp.dot(xb_ref[...], w, preferred_element_type=F32)
    o_ref[...] = o

    @pl.when(j == 0)
    def _():
        oif_ref[...] = jnp.dot(xb_ref[...], wif_ref[...], preferred_element_type=F32)

    @pl.when((j == KV_COL_BLOCK) | (j == KV_COL_BLOCK + 1))
    def _():
        kv_ref[...] = o


def _in_proj(xb, w_main, w_gate, w_if, kv_buf, layer, tm):
    m = xb.shape[0]
    n_main = N_MAIN // PROJ_TN
    in_specs = [
        pl.BlockSpec((tm, D_MODEL), lambda i, j: (i, 0)),
        pl.BlockSpec((None, D_MODEL, PROJ_TN), lambda i, j: (layer, 0, jnp.minimum(j, n_main - 1))),
        pl.BlockSpec((None, D_MODEL, PROJ_TN), lambda i, j: (layer, 0, jnp.maximum(j - n_main, 0))),
        pl.BlockSpec((None, D_MODEL, N_IF_PAD), lambda i, j: (layer, 0, 0)),
    ]
    args = [xb, w_main, w_gate, w_if]
    aliases = {}
    if kv_buf is not None:
        in_specs.append(pl.BlockSpec(memory_space=pl.ANY))
        args.append(kv_buf)
        aliases = {4: 2}
    return pl.pallas_call(
        _proj_kernel,
        grid=(m // tm, N_PROJ // PROJ_TN),
        in_specs=in_specs,
        out_specs=[pl.BlockSpec((tm, PROJ_TN), lambda i, j: (i, j)),
                   pl.BlockSpec((tm, N_IF_PAD), lambda i, j: (i, 0)),
                   pl.BlockSpec((None, None, tm, ATT_W),
                                lambda i, j: (jnp.clip(j - KV_COL_BLOCK, 0, 1), layer, i, 0))],
        out_shape=[jax.ShapeDtypeStruct((m, N_PROJ), F32),
                   jax.ShapeDtypeStruct((m, N_IF_PAD), F32),
                   jax.ShapeDtypeStruct((2, DEPTH, m, ATT_W), F32)],
        input_output_aliases=aliases,
        compiler_params=_cparams(("parallel", "arbitrary")),
        name="in_proj",
    )(*args)


def _t5_bucket_np(dist):
    n = np.maximum(dist, 0)
    max_exact = N_BUCKETS // 2
    nf = np.maximum(n, 1).astype(np.float32)
    large = max_exact + (np.log(nf / np.float32(max_exact)) / np.float32(math.log(MAX_DISTANCE / max_exact))
                         * np.float32(N_BUCKETS - max_exact)).astype(np.int32)
    large = np.minimum(large, N_BUCKETS - 1)
    return np.where(n < max_exact, n, large).astype(np.int32)


def _lambda_scalar(lq, lam_init):
    s01 = jnp.sum(lq[0:1, :] * lq[1:2, :], axis=1, keepdims=True)
    s23 = jnp.sum(lq[2:3, :] * lq[3:4, :], axis=1, keepdims=True)
    return jnp.exp(s01) - jnp.exp(s23) + lam_init


def _sub_norm(o, w_row, lam_init):
    o = o * lax.rsqrt(jnp.mean(o * o, axis=-1, keepdims=True) + LN_EPS)
    return o * w_row * (1.0 - lam_init)


def _pattn_kernel(qi_tab, ki_tab, q_ref, k_ref, v_ref, bias_ref, lq_ref, sw_ref, o_ref,
                  qt_sc, m_sc, acc_sc, *, tq, lam_init):
    t = pl.program_id(2)
    qi = qi_tab[t]
    ki = ki_tab[t]
    tk = k_ref.shape[0]

    @pl.when(ki == 0)
    def _():
        q = q_ref[...] * (DA_HEAD_DIM ** -0.5 * LOG2E)
        lane = lax.broadcasted_iota(jnp.int32, q.shape, 1)
        qt_sc[:, :tq] = jnp.where(lane < DA_HEAD_DIM, q, 0.0).T.astype(BF16)
        qt_sc[:, tq:] = jnp.where(lane >= DA_HEAD_DIM, q, 0.0).T.astype(BF16)
        m_sc[...] = jnp.full_like(m_sc, -jnp.inf)
        acc_sc[...] = jnp.zeros_like(acc_sc)

    def accumulate(with_bias, diagonal):
        kb = k_ref[...].astype(BF16)
        vt = jnp.concatenate([v_ref[...].T, jnp.ones((ACC_PAD, tk), F32)], axis=0).astype(BF16)
        n_chunks = 2 * tq // QCOLS
        cols = [slice(c * QCOLS, (c + 1) * QCOLS) for c in range(n_chunks)]
        keys = [((c * QCOLS) % tq + QCOLS) if diagonal else tk for c in range(n_chunks)]

        def scores(c):
            s = jnp.dot(kb[:keys[c]], qt_sc[:, cols[c]], preferred_element_type=F32)
            if with_bias:
                b0 = (c * QCOLS) % tq
                s = s + bias_ref[:keys[c], b0:b0 + QCOLS]
            return s

        def softmax(c, s):
            m_old = m_sc[:, cols[c]]
            m_new = jnp.maximum(m_old, jnp.max(s, axis=0, keepdims=True))
            m_sc[:, cols[c]] = m_new
            return jnp.exp2(s - m_new).astype(BF16), jnp.exp2(m_old - m_new)

        def update(c, p, alpha):
            acc_sc[:, cols[c]] = alpha * acc_sc[:, cols[c]] + jnp.dot(vt[:, :keys[c]], p,
                                                                      preferred_element_type=F32)

        s_q = {0: scores(0), 1: scores(1)}
        for c in range(n_chunks):
            p, alpha = softmax(c, s_q.pop(c))
            if c + 2 < n_chunks:
                s_q[c + 2] = scores(c + 2)
            update(c, p, alpha)

    @pl.when(qi == ki)
    def _():
        accumulate(True, True)

    @pl.when(qi - ki == 1)
    def _():
        accumulate(True, False)

    @pl.when(qi - ki > 1)
    def _():
        accumulate(False, False)

    @pl.when(ki == qi)
    def _():
        acc = acc_sc[...]
        o_all = acc[:DA_V_DIM] * (1.0 / acc[DA_V_DIM:DA_V_DIM + 1])
        lam = _lambda_scalar(lq_ref[...], lam_init)
        o = o_all[:, :tq] - lam * o_all[:, tq:]
        o = o * lax.rsqrt(jnp.mean(o * o, axis=0, keepdims=True) + LN_EPS)
        o = o * (sw_ref[...] * (1.0 - lam_init))
        o_ref[...] = o.T.astype(o_ref.dtype)


def _prompt_attn(proj, bias_tiles, lam_qk, subln_col, layer, lam_init, batch, seq, tq):
    nq = seq // tq
    tri = [(qi, ki) for qi in range(nq) for ki in range(qi + 1)]
    qi_tab = jnp.asarray([a for a, _ in tri], jnp.int32)
    ki_tab = jnp.asarray([b for _, b in tri], jnp.int32)
    hq, hk, hv = 0, ATT_W // DA_V_DIM, 2 * ATT_W // DA_V_DIM

    def bias_idx(b, h, t, qt, kt):
        return (h, jnp.minimum(qt[t] - kt[t], 1), 0, 0)

    grid_spec = pltpu.PrefetchScalarGridSpec(
        num_scalar_prefetch=2,
        grid=(batch, DA_HEADS, len(tri)),
        in_specs=[
            pl.BlockSpec((tq, DA_V_DIM), lambda b, h, t, qt, kt: (b * nq + qt[t], hq + h)),
            pl.BlockSpec((tq, DA_V_DIM), lambda b, h, t, qt, kt: (b * nq + kt[t], hk + h)),
            pl.BlockSpec((tq, DA_V_DIM), lambda b, h, t, qt, kt: (b * nq + kt[t], hv + h)),
            pl.BlockSpec((None, None, tq, tq), bias_idx),
            pl.BlockSpec((None, 4, DA_HEAD_DIM), lambda b, h, t, qt, kt: (layer, 0, 0)),
            pl.BlockSpec((None, DA_V_DIM, 1), lambda b, h, t, qt, kt: (layer, 0, 0)),
        ],
        out_specs=pl.BlockSpec((tq, DA_V_DIM), lambda b, h, t, qt, kt: (b * nq + qt[t], h)),
        scratch_shapes=[pltpu.VMEM((DA_V_DIM, 2 * tq), BF16), pltpu.VMEM((1, 2 * tq), F32),
                        pltpu.VMEM((DA_V_DIM + ACC_PAD, 2 * tq), F32)],
    )
    return pl.pallas_call(
        functools.partial(_pattn_kernel, tq=tq, lam_init=lam_init),
        grid_spec=grid_spec,
        out_shape=jax.ShapeDtypeStruct((batch * seq, ATT_W), BF16),
        compiler_params=_cparams(("parallel", "parallel", "arbitrary")),
        name="prompt_attn",
    )(qi_tab, ki_tab, proj, proj, proj, bias_tiles, lam_qk, subln_col)


def _skew_kernel(y_ref, o_ref, *, tq, rows):
    for r0 in range(0, tq, rows):
        x = jnp.broadcast_to(y_ref[...], (rows, 2 * tq))
        o_ref[r0:r0 + rows, :] = pltpu.roll(x, r0, 1, stride=1, stride_axis=0)[:, :tq]


def _prompt_bias_tiles(rel_bias, tq):
    assert tq >= MAX_DISTANCE and np.all(_t5_bucket_np(np.arange(MAX_DISTANCE, 1 << 16)) == N_BUCKETS - 1)
    span = 2 * tq
    d_row = np.zeros(span, np.int64)
    d_row[:tq] = np.arange(tq)
    d_row[span - np.arange(1, tq)] = -np.arange(1, tq)
    far = rel_bias[N_BUCKETS - 1]
    rows = []
    for delta in range(2):
        dist = d_row + delta * tq
        y = (rel_bias[_t5_bucket_np(dist)] - far) * LOG2E
        rows.append(jnp.where(jnp.asarray(dist >= 0)[:, None], y, NEG_BIG).T)
    y = jnp.stack(rows, axis=1).reshape(DA_HEADS, 2, 1, span)
    return pl.pallas_call(
        functools.partial(_skew_kernel, tq=tq, rows=256),
        grid=(DA_HEADS, 2),
        in_specs=[pl.BlockSpec((None, None, 1, span), lambda h, d: (h, d, 0, 0))],
        out_specs=pl.BlockSpec((None, None, tq, tq), lambda h, d: (h, d, 0, 0)),
        out_shape=jax.ShapeDtypeStruct((DA_HEADS, 2, tq, tq), F32),
        compiler_params=_cparams(("parallel", "parallel")),
        name="bias_skew",
    )(y)


def _page_rows(page_ref):
    return jnp.concatenate([page_ref[pl.ds(h, PAGE_SIZE, stride=DA_HEADS), :] for h in range(DA_HEADS)],
                           axis=1).astype(BF16)


def _sattn_kernel(pt_ref, w_ref, kn_ref, vn_ref, bl_ref, bn_ref, bf_ref, lq_ref, sw_ref, *rest,
                  n_group, n_pages, n_tok, lam_init):
    k_refs = rest[:n_group]
    v_refs = rest[n_group:2 * n_group]
    o_ref = rest[2 * n_group]
    m_sc, l_sc, acc_sc = rest[2 * n_group + 1:]
    step = pl.program_id(1)
    n_steps = pl.num_programs(1)
    hcols = 2 * n_tok
    w = w_ref[...]

    @pl.when(step == 0)
    def _():
        m_sc[...] = jnp.full_like(m_sc, -jnp.inf)
        l_sc[...] = jnp.zeros_like(l_sc)
        acc_sc[...] = jnp.zeros_like(acc_sc)

    def online_update(s, v_rows):
        m_old = m_sc[...]
        m_new = jnp.maximum(m_old, jnp.max(s, axis=0, keepdims=True))
        p = jnp.exp2(s - m_new)
        alpha = jnp.exp2(m_old - m_new)
        l_sc[...] = alpha * l_sc[...] + jnp.sum(p, axis=0, keepdims=True)
        m_sc[...] = m_new
        pt = p.T.astype(BF16)
        alpha_col = jnp.broadcast_to(alpha, (8, alpha.shape[1])).T[:, 0:1]
        for h in range(DA_HEADS):
            rows = slice(h * hcols, (h + 1) * hcols)
            d = jnp.dot(pt[rows], v_rows[:, h * DA_V_DIM:(h + 1) * DA_V_DIM], preferred_element_type=F32)
            acc_sc[h] = alpha_col[rows] * acc_sc[h] + d

    k_rows = jnp.concatenate([_page_rows(k_refs[g]) for g in range(n_group)], axis=0)
    s = jnp.dot(k_rows, w, preferred_element_type=F32)
    tail = jnp.where(step == n_steps - 1, bl_ref[...], bf_ref[...])
    s = s + jnp.concatenate([bf_ref[...]] * (n_group - 1) + [tail], axis=0)
    online_update(s, jnp.concatenate([_page_rows(v_refs[g]) for g in range(n_group)], axis=0))

    @pl.when(step == n_steps - 1)
    def _():
        pad = jnp.zeros((2 * n_tok - n_tok, kn_ref.shape[1]), F32)
        kn = jnp.concatenate([kn_ref[...], pad], axis=0).astype(BF16)
        vn = jnp.concatenate([vn_ref[...], pad], axis=0).astype(BF16)
        online_update(jnp.dot(kn, w, preferred_element_type=F32) + bn_ref[...], vn)
        lam = _lambda_scalar(lq_ref[...], lam_init)
        inv_l = jnp.broadcast_to(1.0 / l_sc[...], (8, l_sc.shape[1])).T[:, 0:1]
        for h in range(DA_HEADS):
            o = acc_sc[h] * inv_l[h * hcols:(h + 1) * hcols]
            o = o[:n_tok] - lam * o[n_tok:]
            o_ref[:, h * DA_V_DIM:(h + 1) * DA_V_DIM] = _sub_norm(o, sw_ref[...], lam_init).astype(o_ref.dtype)


def _sample_attn(proj, cache_k, cache_v, page_table, rel_bias, lam_qk, subln_w, layer, lam_init,
                 dec_batch, n_tok, n_group):
    n_pages = page_table.shape[1]
    assert n_pages % n_group == 0 and 2 * DA_HEADS * n_tok == DA_V_DIM
    past = n_pages * PAGE_SIZE
    n_steps = n_pages // n_group
    ncol = 2 * DA_HEADS * n_tok
    page_rows = PAGE_SIZE * DA_HEADS

    q = proj[:, :ATT_W].reshape(dec_batch, n_tok, DA_HEADS, 2, DA_HEAD_DIM) * (DA_HEAD_DIM ** -0.5 * LOG2E)
    w = jnp.einsum('bthjd,hH,jJ->bhjdHJt', q, jnp.eye(DA_HEADS, dtype=F32), jnp.eye(2, dtype=F32))
    w = w.reshape(dec_batch, ATT_W, ncol).astype(BF16)

    tok = np.arange(n_tok)

    def col_bias(dist, pad_rows=0):
        b = rel_bias[_t5_bucket_np(dist)] * LOG2E
        b = jnp.where(jnp.asarray(dist >= 0)[..., None], b, NEG_BIG)
        b = jnp.transpose(b, (0, 2, 1))[:, :, None, :]
        b = jnp.broadcast_to(b, (dist.shape[0], DA_HEADS, 2, n_tok)).reshape(dist.shape[0], ncol)
        if pad_rows:
            b = jnp.concatenate([b, jnp.full((pad_rows, ncol), NEG_BIG, F32)], axis=0)
        return b

    assert np.all(_t5_bucket_np(np.arange(PAGE_SIZE + 1, past + n_tok + 1)) == N_BUCKETS - 1)
    bias_last = col_bias((past + tok)[None, :] - (past - PAGE_SIZE + np.arange(PAGE_SIZE))[:, None])
    bias_new = col_bias(tok[None, :] - tok[:, None], pad_rows=n_tok)
    bias_far = col_bias(np.full((PAGE_SIZE, n_tok), PAGE_SIZE + 1))

    page_block = (None, None, page_rows, DA_V_DIM)
    cache_k = cache_k.reshape(cache_k.shape[0], cache_k.shape[1], page_rows, DA_V_DIM)
    cache_v = cache_v.reshape(cache_v.shape[0], cache_v.shape[1], page_rows, DA_V_DIM)

    def page_idx(g):
        return lambda b, s, pt: (layer, pt[b, s * n_group + g], 0, 0)

    const2 = lambda b, s, pt: (0, 0)
    grid_spec = pltpu.PrefetchScalarGridSpec(
        num_scalar_prefetch=1,
        grid=(dec_batch, n_steps),
        in_specs=[
            pl.BlockSpec((None, ATT_W, ncol), lambda b, s, pt: (b, 0, 0)),
            pl.BlockSpec((n_tok, ATT_W), lambda b, s, pt: (b, 1)),
            pl.BlockSpec((n_tok, ATT_W), lambda b, s, pt: (b, 2)),
            pl.BlockSpec((PAGE_SIZE, ncol), const2),
            pl.BlockSpec((2 * n_tok, ncol), const2),
            pl.BlockSpec((PAGE_SIZE, ncol), const2),
            pl.BlockSpec((None, 4, DA_HEAD_DIM), lambda b, s, pt: (layer, 0, 0)),
            pl.BlockSpec((None, 1, DA_V_DIM), lambda b, s, pt: (layer, 0, 0)),
        ] + [pl.BlockSpec(page_block, page_idx(g)) for g in range(n_group)] * 2,
        out_specs=pl.BlockSpec((n_tok, ATT_W), lambda b, s, pt: (b, 0)),
        scratch_shapes=[pltpu.VMEM((1, ncol), F32), pltpu.VMEM((1, ncol), F32),
                        pltpu.VMEM((DA_HEADS, 2 * n_tok, DA_V_DIM), F32)],
    )
    return pl.pallas_call(
        functools.partial(_sattn_kernel, n_group=n_group, n_pages=n_pages, n_tok=n_tok,
                          lam_init=lam_init),
        grid_spec=grid_spec,
        out_shape=jax.ShapeDtypeStruct((dec_batch * n_tok, ATT_W), BF16),
        compiler_params=_cparams(("parallel", "arbitrary")),
        name="sample_attn",
    )(page_table, w, proj, proj, bias_last, bias_new, bias_far, lam_qk, subln_w,
      *([cache_k] * n_group), *([cache_v] * n_group))


def _mlstm_kernel(q_ref, k_ref, v_ref, o_gate_ref, g_ref, nw_ref, c0_ref, n0_ref, m0_ref,
                  h_ref, c_ref, n_ref, m_ref, *, rows, chunk):
    c_idx = pl.program_id(1)

    @pl.when(c_idx == 0)
    def _():
        c_ref[...] = c0_ref[...]
        n_ref[...] = n0_ref[...]
        m_ref[...] = m0_ref[...]

    row = lax.broadcasted_iota(jnp.int32, (chunk, chunk), 0)
    col = lax.broadcasted_iota(jnp.int32, (chunk, chunk), 1)
    tril = row >= col
    eye = row == col

    def padded(x):
        if rows == chunk:
            return x
        return jnp.concatenate([x, jnp.zeros((chunk - rows, x.shape[1]), x.dtype)], axis=0)

    for h in range(ML_HEADS):
        sl = slice(h * ML_DQK, (h + 1) * ML_DQK)
        q = padded(q_ref[:, sl])
        k = padded(k_ref[:, sl]) * (ML_DQK ** -0.5)
        v = padded(v_ref[:, sl])
        ig = g_ref[h:h + 1, :]
        gf = g_ref[ML_HEADS + h:ML_HEADS + h + 1, :]
        lf = jnp.minimum(gf, 0.0) - jnp.log1p(jnp.exp(-jnp.abs(gf)))
        c_state = c_ref[h]
        n_state = n_ref[h]
        m_state = m_ref[:, h:h + 1]

        b_col = jnp.sum(jnp.where(tril, lf, 0.0), axis=1, keepdims=True)
        b_row = jnp.sum(jnp.where(eye, b_col, 0.0), axis=0, keepdims=True)
        logw = jnp.where(tril, b_col - b_row + ig, NEG_BIG)
        m_inter = b_col + m_state
        m_t = jnp.maximum(jnp.max(logw, axis=1, keepdims=True), m_inter)
        inter = jnp.exp(m_inter - m_t)
        qb = q.astype(BF16)
        kb = k.astype(BF16)
        s = lax.dot_general(qb, kb, (((1,), (1,)), ((), ())), preferred_element_type=F32)
        s = s * jnp.exp(logw - m_t)
        num = jnp.dot(s.astype(BF16), v.astype(BF16), preferred_element_type=F32)
        num = num + inter * lax.dot_general(qb, c_state.astype(BF16), (((1,), (1,)), ((), ())),
                                            preferred_element_type=F32)
        den = jnp.sum(s, axis=1, keepdims=True) + inter * jnp.sum(q * n_state, axis=1, keepdims=True)
        hh = num / jnp.maximum(jnp.abs(den), jnp.exp(-m_t))

        b_last = jnp.sum(lf, axis=1, keepdims=True)
        logw_end = b_last - b_row + ig
        m_new = jnp.maximum(b_last + m_state, jnp.max(logw_end, axis=1, keepdims=True))
        w_end = jnp.exp(logw_end - m_new)
        decay = jnp.exp(b_last + m_state - m_new)
        w_col = jnp.sum(jnp.where(eye, w_end, 0.0), axis=1, keepdims=True)
        vw = (v * w_col).astype(BF16)
        c_ref[h] = decay * c_state + lax.dot_general(vw, kb, (((0,), (0,)), ((), ())),
                                                     preferred_element_type=F32)
        n_ref[h] = decay * n_state + jnp.sum(k * w_col, axis=0, keepdims=True)
        m_ref[:, h:h + 1] = m_new

        hh = hh[:rows]
        mc = hh - jnp.mean(hh, axis=1, keepdims=True)
        y = mc * lax.rsqrt(jnp.mean(mc * mc, axis=1, keepdims=True) + LN_EPS) * nw_ref[:, sl]
        h_ref[:, sl] = (jax.nn.sigmoid(o_gate_ref[:, sl]) * y).astype(h_ref.dtype)


def _mlstm(proj, gates_t, norm_w, c0, n0, m0, layer, batch, seq, rows_per_block):
    nc = seq // rows_per_block
    chunk = gates_t.shape[2] // nc
    cq, ck, cv, co = (3 * ATT_W // ML_W, 3 * ATT_W // ML_W + 1, 3 * ATT_W // ML_W + 2,
                      3 * ATT_W // ML_W + 3)
    rows = lambda col: pl.BlockSpec((rows_per_block, ML_W), lambda b, c: (b * nc + c, col))
    state_c = pl.BlockSpec((None, ML_HEADS, ML_DV, ML_DQK), lambda b, c: (b, 0, 0, 0))
    state_n = pl.BlockSpec((None, ML_HEADS, 1, ML_DQK), lambda b, c: (b, 0, 0, 0))
    state_m = pl.BlockSpec((None, 1, ML_HEADS), lambda b, c: (b, 0, 0))
    return pl.pallas_call(
        functools.partial(_mlstm_kernel, rows=rows_per_block, chunk=chunk),
        grid=(batch, nc),
        in_specs=[rows(cq), rows(ck), rows(cv), rows(co),
                  pl.BlockSpec((None, 2 * ML_HEADS, chunk), lambda b, c: (b, 0, c)),
                  pl.BlockSpec((None, 1, ML_W), lambda b, c: (layer, 0, 0)),
                  state_c, state_n, state_m],
        out_specs=[pl.BlockSpec((rows_per_block, ML_W), lambda b, c: (b * nc + c, 0)),
                   state_c, state_n, state_m],
        out_shape=[jax.ShapeDtypeStruct((batch * seq, ML_W), BF16),
                   jax.ShapeDtypeStruct((batch, ML_HEADS, ML_DV, ML_DQK), F32),
                   jax.ShapeDtypeStruct((batch, ML_HEADS, 1, ML_DQK), F32),
                   jax.ShapeDtypeStruct((batch, 1, ML_HEADS), F32)],
        compiler_params=_cparams(("parallel", "arbitrary")),
        name="mlstm",
    )(proj, proj, proj, proj, gates_t, norm_w, c0, n0, m0)


def _merge_kernel(ao_ref, mh_ref, wa_ref, wm_ref, ga_ref, gm_ref, ba_ref, bm_ref, o_ref):
    ta = jnp.dot(ao_ref[...], wa_ref[...], preferred_element_type=F32)
    tmm = jnp.dot(mh_ref[...], wm_ref[...], preferred_element_type=F32)
    ga = jax.nn.sigmoid(ga_ref[...] + ba_ref[...])
    gm = jax.nn.sigmoid(gm_ref[...] + bm_ref[...])
    o_ref[...] = (ga * ta + gm * tmm).astype(o_ref.dtype)


def _merge(ao, mh, proj, w_ba, w_bm, b_gate, layer, tm, tn):
    m = ao.shape[0]
    nd = D_MODEL // tn
    ga0 = COL_GATE // tn
    return pl.pallas_call(
        _merge_kernel,
        grid=(m // tm, nd),
        in_specs=[
            pl.BlockSpec((tm, ATT_W), lambda i, j: (i, 0)),
            pl.BlockSpec((tm, ML_W), lambda i, j: (i, 0)),
            pl.BlockSpec((None, ATT_W, tn), lambda i, j: (layer, 0, j)),
            pl.BlockSpec((None, ML_W, tn), lambda i, j: (layer, 0, j)),
            pl.BlockSpec((tm, tn), lambda i, j: (i, ga0 + j)),
            pl.BlockSpec((tm, tn), lambda i, j: (i, ga0 + nd + j)),
            pl.BlockSpec((None, 1, tn), lambda i, j: (layer, 0, j)),
            pl.BlockSpec((None, 1, tn), lambda i, j: (layer, 0, nd + j)),
        ],
        out_specs=pl.BlockSpec((tm, tn), lambda i, j: (i, j)),
        out_shape=jax.ShapeDtypeStruct((m, D_MODEL), BF16),
        compiler_params=_cparams(("parallel", "parallel")),
        name="merge",
    )(ao, mh, w_ba, w_bm, proj, proj, b_gate, b_gate)


def _out_kernel(x_ref, mg_ref, w_ref, g_ref, b_ref, o_ref, ob_ref):
    y = DN_ALPHA * x_ref[...] + jnp.dot(mg_ref[...], w_ref[...], preferred_element_type=F32)
    o = _layer_norm_rows(y, g_ref[...], b_ref[...])
    o_ref[...] = o
    ob_ref[...] = o.astype(BF16)


def _out_proj_ln(x, merged, w_out, ln_g, ln_b, layer, tm):
    m = x.shape[0]
    return pl.pallas_call(
        _out_kernel,
        grid=(m // tm,),
        in_specs=[
            pl.BlockSpec((tm, D_MODEL), lambda i: (i, 0)),
            pl.BlockSpec((tm, D_MODEL), lambda i: (i, 0)),
            pl.BlockSpec((None, D_MODEL, D_MODEL), lambda i: (layer, 0, 0)),
            pl.BlockSpec((None, None, 1, D_MODEL), lambda i: (layer, 1, 0, 0)),
            pl.BlockSpec((None, None, 1, D_MODEL), lambda i: (layer, 1, 0, 0)),
        ],
        out_specs=[pl.BlockSpec((tm, D_MODEL), lambda i: (i, 0)),
                   pl.BlockSpec((tm, D_MODEL), lambda i: (i, 0))],
        out_shape=[jax.ShapeDtypeStruct((m, D_MODEL), F32),
                   jax.ShapeDtypeStruct((m, D_MODEL), BF16)],
        compiler_params=_cparams(("parallel",)),
        name="out_proj_ln",
    )(x, merged, w_out, ln_g, ln_b)


def _gates_transposed(pif, b_if, batch, seq, pad_to):
    gif = pif[:, :N_IF] + b_if
    gt = jnp.transpose(gif.reshape(batch, seq, N_IF), (0, 2, 1))
    if pad_to > seq:
        pad = jnp.concatenate([jnp.full((batch, ML_HEADS, pad_to - seq), NEG_BIG, F32),
                               jnp.full((batch, ML_HEADS, pad_to - seq), 1e4, F32)], axis=1)
        gt = jnp.concatenate([gt, pad], axis=2)
    return gt


def kernel(x_prompt, x_sample, cache_k, cache_v, page_table, state_C, state_n, state_m, rel_bias,
           w_in, b_gate, b_if, lam_qk, subln_w, mlstm_norm_w, w_branch_attn, w_branch_mlstm,
           w_out, ffn_up, ffn_down, ln_g, ln_b):
    batch, seq, _ = x_prompt.shape
    dec_batch, dec_seq, _ = x_sample.shape
    mp, ms = batch * seq, dec_batch * dec_seq
    tm_p, tm_s = 512, ms
    tq = 1024
    prompt_chunk = 256

    w_main = w_in[:, :, :N_MAIN].astype(BF16)
    w_gate = _gate_weights(w_in[:, :, N_MAIN:])
    w_if = w_in[:, :, N_MAIN:N_MAIN + N_IF_PAD].astype(BF16)
    wba_b = w_branch_attn.astype(BF16)
    wbm_b = w_branch_mlstm.astype(BF16)
    wout_b = w_out.astype(BF16)
    ln_g = ln_g.reshape(DEPTH, 3, 1, D_MODEL)
    ln_b = ln_b.reshape(DEPTH, 3, 1, D_MODEL)
    b_gate3 = b_gate.reshape(DEPTH, 1, N_GATE)
    subln3 = subln_w.reshape(DEPTH, 1, DA_V_DIM)
    subln_col = subln_w.reshape(DEPTH, DA_V_DIM, 1)
    normw3 = mlstm_norm_w.reshape(DEPTH, 1, ML_W)
    bias_tiles = _prompt_bias_tiles(rel_bias, tq)

    zero_c = jnp.zeros((batch, ML_HEADS, ML_DV, ML_DQK), F32)
    zero_n = jnp.zeros((batch, ML_HEADS, 1, ML_DQK), F32)
    zero_m = jnp.zeros((batch, 1, ML_HEADS), F32)

    xp = x_prompt.reshape(mp, D_MODEL)
    xs = x_sample.reshape(ms, D_MODEL)
    xpb = xp.astype(BF16)
    xsb = xs.astype(BF16)

    outs = {name: [] for name in ('cp', 'np', 'mp', 'cs', 'ns', 'ms')}
    kv_p = kv_s = None
    for l in range(DEPTH):
        lam_init = 0.8 - 0.6 * math.exp(-0.3 * l)

        def rowwise_pre(x, xb, kv, ffn_w, tm):
            x1, x1b, ffn_w = _ffn_ln(x, xb, ffn_w or (ffn_up, ffn_down), ln_g, ln_b, l, 0, 0, tm, 512)
            proj, pif, kv = _in_proj(x1b, w_main, w_gate, w_if, kv, l, min(2 * tm, x1b.shape[0]))
            return x1, proj, pif, kv, ffn_w

        def rowwise_post(x1, proj, ao, mh, ffn_w, tm):
            merged = _merge(ao, mh, proj, wba_b, wbm_b, b_gate3, l, tm, 512)
            x2, x2b = _out_proj_ln(x1, merged, wout_b, ln_g, ln_b, l, min(tm, 256))
            return _ffn_ln(x2, x2b, ffn_w or (ffn_up, ffn_down), ln_g, ln_b, l, 1, 2, tm, 512)

        s_x1, s_proj, pif, kv_s, ffn_w1 = rowwise_pre(xs, xsb, kv_s, None, tm_s)
        s_ao = _sample_attn(s_proj, cache_k, cache_v, page_table, rel_bias, lam_qk, subln3, l, lam_init,
                            dec_batch, dec_seq, 16)
        gt = _gates_transposed(pif, b_if[l], dec_batch, dec_seq, 128)
        s_mh, c_new, n_new, m_new = _mlstm(s_proj, gt, normw3, state_C[l],
                                           state_n[l].reshape(dec_batch, ML_HEADS, 1, ML_DQK),
                                           state_m[l].reshape(dec_batch, 1, ML_HEADS), l, dec_batch,
                                           dec_seq, dec_seq)
        outs['cs'].append(c_new)
        outs['ns'].append(n_new.reshape(dec_batch, ML_HEADS, ML_DQK))
        outs['ms'].append(m_new.reshape(dec_batch, ML_HEADS))

        x1, proj, pif, kv_p, _ = rowwise_pre(xp, xpb, kv_p, ffn_w1, tm_p)
        ao = _prompt_attn(proj, bias_tiles, lam_qk, subln_col, l, lam_init, batch, seq, tq)
        gt = _gates_transposed(pif, b_if[l], batch, seq, seq)
        mh, c_new, n_new, m_new = _mlstm(proj, gt, normw3, zero_c, zero_n, zero_m, l, batch, seq,
                                         prompt_chunk)
        outs['cp'].append(c_new)
        outs['np'].append(n_new.reshape(batch, ML_HEADS, ML_DQK))
        outs['mp'].append(m_new.reshape(batch, ML_HEADS))

        xs, xsb, ffn_w2 = rowwise_post(s_x1, s_proj, s_ao, s_mh, None, tm_s)
        xp, xpb, _ = rowwise_post(x1, proj, ao, mh, ffn_w2, tm_p)

    st = lambda name: jnp.stack(outs[name])
    kv_shape_p = (DEPTH, batch, seq, DA_HEADS, DA_V_DIM)
    kv_shape_s = (DEPTH, dec_batch, dec_seq, DA_HEADS, DA_V_DIM)
    return (xp.reshape(batch, seq, D_MODEL), xs.reshape(dec_batch, dec_seq, D_MODEL),
            kv_p[0].reshape(kv_shape_p), kv_p[1].reshape(kv_shape_p),
            kv_s[0].reshape(kv_shape_s), kv_s[1].reshape(kv_shape_s), st('cp'), st('np'), st('mp'),
            st('cs'), st('ns'), st('ms'))
```

```python
import functools
import math

import numpy as np
import jax
import jax.numpy as jnp
from jax import lax
from jax.experimental import pallas as pl
from jax.experimental.pallas import tpu as pltpu

F32 = jnp.float32
BF16 = jnp.bfloat16

D_MODEL = 2048
DEPTH = 4
PAGE_SIZE = 128
DA_HEADS = 8
DA_HEAD_DIM = 64
DA_V_DIM = 2 * DA_HEAD_DIM
ML_HEADS = 4
ML_DQK = 256
ML_DV = 256
D_FF = 5632
N_BUCKETS = 32
MAX_DISTANCE = 128
LN_EPS = 1e-5
DN_ALPHA = (2.0 * DEPTH) ** 0.25

ATT_W = DA_HEADS * DA_V_DIM
ML_W = ML_HEADS * ML_DV
N_MAIN = 3 * ATT_W + 4 * ML_W
N_GATE = 2 * D_MODEL
N_IF = 2 * ML_HEADS
N_IF_PAD = 128
N_PROJ = N_MAIN + N_GATE
PROJ_TN = 1024
KV_COL_BLOCK = ATT_W // PROJ_TN
COL_GATE = N_MAIN

NEG_BIG = -1e30
LOG2E = math.log2(math.e)
QCOLS = 512
ACC_PAD = 16
VMEM_LIMIT = 48 * 1024 * 1024


def _cparams(sem):
    return pltpu.CompilerParams(dimension_semantics=sem, vmem_limit_bytes=VMEM_LIMIT)


def _layer_norm_rows(y, g, b):
    mu = jnp.mean(y, axis=-1, keepdims=True)
    yc = y - mu
    var = jnp.mean(yc * yc, axis=-1, keepdims=True)
    return yc * lax.rsqrt(var + LN_EPS) * g + b


def _ffn_kernel(x_ref, xb_ref, wa_ref, wb_ref, wd_ref, g_ref, b_ref, o_ref, ob_ref, *rest):
    acc_ref = rest[-1]
    j = pl.program_id(1)

    @pl.when(j == 0)
    def _():
        acc_ref[...] = jnp.zeros_like(acc_ref)

    wa, wb, wd = wa_ref[...].astype(BF16), wb_ref[...].astype(BF16), wd_ref[...].astype(BF16)
    for w_out_ref, w in zip(rest[:-1], (wa, wb, wd)):
        w_out_ref[...] = w
    xb = xb_ref[...]
    a = jnp.dot(xb, wa, preferred_element_type=F32)
    b = jnp.dot(xb, wb, preferred_element_type=F32)
    h = (a * jax.nn.sigmoid(a) * b).astype(BF16)
    acc_ref[...] += jnp.dot(h, wd, preferred_element_type=F32)

    @pl.when(j == pl.num_programs(1) - 1)
    def _():
        y = DN_ALPHA * x_ref[...] + 0.5 * acc_ref[...]
        o = _layer_norm_rows(y, g_ref[...], b_ref[...])
        o_ref[...] = o
        ob_ref[...] = o.astype(BF16)


def _ffn_ln(x, xb, weights, ln_g, ln_b, layer, which, ln_idx, tm, tf):
    m = x.shape[0]
    nf = D_FF // tf
    assert m == tm or len(weights) == 3, "bf16 weight copies need a single row tile (each tile written once)"
    row_spec = pl.BlockSpec((tm, D_MODEL), lambda i, j: (i, 0))
    ln_spec = pl.BlockSpec((None, None, 1, D_MODEL), lambda i, j: (layer, ln_idx, 0, 0))
    up_spec = pl.BlockSpec((D_MODEL, tf), lambda i, j: (0, j))
    down_spec = pl.BlockSpec((tf, D_MODEL), lambda i, j: (j, 0))
    out_specs = [row_spec, row_spec]
    out_shape = [jax.ShapeDtypeStruct((m, D_MODEL), F32), jax.ShapeDtypeStruct((m, D_MODEL), BF16)]
    if len(weights) == 2:
        w_up, w_down = weights
        w_args = (w_up, w_up, w_down)
        w_specs = [pl.BlockSpec((None, None, D_MODEL, tf), lambda i, j: (layer, which, 0, j)),
                   pl.BlockSpec((None, None, D_MODEL, tf), lambda i, j: (layer, which, 0, j + nf)),
                   pl.BlockSpec((None, None, tf, D_MODEL), lambda i, j: (layer, which, j, 0))]
        out_specs += [up_spec, up_spec, down_spec]
        out_shape += [jax.ShapeDtypeStruct((D_MODEL, D_FF), BF16)] * 2 + [jax.ShapeDtypeStruct((D_FF, D_MODEL), BF16)]
    else:
        w_args = tuple(weights)
        w_specs = [up_spec, up_spec, down_spec]
    outs = pl.pallas_call(
        _ffn_kernel,
        grid=(m // tm, nf),
        in_specs=[row_spec, row_spec] + w_specs + [ln_spec, ln_spec],
        out_specs=out_specs,
        out_shape=out_shape,
        scratch_shapes=[pltpu.VMEM((tm, D_MODEL), F32)],
        compiler_params=_cparams(("parallel", "arbitrary")),
        name="ffn_ln",
    )(x, xb, *w_args, ln_g, ln_b)
    return outs[0], outs[1], tuple(outs[2:])


def _proj_kernel(xb_ref, w_ref, wif_ref, *rest):
    o_ref, oif_ref, kv_ref = rest[-3:]
    j = pl.program_id(1)
    o = jnp.dot(xb_ref[...], w_ref[...], preferred_element_type=F32)
    o_ref[...] = o

    @pl.when(j == 0)
    def _():
        oif_ref[...] = jnp.dot(xb_ref[...], wif_ref[...], preferred_element_type=F32)

    @pl.when((j == KV_COL_BLOCK) | (j == KV_COL_BLOCK + 1))
    def _():
        kv_ref[...] = o


def _in_proj(xb, w_cat, w_if, kv_buf, layer, tm):
    m = xb.shape[0]
    in_specs = [
        pl.BlockSpec((tm, D_MODEL), lambda i, j: (i, 0)),
        pl.BlockSpec((None, D_MODEL, PROJ_TN), lambda i, j: (layer, 0, j)),
        pl.BlockSpec((None, D_MODEL, N_IF_PAD), lambda i, j: (layer, 0, 0)),
    ]
    args = [xb, w_cat, w_if]
    aliases = {}
    if kv_buf is not None:
        in_specs.append(pl.BlockSpec(memory_space=pl.ANY))
        args.append(kv_buf)
        aliases = {3: 2}
    return pl.pallas_call(
        _proj_kernel,
        grid=(m // tm, N_PROJ // PROJ_TN),
        in_specs=in_specs,
        out_specs=[pl.BlockSpec((tm, PROJ_TN), lambda i, j: (i, j)),
                   pl.BlockSpec((tm, N_IF_PAD), lambda i, j: (i, 0)),
                   pl.BlockSpec((None, None, tm, ATT_W),
                                lambda i, j: (jnp.clip(j - KV_COL_BLOCK, 0, 1), layer, i, 0))],
        out_shape=[jax.ShapeDtypeStruct((m, N_PROJ), F32),
                   jax.ShapeDtypeStruct((m, N_IF_PAD), F32),
                   jax.ShapeDtypeStruct((2, DEPTH, m, ATT_W), F32)],
        input_output_aliases=aliases,
        compiler_params=_cparams(("parallel", "arbitrary")),
        name="in_proj",
    )(*args)


def _t5_bucket_np(dist):
    n = np.maximum(dist, 0)
    max_exact = N_BUCKETS // 2
    nf = np.maximum(n, 1).astype(np.float32)
    large = max_exact + (np.log(nf / np.float32(max_exact)) / np.float32(math.log(MAX_DISTANCE / max_exact))
                         * np.float32(N_BUCKETS - max_exact)).astype(np.int32)
    large = np.minimum(large, N_BUCKETS - 1)
    return np.where(n < max_exact, n, large).astype(np.int32)


def _lambda_scalar(lq, lam_init):
    s01 = jnp.sum(lq[0:1, :] * lq[1:2, :], axis=1, keepdims=True)
    s23 = jnp.sum(lq[2:3, :] * lq[3:4, :], axis=1, keepdims=True)
    return jnp.exp(s01) - jnp.exp(s23) + lam_init


def _sub_norm(o, w_row, lam_init):
    o = o * lax.rsqrt(jnp.mean(o * o, axis=-1, keepdims=True) + LN_EPS)
    return o * w_row * (1.0 - lam_init)


def _pattn_kernel(qi_tab, ki_tab, q_ref, k_ref, v_ref, bias_ref, lq_ref, sw_ref, o_ref,
                  qt_sc, m_sc, acc_sc, *, tq, lam_init):
    t = pl.program_id(2)
    qi = qi_tab[t]
    ki = ki_tab[t]
    tk = k_ref.shape[0]

    @pl.when(ki == 0)
    def _():
        q = q_ref[...] * (DA_HEAD_DIM ** -0.5 * LOG2E)
        lane = lax.broadcasted_iota(jnp.int32, q.shape, 1)
        qt_sc[:, :tq] = jnp.where(lane < DA_HEAD_DIM, q, 0.0).T.astype(BF16)
        qt_sc[:, tq:] = jnp.where(lane >= DA_HEAD_DIM, q, 0.0).T.astype(BF16)
        m_sc[...] = jnp.full_like(m_sc, -jnp.inf)
        acc_sc[...] = jnp.zeros_like(acc_sc)

    def accumulate(with_bias, diagonal):
        kb = k_ref[...].astype(BF16)
        vt = jnp.concatenate([v_ref[...].T, jnp.ones((ACC_PAD, tk), F32)], axis=0).astype(BF16)
        n_chunks = 2 * tq // QCOLS
        cols = [slice(c * QCOLS, (c + 1) * QCOLS) for c in range(n_chunks)]
        keys = [((c * QCOLS) % tq + QCOLS) if diagonal else tk for c in range(n_chunks)]

        def scores(c):
            s = jnp.dot(kb[:keys[c]], qt_sc[:, cols[c]], preferred_element_type=F32)
            if with_bias:
                b0 = (c * QCOLS) % tq
                s = s + bias_ref[:keys[c], b0:b0 + QCOLS]
            return s

        def softmax(c, s):
            m_old = m_sc[:, cols[c]]
            m_new = jnp.maximum(m_old, jnp.max(s, axis=0, keepdims=True))
            m_sc[:, cols[c]] = m_new
            return jnp.exp2(s - m_new).astype(BF16), jnp.exp2(m_old - m_new)

        def update(c, p, alpha):
            acc_sc[:, cols[c]] = alpha * acc_sc[:, cols[c]] + jnp.dot(vt[:, :keys[c]], p,
                                                                      preferred_element_type=F32)

        s_q = {0: scores(0), 1: scores(1)}
        for c in range(n_chunks):
            p, alpha = softmax(c, s_q.pop(c))
            if c + 2 < n_chunks:
                s_q[c + 2] = scores(c + 2)
            update(c, p, alpha)

    @pl.when(qi == ki)
    def _():
        accumulate(True, True)

    @pl.when(qi - ki == 1)
    def _():
        accumulate(True, False)

    @pl.when(qi - ki > 1)
    def _():
        accumulate(False, False)

    @pl.when(ki == qi)
    def _():
        acc = acc_sc[...]
        o_all = acc[:DA_V_DIM] * (1.0 / acc[DA_V_DIM:DA_V_DIM + 1])
        lam = _lambda_scalar(lq_ref[...], lam_init)
        o = o_all[:, :tq] - lam * o_all[:, tq:]
        o = o * lax.rsqrt(jnp.mean(o * o, axis=0, keepdims=True) + LN_EPS)
        o = o * (sw_ref[...] * (1.0 - lam_init))
        o_ref[...] = o.T.astype(o_ref.dtype)


def _prompt_attn(proj, bias_tiles, lam_qk, subln_col, layer, lam_init, batch, seq, tq):
    nq = seq // tq
    tri = [(qi, ki) for qi in range(nq) for ki in range(qi + 1)]
    qi_tab = jnp.asarray([a for a, _ in tri], jnp.int32)
    ki_tab = jnp.asarray([b for _, b in tri], jnp.int32)
    hq, hk, hv = 0, ATT_W // DA_V_DIM, 2 * ATT_W // DA_V_DIM

    def bias_idx(b, h, t, qt, kt):
        return (h, jnp.minimum(qt[t] - kt[t], 1), 0, 0)

    grid_spec = pltpu.PrefetchScalarGridSpec(
        num_scalar_prefetch=2,
        grid=(batch, DA_HEADS, len(tri)),
        in_specs=[
            pl.BlockSpec((tq, DA_V_DIM), lambda b, h, t, qt, kt: (b * nq + qt[t], hq + h)),
            pl.BlockSpec((tq, DA_V_DIM), lambda b, h, t, qt, kt: (b * nq + kt[t], hk + h)),
            pl.BlockSpec((tq, DA_V_DIM), lambda b, h, t, qt, kt: (b * nq + kt[t], hv + h)),
            pl.BlockSpec((None, None, tq, tq), bias_idx),
            pl.BlockSpec((None, 4, DA_HEAD_DIM), lambda b, h, t, qt, kt: (layer, 0, 0)),
            pl.BlockSpec((None, DA_V_DIM, 1), lambda b, h, t, qt, kt: (layer, 0, 0)),
        ],
        out_specs=pl.BlockSpec((tq, DA_V_DIM), lambda b, h, t, qt, kt: (b * nq + qt[t], h)),
        scratch_shapes=[pltpu.VMEM((DA_V_DIM, 2 * tq), BF16), pltpu.VMEM((1, 2 * tq), F32),
                        pltpu.VMEM((DA_V_DIM + ACC_PAD, 2 * tq), F32)],
    )
    return pl.pallas_call(
        functools.partial(_pattn_kernel, tq=tq, lam_init=lam_init),
        grid_spec=grid_spec,
        out_shape=jax.ShapeDtypeStruct((batch * seq, ATT_W), BF16),
        compiler_params=_cparams(("parallel", "parallel", "arbitrary")),
        name="prompt_attn",
    )(qi_tab, ki_tab, proj, proj, proj, bias_tiles, lam_qk, subln_col)


def _skew_kernel(y_ref, o_ref, *, tq, rows):
    for r0 in range(0, tq, rows):
        x = jnp.broadcast_to(y_ref[...], (rows, 2 * tq))
        o_ref[r0:r0 + rows, :] = pltpu.roll(x, r0, 1, stride=1, stride_axis=0)[:, :tq]


def _prompt_bias_tiles(rel_bias, tq):
    assert tq >= MAX_DISTANCE and np.all(_t5_bucket_np(np.arange(MAX_DISTANCE, 1 << 16)) == N_BUCKETS - 1)
    span = 2 * tq
    d_row = np.zeros(span, np.int64)
    d_row[:tq] = np.arange(tq)
    d_row[span - np.arange(1, tq)] = -np.arange(1, tq)
    far = rel_bias[N_BUCKETS - 1]
    rows = []
    for delta in range(2):
        dist = d_row + delta * tq
        y = (rel_bias[_t5_bucket_np(dist)] - far) * LOG2E
        rows.append(jnp.where(jnp.asarray(dist >= 0)[:, None], y, NEG_BIG).T)
    y = jnp.stack(rows, axis=1).reshape(DA_HEADS, 2, 1, span)
    return pl.pallas_call(
        functools.partial(_skew_kernel, tq=tq, rows=256),
        grid=(DA_HEADS, 2),
        in_specs=[pl.BlockSpec((None, None, 1, span), lambda h, d: (h, d, 0, 0))],
        out_specs=pl.BlockSpec((None, None, tq, tq), lambda h, d: (h, d, 0, 0)),
        out_shape=jax.ShapeDtypeStruct((DA_HEADS, 2, tq, tq), F32),
        compiler_params=_cparams(("parallel", "parallel")),
        name="bias_skew",
    )(y)


def _page_rows(page_ref):
    return jnp.concatenate([page_ref[pl.ds(h, PAGE_SIZE, stride=DA_HEADS), :] for h in range(DA_HEADS)],
                           axis=1).astype(BF16)


def _sattn_kernel(pt_ref, w_ref, kn_ref, vn_ref, bl_ref, bn_ref, bf_ref, lq_ref, sw_ref, *rest,
                  n_group, n_pages, n_tok, lam_init):
    k_refs = rest[:n_group]
    v_refs = rest[n_group:2 * n_group]
    o_ref = rest[2 * n_group]
    m_sc, l_sc, acc_sc = rest[2 * n_group + 1:]
    step = pl.program_id(1)
    n_steps = pl.num_programs(1)
    hcols = 2 * n_tok
    w = w_ref[...]

    @pl.when(step == 0)
    def _():
        m_sc[...] = jnp.full_like(m_sc, -jnp.inf)
        l_sc[...] = jnp.zeros_like(l_sc)
        acc_sc[...] = jnp.zeros_like(acc_sc)

    def online_update(s, v_rows):
        m_old = m_sc[...]
        m_new = jnp.maximum(m_old, jnp.max(s, axis=0, keepdims=True))
        p = jnp.exp2(s - m_new)
        alpha = jnp.exp2(m_old - m_new)
        l_sc[...] = alpha * l_sc[...] + jnp.sum(p, axis=0, keepdims=True)
        m_sc[...] = m_new
        pt = p.T.astype(BF16)
        alpha_col = jnp.broadcast_to(alpha, (8, alpha.shape[1])).T[:, 0:1]
        for h in range(DA_HEADS):
            rows = slice(h * hcols, (h + 1) * hcols)
            d = jnp.dot(pt[rows], v_rows[:, h * DA_V_DIM:(h + 1) * DA_V_DIM], preferred_element_type=F32)
            acc_sc[h] = alpha_col[rows] * acc_sc[h] + d

    k_rows = jnp.concatenate([_page_rows(k_refs[g]) for g in range(n_group)], axis=0)
    s = jnp.dot(k_rows, w, preferred_element_type=F32)
    tail = jnp.where(step == n_steps - 1, bl_ref[...], bf_ref[...])
    s = s + jnp.concatenate([bf_ref[...]] * (n_group - 1) + [tail], axis=0)
    online_update(s, jnp.concatenate([_page_rows(v_refs[g]) for g in range(n_group)], axis=0))

    @pl.when(step == n_steps - 1)
    def _():
        pad = jnp.zeros((2 * n_tok - n_tok, kn_ref.shape[1]), F32)
        kn = jnp.concatenate([kn_ref[...], pad], axis=0).astype(BF16)
        vn = jnp.concatenate([vn_ref[...], pad], axis=0).astype(BF16)
        online_update(jnp.dot(kn, w, preferred_element_type=F32) + bn_ref[...], vn)
        lam = _lambda_scalar(lq_ref[...], lam_init)
        inv_l = jnp.broadcast_to(1.0 / l_sc[...], (8, l_sc.shape[1])).T[:, 0:1]
        for h in range(DA_HEADS):
            o = acc_sc[h] * inv_l[h * hcols:(h + 1) * hcols]
            o = o[:n_tok] - lam * o[n_tok:]
            o_ref[:, h * DA_V_DIM:(h + 1) * DA_V_DIM] = _sub_norm(o, sw_ref[...], lam_init).astype(o_ref.dtype)


def _sample_attn(proj, cache_k, cache_v, page_table, rel_bias, lam_qk, subln_w, layer, lam_init,
                 dec_batch, n_tok, n_group):
    n_pages = page_table.shape[1]
    assert n_pages % n_group == 0 and 2 * DA_HEADS * n_tok == DA_V_DIM
    past = n_pages * PAGE_SIZE
    n_steps = n_pages // n_group
    ncol = 2 * DA_HEADS * n_tok
    page_rows = PAGE_SIZE * DA_HEADS

    q = proj[:, :ATT_W].reshape(dec_batch, n_tok, DA_HEADS, 2, DA_HEAD_DIM) * (DA_HEAD_DIM ** -0.5 * LOG2E)
    w = jnp.einsum('bthjd,hH,jJ->bhjdHJt', q, jnp.eye(DA_HEADS, dtype=F32), jnp.eye(2, dtype=F32))
    w = w.reshape(dec_batch, ATT_W, ncol).astype(BF16)

    tok = np.arange(n_tok)

    def col_bias(dist, pad_rows=0):
        b = rel_bias[_t5_bucket_np(dist)] * LOG2E
        b = jnp.where(jnp.asarray(dist >= 0)[..., None], b, NEG_BIG)
        b = jnp.transpose(b, (0, 2, 1))[:, :, None, :]
        b = jnp.broadcast_to(b, (dist.shape[0], DA_HEADS, 2, n_tok)).reshape(dist.shape[0], ncol)
        if pad_rows:
            b = jnp.concatenate([b, jnp.full((pad_rows, ncol), NEG_BIG, F32)], axis=0)
        return b

    assert np.all(_t5_bucket_np(np.arange(PAGE_SIZE + 1, past + n_tok + 1)) == N_BUCKETS - 1)
    bias_last = col_bias((past + tok)[None, :] - (past - PAGE_SIZE + np.arange(PAGE_SIZE))[:, None])
    bias_new = col_bias(tok[None, :] - tok[:, None], pad_rows=n_tok)
    bias_far = col_bias(np.full((PAGE_SIZE, n_tok), PAGE_SIZE + 1))

    page_block = (None, None, page_rows, DA_V_DIM)
    cache_k = cache_k.reshape(cache_k.shape[0], cache_k.shape[1], page_rows, DA_V_DIM)
    cache_v = cache_v.reshape(cache_v.shape[0], cache_v.shape[1], page_rows, DA_V_DIM)

    def page_idx(g):
        return lambda b, s, pt: (layer, pt[b, s * n_group + g], 0, 0)

    const2 = lambda b, s, pt: (0, 0)
    grid_spec = pltpu.PrefetchScalarGridSpec(
        num_scalar_prefetch=1,
        grid=(dec_batch, n_steps),
        in_specs=[
            pl.BlockSpec((None, ATT_W, ncol), lambda b, s, pt: (b, 0, 0)),
            pl.BlockSpec((n_tok, ATT_W), lambda b, s, pt: (b, 1)),
            pl.BlockSpec((n_tok, ATT_W), lambda b, s, pt: (b, 2)),
            pl.BlockSpec((PAGE_SIZE, ncol), const2),
            pl.BlockSpec((2 * n_tok, ncol), const2),
            pl.BlockSpec((PAGE_SIZE, ncol), const2),
            pl.BlockSpec((None, 4, DA_HEAD_DIM), lambda b, s, pt: (layer, 0, 0)),
            pl.BlockSpec((None, 1, DA_V_DIM), lambda b, s, pt: (layer, 0, 0)),
        ] + [pl.BlockSpec(page_block, page_idx(g)) for g in range(n_group)] * 2,
        out_specs=pl.BlockSpec((n_tok, ATT_W), lambda b, s, pt: (b, 0)),
        scratch_shapes=[pltpu.VMEM((1, ncol), F32), pltpu.VMEM((1, ncol), F32),
                        pltpu.VMEM((DA_HEADS, 2 * n_tok, DA_V_DIM), F32)],
    )
    return pl.pallas_call(
        functools.partial(_sattn_kernel, n_group=n_group, n_pages=n_pages, n_tok=n_tok,
                          lam_init=lam_init),
        grid_spec=grid_spec,
        out_shape=jax.ShapeDtypeStruct((dec_batch * n_tok, ATT_W), BF16),
        compiler_params=_cparams(("parallel", "arbitrary")),
        name="sample_attn",
    )(page_table, w, proj, proj, bias_last, bias_new, bias_far, lam_qk, subln_w,
      *([cache_k] * n_group), *([cache_v] * n_group))


def _mlstm_kernel(q_ref, k_ref, v_ref, o_gate_ref, g_ref, nw_ref, c0_ref, n0_ref, m0_ref,
                  h_ref, c_ref, n_ref, m_ref, *, rows, chunk):
    c_idx = pl.program_id(1)

    @pl.when(c_idx == 0)
    def _():
        c_ref[...] = c0_ref[...]
        n_ref[...] = n0_ref[...]
        m_ref[...] = m0_ref[...]

    row = lax.broadcasted_iota(jnp.int32, (chunk, chunk), 0)
    col = lax.broadcasted_iota(jnp.int32, (chunk, chunk), 1)
    tril = row >= col
    eye = row == col

    def padded(x):
        if rows == chunk:
            return x
        return jnp.concatenate([x, jnp.zeros((chunk - rows, x.shape[1]), x.dtype)], axis=0)

    for h in range(ML_HEADS):
        sl = slice(h * ML_DQK, (h + 1) * ML_DQK)
        q = padded(q_ref[:, sl])
        k = padded(k_ref[:, sl]) * (ML_DQK ** -0.5)
        v = padded(v_ref[:, sl])
        ig = g_ref[h:h + 1, :]
        gf = g_ref[ML_HEADS + h:ML_HEADS + h + 1, :]
        lf = jnp.minimum(gf, 0.0) - jnp.log1p(jnp.exp(-jnp.abs(gf)))
        c_state = c_ref[h]
        n_state = n_ref[h]
        m_state = m_ref[:, h:h + 1]

        b_col = jnp.sum(jnp.where(tril, lf, 0.0), axis=1, keepdims=True)
        b_row = jnp.sum(jnp.where(eye, b_col, 0.0), axis=0, keepdims=True)
        logw = jnp.where(tril, b_col - b_row + ig, NEG_BIG)
        m_inter = b_col + m_state
        m_t = jnp.maximum(jnp.max(logw, axis=1, keepdims=True), m_inter)
        inter = jnp.exp(m_inter - m_t)
        qb = q.astype(BF16)
        kb = k.astype(BF16)
        s = lax.dot_general(qb, kb, (((1,), (1,)), ((), ())), preferred_element_type=F32)
        s = s * jnp.exp(logw - m_t)
        num = jnp.dot(s.astype(BF16), v.astype(BF16), preferred_element_type=F32)
        num = num + inter * lax.dot_general(qb, c_state.astype(BF16), (((1,), (1,)), ((), ())),
                                            preferred_element_type=F32)
        den = jnp.sum(s, axis=1, keepdims=True) + inter * jnp.sum(q * n_state, axis=1, keepdims=True)
        hh = num / jnp.maximum(jnp.abs(den), jnp.exp(-m_t))

        b_last = jnp.sum(lf, axis=1, keepdims=True)
        logw_end = b_last - b_row + ig
        m_new = jnp.maximum(b_last + m_state, jnp.max(logw_end, axis=1, keepdims=True))
        w_end = jnp.exp(logw_end - m_new)
        decay = jnp.exp(b_last + m_state - m_new)
        w_col = jnp.sum(jnp.where(eye, w_end, 0.0), axis=1, keepdims=True)
        vw = (v * w_col).astype(BF16)
        c_ref[h] = decay * c_state + lax.dot_general(vw, kb, (((0,), (0,)), ((), ())),
                                                     preferred_element_type=F32)
        n_ref[h] = decay * n_state + jnp.sum(k * w_col, axis=0, keepdims=True)
        m_ref[:, h:h + 1] = m_new

        hh = hh[:rows]
        mc = hh - jnp.mean(hh, axis=1, keepdims=True)
        y = mc * lax.rsqrt(jnp.mean(mc * mc, axis=1, keepdims=True) + LN_EPS) * nw_ref[:, sl]
        h_ref[:, sl] = (jax.nn.sigmoid(o_gate_ref[:, sl]) * y).astype(h_ref.dtype)


def _mlstm(proj, gates_t, norm_w, c0, n0, m0, layer, batch, seq, rows_per_block):
    nc = seq // rows_per_block
    chunk = gates_t.shape[2] // nc
    cq, ck, cv, co = (3 * ATT_W // ML_W, 3 * ATT_W // ML_W + 1, 3 * ATT_W // ML_W + 2,
                      3 * ATT_W // ML_W + 3)
    rows = lambda col: pl.BlockSpec((rows_per_block, ML_W), lambda b, c: (b * nc + c, col))
    state_c = pl.BlockSpec((None, ML_HEADS, ML_DV, ML_DQK), lambda b, c: (b, 0, 0, 0))
    state_n = pl.BlockSpec((None, ML_HEADS, 1, ML_DQK), lambda b, c: (b, 0, 0, 0))
    state_m = pl.BlockSpec((None, 1, ML_HEADS), lambda b, c: (b, 0, 0))
    return pl.pallas_call(
        functools.partial(_mlstm_kernel, rows=rows_per_block, chunk=chunk),
        grid=(batch, nc),
        in_specs=[rows(cq), rows(ck), rows(cv), rows(co),
                  pl.BlockSpec((None, 2 * ML_HEADS, chunk), lambda b, c: (b, 0, c)),
                  pl.BlockSpec((None, 1, ML_W), lambda b, c: (layer, 0, 0)),
                  state_c, state_n, state_m],
        out_specs=[pl.BlockSpec((rows_per_block, ML_W), lambda b, c: (b * nc + c, 0)),
                   state_c, state_n, state_m],
        out_shape=[jax.ShapeDtypeStruct((batch * seq, ML_W), BF16),
                   jax.ShapeDtypeStruct((batch, ML_HEADS, ML_DV, ML_DQK), F32),
                   jax.ShapeDtypeStruct((batch, ML_HEADS, 1, ML_DQK), F32),
                   jax.ShapeDtypeStruct((batch, 1, ML_HEADS), F32)],
        compiler_params=_cparams(("parallel", "arbitrary")),
        name="mlstm",
    )(proj, proj, proj, proj, gates_t, norm_w, c0, n0, m0)


def _merge_kernel(ao_ref, mh_ref, wa_ref, wm_ref, ga_ref, gm_ref, ba_ref, bm_ref, o_ref):
    ta = jnp.dot(ao_ref[...], wa_ref[...], preferred_element_type=F32)
    tmm = jnp.dot(mh_ref[...], wm_ref[...], preferred_element_type=F32)
    ga = jax.nn.sigmoid(ga_ref[...] + ba_ref[...])
    gm = jax.nn.sigmoid(gm_ref[...] + bm_ref[...])
    o_ref[...] = (ga * ta + gm * tmm).astype(o_ref.dtype)


def _merge(ao, mh, proj, w_ba, w_bm, b_gate, layer, tm, tn):
    m = ao.shape[0]
    nd = D_MODEL // tn
    ga0 = COL_GATE // tn
    return pl.pallas_call(
        _merge_kernel,
        grid=(m // tm, nd),
        in_specs=[
            pl.BlockSpec((tm, ATT_W), lambda i, j: (i, 0)),
            pl.BlockSpec((tm, ML_W), lambda i, j: (i, 0)),
            pl.BlockSpec((None, ATT_W, tn), lambda i, j: (layer, 0, j)),
            pl.BlockSpec((None, ML_W, tn), lambda i, j: (layer, 0, j)),
            pl.BlockSpec((tm, tn), lambda i, j: (i, ga0 + j)),
            pl.BlockSpec((tm, tn), lambda i, j: (i, ga0 + nd + j)),
            pl.BlockSpec((None, 1, tn), lambda i, j: (layer, 0, j)),
            pl.BlockSpec((None, 1, tn), lambda i, j: (layer, 0, nd + j)),
        ],
        out_specs=pl.BlockSpec((tm, tn), lambda i, j: (i, j)),
        out_shape=jax.ShapeDtypeStruct((m, D_MODEL), BF16),
        compiler_params=_cparams(("parallel", "parallel")),
        name="merge",
    )(ao, mh, w_ba, w_bm, proj, proj, b_gate, b_gate)


def _out_kernel(x_ref, mg_ref, w_ref, g_ref, b_ref, o_ref, ob_ref):
    y = DN_ALPHA * x_ref[...] + jnp.dot(mg_ref[...], w_ref[...], preferred_element_type=F32)
    o = _layer_norm_rows(y, g_ref[...], b_ref[...])
    o_ref[...] = o
    ob_ref[...] = o.astype(BF16)


def _out_proj_ln(x, merged, w_out, ln_g, ln_b, layer, tm):
    m = x.shape[0]
    return pl.pallas_call(
        _out_kernel,
        grid=(m // tm,),
        in_specs=[
            pl.BlockSpec((tm, D_MODEL), lambda i: (i, 0)),
            pl.BlockSpec((tm, D_MODEL), lambda i: (i, 0)),
            pl.BlockSpec((None, D_MODEL, D_MODEL), lambda i: (layer, 0, 0)),
            pl.BlockSpec((None, None, 1, D_MODEL), lambda i: (layer, 1, 0, 0)),
            pl.BlockSpec((None, None, 1, D_MODEL), lambda i: (layer, 1, 0, 0)),
        ],
        out_specs=[pl.BlockSpec((tm, D_MODEL), lambda i: (i, 0)),
                   pl.BlockSpec((tm, D_MODEL), lambda i: (i, 0))],
        out_shape=[jax.ShapeDtypeStruct((m, D_MODEL), F32),
                   jax.ShapeDtypeStruct((m, D_MODEL), BF16)],
        compiler_params=_cparams(("parallel",)),
        name="out_proj_ln",
    )(x, merged, w_out, ln_g, ln_b)


def _gates_transposed(pif, b_if, batch, seq, pad_to):
    gif = pif[:, :N_IF] + b_if
    gt = jnp.transpose(gif.reshape(batch, seq, N_IF), (0, 2, 1))
    if pad_to > seq:
        pad = jnp.concatenate([jnp.full((batch, ML_HEADS, pad_to - seq), NEG_BIG, F32),
                               jnp.full((batch, ML_HEADS, pad_to - seq), 1e4, F32)], axis=1)
        gt = jnp.concatenate([gt, pad], axis=2)
    return gt


def kernel(x_prompt, x_sample, cache_k, cache_v, page_table, state_C, state_n, state_m, rel_bias,
           w_in, b_gate, b_if, lam_qk, subln_w, mlstm_norm_w, w_branch_attn, w_branch_mlstm,
           w_out, ffn_up, ffn_down, ln_g, ln_b):
    batch, seq, _ = x_prompt.shape
    dec_batch, dec_seq, _ = x_sample.shape
    mp, ms = batch * seq, dec_batch * dec_seq
    tm_p, tm_s = 512, ms
    tq = 1024
    prompt_chunk = 256

    w_cat = jnp.concatenate([w_in[:, :, :N_MAIN], w_in[:, :, N_MAIN + N_IF:]], axis=2).astype(BF16)
    w_if = w_in[:, :, N_MAIN:N_MAIN + N_IF_PAD].astype(BF16)
    wba_b = w_branch_attn.astype(BF16)
    wbm_b = w_branch_mlstm.astype(BF16)
    wout_b = w_out.astype(BF16)
    ln_g = ln_g.reshape(DEPTH, 3, 1, D_MODEL)
    ln_b = ln_b.reshape(DEPTH, 3, 1, D_MODEL)
    b_gate3 = b_gate.reshape(DEPTH, 1, N_GATE)
    subln3 = subln_w.reshape(DEPTH, 1, DA_V_DIM)
    subln_col = subln_w.reshape(DEPTH, DA_V_DIM, 1)
    normw3 = mlstm_norm_w.reshape(DEPTH, 1, ML_W)
    bias_tiles = _prompt_bias_tiles(rel_bias, tq)

    zero_c = jnp.zeros((batch, ML_HEADS, ML_DV, ML_DQK), F32)
    zero_n = jnp.zeros((batch, ML_HEADS, 1, ML_DQK), F32)
    zero_m = jnp.zeros((batch, 1, ML_HEADS), F32)

    xp = x_prompt.reshape(mp, D_MODEL)
    xs = x_sample.reshape(ms, D_MODEL)
    xpb = xp.astype(BF16)
    xsb = xs.astype(BF16)

    outs = {name: [] for name in ('cp', 'np', 'mp', 'cs', 'ns', 'ms')}
    kv_p = kv_s = None
    for l in range(DEPTH):
        lam_init = 0.8 - 0.6 * math.exp(-0.3 * l)

        def rowwise_pre(x, xb, kv, ffn_w, tm):
            x1, x1b, ffn_w = _ffn_ln(x, xb, ffn_w or (ffn_up, ffn_down), ln_g, ln_b, l, 0, 0, tm, 512)
            proj, pif, kv = _in_proj(x1b, w_cat, w_if, kv, l, min(2 * tm, x1b.shape[0]))
            return x1, proj, pif, kv, ffn_w

        def rowwise_post(x1, proj, ao, mh, ffn_w, tm):
            merged = _merge(ao, mh, proj, wba_b, wbm_b, b_gate3, l, min(2 * tm, ao.shape[0]), 1024)
            x2, x2b = _out_proj_ln(x1, merged, wout_b, ln_g, ln_b, l, tm)
            return _ffn_ln(x2, x2b, ffn_w or (ffn_up, ffn_down), ln_g, ln_b, l, 1, 2, tm, 512)

        s_x1, s_proj, pif, kv_s, ffn_w1 = rowwise_pre(xs, xsb, kv_s, None, tm_s)
        s_ao = _sample_attn(s_proj, cache_k, cache_v, page_table, rel_bias, lam_qk, subln3, l, lam_init,
                            dec_batch, dec_seq, 16)
        gt = _gates_transposed(pif, b_if[l], dec_batch, dec_seq, 128)
        s_mh, c_new, n_new, m_new = _mlstm(s_proj, gt, normw3, state_C[l],
                                           state_n[l].reshape(dec_batch, ML_HEADS, 1, ML_DQK),
                                           state_m[l].reshape(dec_batch, 1, ML_HEADS), l, dec_batch,
                                           dec_seq, dec_seq)
        outs['cs'].append(c_new)
        outs['ns'].append(n_new.reshape(dec_batch, ML_HEADS, ML_DQK))
        outs['ms'].append(m_new.reshape(dec_batch, ML_HEADS))

        x1, proj, pif, kv_p, _ = rowwise_pre(xp, xpb, kv_p, ffn_w1, tm_p)
        ao = _prompt_attn(proj, bias_tiles, lam_qk, subln_col, l, lam_init, batch, seq, tq)
        gt = _gates_transposed(pif, b_if[l], batch, seq, seq)
        mh, c_new, n_new, m_new = _mlstm(proj, gt, normw3, zero_c, zero_n, zero_m, l, batch, seq,
                                         prompt_chunk)
        outs['cp'].append(c_new)
        outs['np'].append(n_new.reshape(batch, ML_HEADS, ML_DQK))
        outs['mp'].append(m_new.reshape(batch, ML_HEADS))

        xs, xsb, ffn_w2 = rowwise_post(s_x1, s_proj, s_ao, s_mh, None, tm_s)
        xp, xpb, _ = rowwise_post(x1, proj, ao, mh, ffn_w2, tm_p)

    st = lambda name: jnp.stack(outs[name])
    kv_shape_p = (DEPTH, batch, seq, DA_HEADS, DA_V_DIM)
    kv_shape_s = (DEPTH, dec_batch, dec_seq, DA_HEADS, DA_V_DIM)
    return (xp.reshape(batch, seq, D_MODEL), xs.reshape(dec_batch, dec_seq, D_MODEL),
            kv_p[0].reshape(kv_shape_p), kv_p[1].reshape(kv_shape_p),
            kv_s[0].reshape(kv_shape_s), kv_s[1].reshape(kv_shape_s), st('cp'), st('np'), st('mp'),
            st('cs'), st('ns'), st('ms'))
```

```python
import functools
import math

import numpy as np
import jax
import jax.numpy as jnp
from jax import lax
from jax.experimental import pallas as pl
from jax.experimental.pallas import tpu as pltpu

F32 = jnp.float32
BF16 = jnp.bfloat16

D_MODEL = 2048
DEPTH = 4
PAGE_SIZE = 128
DA_HEADS = 8
DA_HEAD_DIM = 64
DA_V_DIM = 2 * DA_HEAD_DIM
ML_HEADS = 4
ML_DQK = 256
ML_DV = 256
D_FF = 5632
N_BUCKETS = 32
MAX_DISTANCE = 128
LN_EPS = 1e-5
DN_ALPHA = (2.0 * DEPTH) ** 0.25

ATT_W = DA_HEADS * DA_V_DIM
ML_W = ML_HEADS * ML_DV
N_MAIN = 3 * ATT_W + 4 * ML_W
N_GATE = 2 * D_MODEL
N_IF = 2 * ML_HEADS
N_IF_PAD = 128
N_PROJ = N_MAIN + N_GATE
PROJ_TN = 1024
KV_COL_BLOCK = ATT_W // PROJ_TN
COL_GATE = N_MAIN

NEG_BIG = -1e30
LOG2E = math.log2(math.e)
QCOLS = 512
ACC_PAD = 16
VMEM_LIMIT = 48 * 1024 * 1024


def _cparams(sem):
    return pltpu.CompilerParams(dimension_semantics=sem, vmem_limit_bytes=VMEM_LIMIT)


def _layer_norm_rows(y, g, b):
    mu = jnp.mean(y, axis=-1, keepdims=True)
    yc = y - mu
    var = jnp.mean(yc * yc, axis=-1, keepdims=True)
    return yc * lax.rsqrt(var + LN_EPS) * g + b


def _ffn_kernel(x_ref, xb_ref, wa_ref, wb_ref, wd_ref, g_ref, b_ref, o_ref, ob_ref, *rest):
    acc_ref = rest[-1]
    j = pl.program_id(1)

    @pl.when(j == 0)
    def _():
        acc_ref[...] = jnp.zeros_like(acc_ref)

    wa, wb, wd = wa_ref[...].astype(BF16), wb_ref[...].astype(BF16), wd_ref[...].astype(BF16)
    for w_out_ref, w in zip(rest[:-1], (wa, wb, wd)):
        w_out_ref[...] = w
    xb = xb_ref[...]
    a = jnp.dot(xb, wa, preferred_element_type=F32)
    b = jnp.dot(xb, wb, preferred_element_type=F32)
    h = (a * jax.nn.sigmoid(a) * b).astype(BF16)
    acc_ref[...] += jnp.dot(h, wd, preferred_element_type=F32)

    @pl.when(j == pl.num_programs(1) - 1)
    def _():
        y = DN_ALPHA * x_ref[...] + 0.5 * acc_ref[...]
        o = _layer_norm_rows(y, g_ref[...], b_ref[...])
        o_ref[...] = o
        ob_ref[...] = o.astype(BF16)


def _ffn_ln(x, xb, weights, ln_g, ln_b, layer, which, ln_idx, tm, tf):
    m = x.shape[0]
    nf = D_FF // tf
    assert m == tm or len(weights) == 3, "bf16 weight copies need a single row tile (each tile written once)"
    row_spec = pl.BlockSpec((tm, D_MODEL), lambda i, j: (i, 0))
    ln_spec = pl.BlockSpec((None, None, 1, D_MODEL), lambda i, j: (layer, ln_idx, 0, 0))
    up_spec = pl.BlockSpec((D_MODEL, tf), lambda i, j: (0, j))
    down_spec = pl.BlockSpec((tf, D_MODEL), lambda i, j: (j, 0))
    out_specs = [row_spec, row_spec]
    out_shape = [jax.ShapeDtypeStruct((m, D_MODEL), F32), jax.ShapeDtypeStruct((m, D_MODEL), BF16)]
    if len(weights) == 2:
        w_up, w_down = weights
        w_args = (w_up, w_up, w_down)
        w_specs = [pl.BlockSpec((None, None, D_MODEL, tf), lambda i, j: (layer, which, 0, j)),
                   pl.BlockSpec((None, None, D_MODEL, tf), lambda i, j: (layer, which, 0, j + nf)),
                   pl.BlockSpec((None, None, tf, D_MODEL), lambda i, j: (layer, which, j, 0))]
        out_specs += [up_spec, up_spec, down_spec]
        out_shape += [jax.ShapeDtypeStruct((D_MODEL, D_FF), BF16)] * 2 + [jax.ShapeDtypeStruct((D_FF, D_MODEL), BF16)]
    else:
        w_args = tuple(weights)
        w_specs = [up_spec, up_spec, down_spec]
    outs = pl.pallas_call(
        _ffn_kernel,
        grid=(m // tm, nf),
        in_specs=[row_spec, row_spec] + w_specs + [ln_spec, ln_spec],
        out_specs=out_specs,
        out_shape=out_shape,
        scratch_shapes=[pltpu.VMEM((tm, D_MODEL), F32)],
        compiler_params=_cparams(("parallel", "arbitrary")),
        name="ffn_ln",
    )(x, xb, *w_args, ln_g, ln_b)
    return outs[0], outs[1], tuple(outs[2:])


def _proj_kernel(xb_ref, w_ref, wif_ref, wtail_ref, *rest):
    o_ref, oif_ref, otail_ref, kv_ref = rest[-4:]
    j = pl.program_id(1)
    o = jnp.dot(xb_ref[...], w_ref[...], preferred_element_type=F32)
    o_ref[...] = o

    @pl.when(j == 0)
    def _():
        oif_ref[...] = jnp.dot(xb_ref[...], wif_ref[...], preferred_element_type=F32)
        otail_ref[...] = jnp.dot(xb_ref[...], wtail_ref[...], preferred_element_type=F32)

    @pl.when((j == KV_COL_BLOCK) | (j == KV_COL_BLOCK + 1))
    def _():
        kv_ref[...] = o


def _in_proj(xb, w_cat, w_if, w_tail, kv_buf, layer, tm):
    m = xb.shape[0]
    in_specs = [
        pl.BlockSpec((tm, D_MODEL), lambda i, j: (i, 0)),
        pl.BlockSpec((None, D_MODEL, PROJ_TN), lambda i, j: (layer, 0, j)),
        pl.BlockSpec((None, D_MODEL, N_IF_PAD), lambda i, j: (layer, 0, 0)),
        pl.BlockSpec((None, D_MODEL, N_IF_PAD), lambda i, j: (layer, 0, 0)),
    ]
    args = [xb, w_cat, w_if, w_tail]
    aliases = {}
    if kv_buf is not None:
        in_specs.append(pl.BlockSpec(memory_space=pl.ANY))
        args.append(kv_buf)
        aliases = {4: 3}
    return pl.pallas_call(
        _proj_kernel,
        grid=(m // tm, N_PROJ // PROJ_TN),
        in_specs=in_specs,
        out_specs=[pl.BlockSpec((tm, PROJ_TN), lambda i, j: (i, j)),
                   pl.BlockSpec((tm, N_IF_PAD), lambda i, j: (i, 0)),
                   pl.BlockSpec((tm, N_IF_PAD), lambda i, j: (i, 0)),
                   pl.BlockSpec((None, None, tm, ATT_W),
                                lambda i, j: (jnp.clip(j - KV_COL_BLOCK, 0, 1), layer, i, 0))],
        out_shape=[jax.ShapeDtypeStruct((m, N_PROJ), F32),
                   jax.ShapeDtypeStruct((m, N_IF_PAD), F32),
                   jax.ShapeDtypeStruct((m, N_IF_PAD), F32),
                   jax.ShapeDtypeStruct((2, DEPTH, m, ATT_W), F32)],
        input_output_aliases=aliases,
        compiler_params=_cparams(("parallel", "arbitrary")),
        name="in_proj",
    )(*args)


def _t5_bucket_np(dist):
    n = np.maximum(dist, 0)
    max_exact = N_BUCKETS // 2
    nf = np.maximum(n, 1).astype(np.float32)
    large = max_exact + (np.log(nf / np.float32(max_exact)) / np.float32(math.log(MAX_DISTANCE / max_exact))
                         * np.float32(N_BUCKETS - max_exact)).astype(np.int32)
    large = np.minimum(large, N_BUCKETS - 1)
    return np.where(n < max_exact, n, large).astype(np.int32)


def _lambda_scalar(lq, lam_init):
    s01 = jnp.sum(lq[0:1, :] * lq[1:2, :], axis=1, keepdims=True)
    s23 = jnp.sum(lq[2:3, :] * lq[3:4, :], axis=1, keepdims=True)
    return jnp.exp(s01) - jnp.exp(s23) + lam_init


def _sub_norm(o, w_row, lam_init):
    o = o * lax.rsqrt(jnp.mean(o * o, axis=-1, keepdims=True) + LN_EPS)
    return o * w_row * (1.0 - lam_init)


def _pattn_kernel(qi_tab, ki_tab, q_ref, k_ref, v_ref, bias_ref, lq_ref, sw_ref, o_ref,
                  qt_sc, m_sc, acc_sc, *, tq, lam_init):
    t = pl.program_id(2)
    qi = qi_tab[t]
    ki = ki_tab[t]
    tk = k_ref.shape[0]

    @pl.when(ki == 0)
    def _():
        q = q_ref[...] * (DA_HEAD_DIM ** -0.5 * LOG2E)
        lane = lax.broadcasted_iota(jnp.int32, q.shape, 1)
        qt_sc[:, :tq] = jnp.where(lane < DA_HEAD_DIM, q, 0.0).T.astype(BF16)
        qt_sc[:, tq:] = jnp.where(lane >= DA_HEAD_DIM, q, 0.0).T.astype(BF16)
        m_sc[...] = jnp.full_like(m_sc, -jnp.inf)
        acc_sc[...] = jnp.zeros_like(acc_sc)

    def accumulate(with_bias, diagonal):
        kb = k_ref[...].astype(BF16)
        vt = jnp.concatenate([v_ref[...].T, jnp.ones((ACC_PAD, tk), F32)], axis=0).astype(BF16)
        n_chunks = 2 * tq // QCOLS
        cols = [slice(c * QCOLS, (c + 1) * QCOLS) for c in range(n_chunks)]
        keys = [((c * QCOLS) % tq + QCOLS) if diagonal else tk for c in range(n_chunks)]

        def scores(c):
            s = jnp.dot(kb[:keys[c]], qt_sc[:, cols[c]], preferred_element_type=F32)
            if with_bias:
                b0 = (c * QCOLS) % tq
                s = s + bias_ref[:keys[c], b0:b0 + QCOLS]
            return s

        def softmax(c, s):
            m_old = m_sc[:, cols[c]]
            m_new = jnp.maximum(m_old, jnp.max(s, axis=0, keepdims=True))
            m_sc[:, cols[c]] = m_new
            return jnp.exp2(s - m_new).astype(BF16), jnp.exp2(m_old - m_new)

        def update(c, p, alpha):
            acc_sc[:, cols[c]] = alpha * acc_sc[:, cols[c]] + jnp.dot(vt[:, :keys[c]], p,
                                                                      preferred_element_type=F32)

        s_q = {0: scores(0), 1: scores(1)}
        for c in range(n_chunks):
            p, alpha = softmax(c, s_q.pop(c))
            if c + 2 < n_chunks:
                s_q[c + 2] = scores(c + 2)
            update(c, p, alpha)

    @pl.when(qi == ki)
    def _():
        accumulate(True, True)

    @pl.when(qi - ki == 1)
    def _():
        accumulate(True, False)

    @pl.when(qi - ki > 1)
    def _():
        accumulate(False, False)

    @pl.when(ki == qi)
    def _():
        acc = acc_sc[...]
        o_all = acc[:DA_V_DIM] * (1.0 / acc[DA_V_DIM:DA_V_DIM + 1])
        lam = _lambda_scalar(lq_ref[...], lam_init)
        o = o_all[:, :tq] - lam * o_all[:, tq:]
        o = o * lax.rsqrt(jnp.mean(o * o, axis=0, keepdims=True) + LN_EPS)
        o = o * (sw_ref[...] * (1.0 - lam_init))
        o_ref[...] = o.T.astype(o_ref.dtype)


def _prompt_attn(proj, bias_tiles, lam_qk, subln_col, layer, lam_init, batch, seq, tq):
    nq = seq // tq
    tri = [(qi, ki) for qi in range(nq) for ki in range(qi + 1)]
    qi_tab = jnp.asarray([a for a, _ in tri], jnp.int32)
    ki_tab = jnp.asarray([b for _, b in tri], jnp.int32)
    hq, hk, hv = 0, ATT_W // DA_V_DIM, 2 * ATT_W // DA_V_DIM

    def bias_idx(b, h, t, qt, kt):
        return (h, jnp.minimum(qt[t] - kt[t], 1), 0, 0)

    grid_spec = pltpu.PrefetchScalarGridSpec(
        num_scalar_prefetch=2,
        grid=(batch, DA_HEADS, len(tri)),
        in_specs=[
            pl.BlockSpec((tq, DA_V_DIM), lambda b, h, t, qt, kt: (b * nq + qt[t], hq + h)),
            pl.BlockSpec((tq, DA_V_DIM), lambda b, h, t, qt, kt: (b * nq + kt[t], hk + h)),
            pl.BlockSpec((tq, DA_V_DIM), lambda b, h, t, qt, kt: (b * nq + kt[t], hv + h)),
            pl.BlockSpec((None, None, tq, tq), bias_idx),
            pl.BlockSpec((None, 4, DA_HEAD_DIM), lambda b, h, t, qt, kt: (layer, 0, 0)),
            pl.BlockSpec((None, DA_V_DIM, 1), lambda b, h, t, qt, kt: (layer, 0, 0)),
        ],
        out_specs=pl.BlockSpec((tq, DA_V_DIM), lambda b, h, t, qt, kt: (b * nq + qt[t], h)),
        scratch_shapes=[pltpu.VMEM((DA_V_DIM, 2 * tq), BF16), pltpu.VMEM((1, 2 * tq), F32),
                        pltpu.VMEM((DA_V_DIM + ACC_PAD, 2 * tq), F32)],
    )
    return pl.pallas_call(
        functools.partial(_pattn_kernel, tq=tq, lam_init=lam_init),
        grid_spec=grid_spec,
        out_shape=jax.ShapeDtypeStruct((batch * seq, ATT_W), BF16),
        compiler_params=_cparams(("parallel", "parallel", "arbitrary")),
        name="prompt_attn",
    )(qi_tab, ki_tab, proj, proj, proj, bias_tiles, lam_qk, subln_col)


def _skew_kernel(y_ref, o_ref, *, tq, rows):
    for r0 in range(0, tq, rows):
        x = jnp.broadcast_to(y_ref[...], (rows, 2 * tq))
        o_ref[r0:r0 + rows, :] = pltpu.roll(x, r0, 1, stride=1, stride_axis=0)[:, :tq]


def _prompt_bias_tiles(rel_bias, tq):
    assert tq >= MAX_DISTANCE and np.all(_t5_bucket_np(np.arange(MAX_DISTANCE, 1 << 16)) == N_BUCKETS - 1)
    span = 2 * tq
    d_row = np.zeros(span, np.int64)
    d_row[:tq] = np.arange(tq)
    d_row[span - np.arange(1, tq)] = -np.arange(1, tq)
    far = rel_bias[N_BUCKETS - 1]
    rows = []
    for delta in range(2):
        dist = d_row + delta * tq
        y = (rel_bias[_t5_bucket_np(dist)] - far) * LOG2E
        rows.append(jnp.where(jnp.asarray(dist >= 0)[:, None], y, NEG_BIG).T)
    y = jnp.stack(rows, axis=1).reshape(DA_HEADS, 2, 1, span)
    return pl.pallas_call(
        functools.partial(_skew_kernel, tq=tq, rows=256),
        grid=(DA_HEADS, 2),
        in_specs=[pl.BlockSpec((None, None, 1, span), lambda h, d: (h, d, 0, 0))],
        out_specs=pl.BlockSpec((None, None, tq, tq), lambda h, d: (h, d, 0, 0)),
        out_shape=jax.ShapeDtypeStruct((DA_HEADS, 2, tq, tq), F32),
        compiler_params=_cparams(("parallel", "parallel")),
        name="bias_skew",
    )(y)


def _page_rows(page_ref):
    return jnp.concatenate([page_ref[pl.ds(h, PAGE_SIZE, stride=DA_HEADS), :] for h in range(DA_HEADS)],
                           axis=1).astype(BF16)


def _sattn_kernel(pt_ref, w_ref, kn_ref, vn_ref, bl_ref, bn_ref, bf_ref, lq_ref, sw_ref, *rest,
                  n_group, n_pages, n_tok, lam_init):
    k_refs = rest[:n_group]
    v_refs = rest[n_group:2 * n_group]
    o_ref = rest[2 * n_group]
    m_sc, l_sc, acc_sc = rest[2 * n_group + 1:]
    step = pl.program_id(1)
    n_steps = pl.num_programs(1)
    hcols = 2 * n_tok
    w = w_ref[...]

    @pl.when(step == 0)
    def _():
        m_sc[...] = jnp.full_like(m_sc, -jnp.inf)
        l_sc[...] = jnp.zeros_like(l_sc)
        acc_sc[...] = jnp.zeros_like(acc_sc)

    def online_update(s, v_rows):
        m_old = m_sc[...]
        m_new = jnp.maximum(m_old, jnp.max(s, axis=0, keepdims=True))
        p = jnp.exp2(s - m_new)
        alpha = jnp.exp2(m_old - m_new)
        l_sc[...] = alpha * l_sc[...] + jnp.sum(p, axis=0, keepdims=True)
        m_sc[...] = m_new
        pt = p.T.astype(BF16)
        alpha_col = jnp.broadcast_to(alpha, (8, alpha.shape[1])).T[:, 0:1]
        for h in range(DA_HEADS):
            rows = slice(h * hcols, (h + 1) * hcols)
            d = jnp.dot(pt[rows], v_rows[:, h * DA_V_DIM:(h + 1) * DA_V_DIM], preferred_element_type=F32)
            acc_sc[h] = alpha_col[rows] * acc_sc[h] + d

    k_rows = jnp.concatenate([_page_rows(k_refs[g]) for g in range(n_group)], axis=0)
    s = jnp.dot(k_rows, w, preferred_element_type=F32)
    tail = jnp.where(step == n_steps - 1, bl_ref[...], bf_ref[...])
    s = s + jnp.concatenate([bf_ref[...]] * (n_group - 1) + [tail], axis=0)
    online_update(s, jnp.concatenate([_page_rows(v_refs[g]) for g in range(n_group)], axis=0))

    @pl.when(step == n_steps - 1)
    def _():
        pad = jnp.zeros((2 * n_tok - n_tok, kn_ref.shape[1]), F32)
        kn = jnp.concatenate([kn_ref[...], pad], axis=0).astype(BF16)
        vn = jnp.concatenate([vn_ref[...], pad], axis=0).astype(BF16)
        online_update(jnp.dot(kn, w, preferred_element_type=F32) + bn_ref[...], vn)
        lam = _lambda_scalar(lq_ref[...], lam_init)
        inv_l = jnp.broadcast_to(1.0 / l_sc[...], (8, l_sc.shape[1])).T[:, 0:1]
        for h in range(DA_HEADS):
            o = acc_sc[h] * inv_l[h * hcols:(h + 1) * hcols]
            o = o[:n_tok] - lam * o[n_tok:]
            o_ref[:, h * DA_V_DIM:(h + 1) * DA_V_DIM] = _sub_norm(o, sw_ref[...], lam_init).astype(o_ref.dtype)


def _sample_attn(proj, cache_k, cache_v, page_table, rel_bias, lam_qk, subln_w, layer, lam_init,
                 dec_batch, n_tok, n_group):
    n_pages = page_table.shape[1]
    assert n_pages % n_group == 0 and 2 * DA_HEADS * n_tok == DA_V_DIM
    past = n_pages * PAGE_SIZE
    n_steps = n_pages // n_group
    ncol = 2 * DA_HEADS * n_tok
    page_rows = PAGE_SIZE * DA_HEADS

    q = proj[:, :ATT_W].reshape(dec_batch, n_tok, DA_HEADS, 2, DA_HEAD_DIM) * (DA_HEAD_DIM ** -0.5 * LOG2E)
    w = jnp.einsum('bthjd,hH,jJ->bhjdHJt', q, jnp.eye(DA_HEADS, dtype=F32), jnp.eye(2, dtype=F32))
    w = w.reshape(dec_batch, ATT_W, ncol).astype(BF16)

    tok = np.arange(n_tok)

    def col_bias(dist, pad_rows=0):
        b = rel_bias[_t5_bucket_np(dist)] * LOG2E
        b = jnp.where(jnp.asarray(dist >= 0)[..., None], b, NEG_BIG)
        b = jnp.transpose(b, (0, 2, 1))[:, :, None, :]
        b = jnp.broadcast_to(b, (dist.shape[0], DA_HEADS, 2, n_tok)).reshape(dist.shape[0], ncol)
        if pad_rows:
            b = jnp.concatenate([b, jnp.full((pad_rows, ncol), NEG_BIG, F32)], axis=0)
        return b

    assert np.all(_t5_bucket_np(np.arange(PAGE_SIZE + 1, past + n_tok + 1)) == N_BUCKETS - 1)
    bias_last = col_bias((past + tok)[None, :] - (past - PAGE_SIZE + np.arange(PAGE_SIZE))[:, None])
    bias_new = col_bias(tok[None, :] - tok[:, None], pad_rows=n_tok)
    bias_far = col_bias(np.full((PAGE_SIZE, n_tok), PAGE_SIZE + 1))

    page_block = (None, None, page_rows, DA_V_DIM)
    cache_k = cache_k.reshape(cache_k.shape[0], cache_k.shape[1], page_rows, DA_V_DIM)
    cache_v = cache_v.reshape(cache_v.shape[0], cache_v.shape[1], page_rows, DA_V_DIM)

    def page_idx(g):
        return lambda b, s, pt: (layer, pt[b, s * n_group + g], 0, 0)

    const2 = lambda b, s, pt: (0, 0)
    grid_spec = pltpu.PrefetchScalarGridSpec(
        num_scalar_prefetch=1,
        grid=(dec_batch, n_steps),
        in_specs=[
            pl.BlockSpec((None, ATT_W, ncol), lambda b, s, pt: (b, 0, 0)),
            pl.BlockSpec((n_tok, ATT_W), lambda b, s, pt: (b, 1)),
            pl.BlockSpec((n_tok, ATT_W), lambda b, s, pt: (b, 2)),
            pl.BlockSpec((PAGE_SIZE, ncol), const2),
            pl.BlockSpec((2 * n_tok, ncol), const2),
            pl.BlockSpec((PAGE_SIZE, ncol), const2),
            pl.BlockSpec((None, 4, DA_HEAD_DIM), lambda b, s, pt: (layer, 0, 0)),
            pl.BlockSpec((None, 1, DA_V_DIM), lambda b, s, pt: (layer, 0, 0)),
        ] + [pl.BlockSpec(page_block, page_idx(g)) for g in range(n_group)] * 2,
        out_specs=pl.BlockSpec((n_tok, ATT_W), lambda b, s, pt: (b, 0)),
        scratch_shapes=[pltpu.VMEM((1, ncol), F32), pltpu.VMEM((1, ncol), F32),
                        pltpu.VMEM((DA_HEADS, 2 * n_tok, DA_V_DIM), F32)],
    )
    return pl.pallas_call(
        functools.partial(_sattn_kernel, n_group=n_group, n_pages=n_pages, n_tok=n_tok,
                          lam_init=lam_init),
        grid_spec=grid_spec,
        out_shape=jax.ShapeDtypeStruct((dec_batch * n_tok, ATT_W), BF16),
        compiler_params=_cparams(("parallel", "arbitrary")),
        name="sample_attn",
    )(page_table, w, proj, proj, bias_last, bias_new, bias_far, lam_qk, subln_w,
      *([cache_k] * n_group), *([cache_v] * n_group))


def _mlstm_kernel(q_ref, k_ref, v_ref, o_gate_ref, g_ref, nw_ref, c0_ref, n0_ref, m0_ref,
                  h_ref, c_ref, n_ref, m_ref, *, rows, chunk):
    c_idx = pl.program_id(1)

    @pl.when(c_idx == 0)
    def _():
        c_ref[...] = c0_ref[...]
        n_ref[...] = n0_ref[...]
        m_ref[...] = m0_ref[...]

    row = lax.broadcasted_iota(jnp.int32, (chunk, chunk), 0)
    col = lax.broadcasted_iota(jnp.int32, (chunk, chunk), 1)
    tril = row >= col
    eye = row == col

    def padded(x):
        if rows == chunk:
            return x
        return jnp.concatenate([x, jnp.zeros((chunk - rows, x.shape[1]), x.dtype)], axis=0)

    for h in range(ML_HEADS):
        sl = slice(h * ML_DQK, (h + 1) * ML_DQK)
        q = padded(q_ref[:, sl])
        k = padded(k_ref[:, sl]) * (ML_DQK ** -0.5)
        v = padded(v_ref[:, sl])
        ig = g_ref[h:h + 1, :]
        gf = g_ref[ML_HEADS + h:ML_HEADS + h + 1, :]
        lf = jnp.minimum(gf, 0.0) - jnp.log1p(jnp.exp(-jnp.abs(gf)))
        c_state = c_ref[h]
        n_state = n_ref[h]
        m_state = m_ref[:, h:h + 1]

        b_col = jnp.sum(jnp.where(tril, lf, 0.0), axis=1, keepdims=True)
        b_row = jnp.sum(jnp.where(eye, b_col, 0.0), axis=0, keepdims=True)
        logw = jnp.where(tril, b_col - b_row + ig, NEG_BIG)
        m_inter = b_col + m_state
        m_t = jnp.maximum(jnp.max(logw, axis=1, keepdims=True), m_inter)
        inter = jnp.exp(m_inter - m_t)
        qb = q.astype(BF16)
        kb = k.astype(BF16)
        s = lax.dot_general(qb, kb, (((1,), (1,)), ((), ())), preferred_element_type=F32)
        s = s * jnp.exp(logw - m_t)
        num = jnp.dot(s.astype(BF16), v.astype(BF16), preferred_element_type=F32)
        num = num + inter * lax.dot_general(qb, c_state.astype(BF16), (((1,), (1,)), ((), ())),
                                            preferred_element_type=F32)
        den = jnp.sum(s, axis=1, keepdims=True) + inter * jnp.sum(q * n_state, axis=1, keepdims=True)
        hh = num / jnp.maximum(jnp.abs(den), jnp.exp(-m_t))

        b_last = jnp.sum(lf, axis=1, keepdims=True)
        logw_end = b_last - b_row + ig
        m_new = jnp.maximum(b_last + m_state, jnp.max(logw_end, axis=1, keepdims=True))
        w_end = jnp.exp(logw_end - m_new)
        decay = jnp.exp(b_last + m_state - m_new)
        w_col = jnp.sum(jnp.where(eye, w_end, 0.0), axis=1, keepdims=True)
        vw = (v * w_col).astype(BF16)
        c_ref[h] = decay * c_state + lax.dot_general(vw, kb, (((0,), (0,)), ((), ())),
                                                     preferred_element_type=F32)
        n_ref[h] = decay * n_state + jnp.sum(k * w_col, axis=0, keepdims=True)
        m_ref[:, h:h + 1] = m_new

        hh = hh[:rows]
        mc = hh - jnp.mean(hh, axis=1, keepdims=True)
        y = mc * lax.rsqrt(jnp.mean(mc * mc, axis=1, keepdims=True) + LN_EPS) * nw_ref[:, sl]
        h_ref[:, sl] = (jax.nn.sigmoid(o_gate_ref[:, sl]) * y).astype(h_ref.dtype)


def _mlstm(proj, gates_t, norm_w, c0, n0, m0, layer, batch, seq, rows_per_block):
    nc = seq // rows_per_block
    chunk = gates_t.shape[2] // nc
    cq, ck, cv, co = (3 * ATT_W // ML_W, 3 * ATT_W // ML_W + 1, 3 * ATT_W // ML_W + 2,
                      3 * ATT_W // ML_W + 3)
    rows = lambda col: pl.BlockSpec((rows_per_block, ML_W), lambda b, c: (b * nc + c, col))
    state_c = pl.BlockSpec((None, ML_HEADS, ML_DV, ML_DQK), lambda b, c: (b, 0, 0, 0))
    state_n = pl.BlockSpec((None, ML_HEADS, 1, ML_DQK), lambda b, c: (b, 0, 0, 0))
    state_m = pl.BlockSpec((None, 1, ML_HEADS), lambda b, c: (b, 0, 0))
    return pl.pallas_call(
        functools.partial(_mlstm_kernel, rows=rows_per_block, chunk=chunk),
        grid=(batch, nc),
        in_specs=[rows(cq), rows(ck), rows(cv), rows(co),
                  pl.BlockSpec((None, 2 * ML_HEADS, chunk), lambda b, c: (b, 0, c)),
                  pl.BlockSpec((None, 1, ML_W), lambda b, c: (layer, 0, 0)),
                  state_c, state_n, state_m],
        out_specs=[pl.BlockSpec((rows_per_block, ML_W), lambda b, c: (b * nc + c, 0)),
                   state_c, state_n, state_m],
        out_shape=[jax.ShapeDtypeStruct((batch * seq, ML_W), BF16),
                   jax.ShapeDtypeStruct((batch, ML_HEADS, ML_DV, ML_DQK), F32),
                   jax.ShapeDtypeStruct((batch, ML_HEADS, 1, ML_DQK), F32),
                   jax.ShapeDtypeStruct((batch, 1, ML_HEADS), F32)],
        compiler_params=_cparams(("parallel", "arbitrary")),
        name="mlstm",
    )(proj, proj, proj, proj, gates_t, norm_w, c0, n0, m0)


def _shift_left(block, nxt):
    n = block.shape[1]
    r = pltpu.roll(block, n - N_IF, 1)
    rx = pltpu.roll(nxt, N_IF_PAD - N_IF, 1)
    lane = lax.broadcasted_iota(jnp.int32, rx.shape, 1)
    tail = jnp.where(lane < N_IF_PAD - N_IF, r[:, n - N_IF_PAD:], rx)
    return jnp.concatenate([r[:, :n - N_IF_PAD], tail], axis=1)


def _merge_kernel(ao_ref, mh_ref, wa_ref, wm_ref, ga_ref, gm_ref, na_ref, nm_ref, tail_ref, ba_ref, bm_ref,
                  o_ref):
    last = pl.program_id(1) == pl.num_programs(1) - 1
    ta = jnp.dot(ao_ref[...], wa_ref[...], preferred_element_type=F32)
    tmm = jnp.dot(mh_ref[...], wm_ref[...], preferred_element_type=F32)
    ga = jax.nn.sigmoid(_shift_left(ga_ref[...], na_ref[...]) + ba_ref[...])
    nm = jnp.where(last, tail_ref[...], nm_ref[...])
    gm = jax.nn.sigmoid(_shift_left(gm_ref[...], nm) + bm_ref[...])
    o_ref[...] = (ga * ta + gm * tmm).astype(o_ref.dtype)


def _merge(ao, mh, proj, ptail, w_ba, w_bm, b_gate, layer, tm, tn):
    m = ao.shape[0]
    nd = D_MODEL // tn
    ga0 = COL_GATE // tn
    nxt = tn // N_IF_PAD
    nx0 = COL_GATE // N_IF_PAD
    return pl.pallas_call(
        _merge_kernel,
        grid=(m // tm, nd),
        in_specs=[
            pl.BlockSpec((tm, ATT_W), lambda i, j: (i, 0)),
            pl.BlockSpec((tm, ML_W), lambda i, j: (i, 0)),
            pl.BlockSpec((None, ATT_W, tn), lambda i, j: (layer, 0, j)),
            pl.BlockSpec((None, ML_W, tn), lambda i, j: (layer, 0, j)),
            pl.BlockSpec((tm, tn), lambda i, j: (i, ga0 + j)),
            pl.BlockSpec((tm, tn), lambda i, j: (i, ga0 + nd + j)),
            pl.BlockSpec((tm, N_IF_PAD), lambda i, j: (i, nx0 + (j + 1) * nxt)),
            pl.BlockSpec((tm, N_IF_PAD), lambda i, j: (i, nx0 + jnp.minimum(nd + j + 1, 2 * nd - 1) * nxt)),
            pl.BlockSpec((tm, N_IF_PAD), lambda i, j: (i, 0)),
            pl.BlockSpec((None, 1, tn), lambda i, j: (layer, 0, j)),
            pl.BlockSpec((None, 1, tn), lambda i, j: (layer, 0, nd + j)),
        ],
        out_specs=pl.BlockSpec((tm, tn), lambda i, j: (i, j)),
        out_shape=jax.ShapeDtypeStruct((m, D_MODEL), BF16),
        compiler_params=_cparams(("parallel", "parallel")),
        name="merge",
    )(ao, mh, w_ba, w_bm, proj, proj, proj, proj, ptail, b_gate, b_gate)


def _out_kernel(x_ref, mg_ref, w_ref, g_ref, b_ref, o_ref, ob_ref):
    y = DN_ALPHA * x_ref[...] + jnp.dot(mg_ref[...], w_ref[...], preferred_element_type=F32)
    o = _layer_norm_rows(y, g_ref[...], b_ref[...])
    o_ref[...] = o
    ob_ref[...] = o.astype(BF16)


def _out_proj_ln(x, merged, w_out, ln_g, ln_b, layer, tm):
    m = x.shape[0]
    return pl.pallas_call(
        _out_kernel,
        grid=(m // tm,),
        in_specs=[
            pl.BlockSpec((tm, D_MODEL), lambda i: (i, 0)),
            pl.BlockSpec((tm, D_MODEL), lambda i: (i, 0)),
            pl.BlockSpec((None, D_MODEL, D_MODEL), lambda i: (layer, 0, 0)),
            pl.BlockSpec((None, None, 1, D_MODEL), lambda i: (layer, 1, 0, 0)),
            pl.BlockSpec((None, None, 1, D_MODEL), lambda i: (layer, 1, 0, 0)),
        ],
        out_specs=[pl.BlockSpec((tm, D_MODEL), lambda i: (i, 0)),
                   pl.BlockSpec((tm, D_MODEL), lambda i: (i, 0))],
        out_shape=[jax.ShapeDtypeStruct((m, D_MODEL), F32),
                   jax.ShapeDtypeStruct((m, D_MODEL), BF16)],
        compiler_params=_cparams(("parallel",)),
        name="out_proj_ln",
    )(x, merged, w_out, ln_g, ln_b)


def _gates_transposed(pif, b_if, batch, seq, pad_to):
    gif = pif[:, :N_IF] + b_if
    gt = jnp.transpose(gif.reshape(batch, seq, N_IF), (0, 2, 1))
    if pad_to > seq:
        pad = jnp.concatenate([jnp.full((batch, ML_HEADS, pad_to - seq), NEG_BIG, F32),
                               jnp.full((batch, ML_HEADS, pad_to - seq), 1e4, F32)], axis=1)
        gt = jnp.concatenate([gt, pad], axis=2)
    return gt


def kernel(x_prompt, x_sample, cache_k, cache_v, page_table, state_C, state_n, state_m, rel_bias,
           w_in, b_gate, b_if, lam_qk, subln_w, mlstm_norm_w, w_branch_attn, w_branch_mlstm,
           w_out, ffn_up, ffn_down, ln_g, ln_b):
    batch, seq, _ = x_prompt.shape
    dec_batch, dec_seq, _ = x_sample.shape
    mp, ms = batch * seq, dec_batch * dec_seq
    tm_p, tm_s = 512, ms
    tq = 1024
    prompt_chunk = 256

    w_cat = w_in[:, :, :N_PROJ].astype(BF16)
    w_tail = jnp.pad(w_in[:, :, N_PROJ:], ((0, 0), (0, 0), (0, N_IF_PAD - N_IF))).astype(BF16)
    w_if = w_in[:, :, N_MAIN:N_MAIN + N_IF_PAD].astype(BF16)
    wba_b = w_branch_attn.astype(BF16)
    wbm_b = w_branch_mlstm.astype(BF16)
    wout_b = w_out.astype(BF16)
    ln_g = ln_g.reshape(DEPTH, 3, 1, D_MODEL)
    ln_b = ln_b.reshape(DEPTH, 3, 1, D_MODEL)
    b_gate3 = b_gate.reshape(DEPTH, 1, N_GATE)
    subln3 = subln_w.reshape(DEPTH, 1, DA_V_DIM)
    subln_col = subln_w.reshape(DEPTH, DA_V_DIM, 1)
    normw3 = mlstm_norm_w.reshape(DEPTH, 1, ML_W)
    bias_tiles = _prompt_bias_tiles(rel_bias, tq)

    zero_c = jnp.zeros((batch, ML_HEADS, ML_DV, ML_DQK), F32)
    zero_n = jnp.zeros((batch, ML_HEADS, 1, ML_DQK), F32)
    zero_m = jnp.zeros((batch, 1, ML_HEADS), F32)

    xp = x_prompt.reshape(mp, D_MODEL)
    xs = x_sample.reshape(ms, D_MODEL)
    xpb = xp.astype(BF16)
    xsb = xs.astype(BF16)

    outs = {name: [] for name in ('cp', 'np', 'mp', 'cs', 'ns', 'ms')}
    kv_p = kv_s = None
    for l in range(DEPTH):
        lam_init = 0.8 - 0.6 * math.exp(-0.3 * l)

        def rowwise_pre(x, xb, kv, ffn_w, tm):
            x1, x1b, ffn_w = _ffn_ln(x, xb, ffn_w or (ffn_up, ffn_down), ln_g, ln_b, l, 0, 0, tm, 512)
            proj, pif, ptail, kv = _in_proj(x1b, w_cat, w_if, w_tail, kv, l, min(2 * tm, x1b.shape[0]))
            return x1, (proj, ptail), pif, kv, ffn_w

        def rowwise_post(x1, projs, ao, mh, ffn_w, tm):
            merged = _merge(ao, mh, *projs, wba_b, wbm_b, b_gate3, l, tm, 1024)
            x2, x2b = _out_proj_ln(x1, merged, wout_b, ln_g, ln_b, l, tm)
            return _ffn_ln(x2, x2b, ffn_w or (ffn_up, ffn_down), ln_g, ln_b, l, 1, 2, tm, 512)

        s_x1, s_projs, pif, kv_s, ffn_w1 = rowwise_pre(xs, xsb, kv_s, None, tm_s)
        s_proj = s_projs[0]
        s_ao = _sample_attn(s_proj, cache_k, cache_v, page_table, rel_bias, lam_qk, subln3, l, lam_init,
                            dec_batch, dec_seq, 16)
        gt = _gates_transposed(pif, b_if[l], dec_batch, dec_seq, 128)
        s_mh, c_new, n_new, m_new = _mlstm(s_proj, gt, normw3, state_C[l],
                                           state_n[l].reshape(dec_batch, ML_HEADS, 1, ML_DQK),
                                           state_m[l].reshape(dec_batch, 1, ML_HEADS), l, dec_batch,
                                           dec_seq, dec_seq)
        outs['cs'].append(c_new)
        outs['ns'].append(n_new.reshape(dec_batch, ML_HEADS, ML_DQK))
        outs['ms'].append(m_new.reshape(dec_batch, ML_HEADS))

        x1, projs, pif, kv_p, _ = rowwise_pre(xp, xpb, kv_p, ffn_w1, tm_p)
        proj = projs[0]
        ao = _prompt_attn(proj, bias_tiles, lam_qk, subln_col, l, lam_init, batch, seq, tq)
        gt = _gates_transposed(pif, b_if[l], batch, seq, seq)
        mh, c_new, n_new, m_new = _mlstm(proj, gt, normw3, zero_c, zero_n, zero_m, l, batch, seq,
                                         prompt_chunk)
        outs['cp'].append(c_new)
        outs['np'].append(n_new.reshape(batch, ML_HEADS, ML_DQK))
        outs['mp'].append(m_new.reshape(batch, ML_HEADS))

        xs, xsb, ffn_w2 = rowwise_post(s_x1, s_projs, s_ao, s_mh, None, tm_s)
        xp, xpb, _ = rowwise_post(x1, projs, ao, mh, ffn_w2, tm_p)

    st = lambda name: jnp.stack(outs[name])
    kv_shape_p = (DEPTH, batch, seq, DA_HEADS, DA_V_DIM)
    kv_shape_s = (DEPTH, dec_batch, dec_seq, DA_HEADS, DA_V_DIM)
    return (xp.reshape(batch, seq, D_MODEL), xs.reshape(dec_batch, dec_seq, D_MODEL),
            kv_p[0].reshape(kv_shape_p), kv_p[1].reshape(kv_shape_p),
            kv_s[0].reshape(kv_shape_s), kv_s[1].reshape(kv_shape_s), st('cp'), st('np'), st('mp'),
            st('cs'), st('ns'), st('ms'))
```
